```python
import math
import jax, jax.numpy as jnp
from jax import lax
import numpy as np

D_MODEL = 1024
BATCH = 8
SEQ = 8192
DEPTH = 2

GRID_W = 64
CTX_LEN = 256
N_HEADS = 8
N_KV_HEADS = 2
HEAD_DIM = D_MODEL // N_HEADS
GQA_GROUP = N_HEADS // N_KV_HEADS
ROPE_THETA = 10000.0
Q_BLOCK = 128
Q_W = N_HEADS * HEAD_DIM
KV_W = N_KV_HEADS * HEAD_DIM
HY_WIDTH = D_MODEL
HY_SHORT = 3
HY_EMB = 33
HY_BANDS = (HY_EMB - 1) // 2
HY_FILTER_HIDDEN = 64
HY_TARGET = 1e-2
HY_FAST_PCT = 0.3
HY_SLOW_PCT = 1.5
HY_MAX_DECAY = math.log(HY_TARGET) / HY_FAST_PCT
HY_MIN_DECAY = math.log(HY_TARGET) / HY_SLOW_PCT
HY_SHIFT = 0.05
POOL_WIDTH = D_MODEL
POOL_WINDOWS = (2, 4, 8, 16)
POOL_GROUP = POOL_WIDTH // len(POOL_WINDOWS)
N_BRANCH = 3
D_FF = -(-8 * D_MODEL // (3 * 256)) * 256
DN_ALPHA = (2 * DEPTH) ** 0.25
DN_BETA = (8 * DEPTH) ** -0.25
EPS = 1e-6

C_Q = 0
C_K = C_Q + Q_W
C_V = C_K + KV_W
C_HY = C_V + KV_W
C_POOL = C_HY + 3 * HY_WIDTH
C_GATE = C_POOL + POOL_WIDTH
IN_WIDTH = C_GATE + N_BRANCH * D_MODEL

kernel_name = 'hybrid_hyena_gqa_pool_dit_block'


def layer_norm(x, g, b):
    xf = x.astype(jnp.float32)
    mu = jnp.mean(xf, axis=-1, keepdims=True)
    var = jnp.mean(jnp.square(xf - mu), axis=-1, keepdims=True)
    return ((xf - mu) * lax.rsqrt(var + EPS) * g + b).astype(x.dtype)


def rms_norm(x, g):
    xf = x.astype(jnp.float32)
    y = xf * lax.rsqrt(jnp.mean(jnp.square(xf), axis=-1, keepdims=True) + EPS)
    return (y * g).astype(x.dtype)


def axial_rope(x, rows, cols):
    half = HEAD_DIM // 2
    quarter = half // 2
    inv = jnp.power(ROPE_THETA, -jnp.arange(quarter, dtype=jnp.float32) / quarter)

    def rot(xh, pos):
        ang = pos.astype(jnp.float32)[:, None] * inv[None, :]
        cos = jnp.cos(ang)[None, :, None, :]
        sin = jnp.sin(ang)[None, :, None, :]
        a, b = xh[..., :quarter], xh[..., quarter:]
        return jnp.concatenate([a * cos - b * sin, b * cos + a * sin], axis=-1)

    xf = x.astype(jnp.float32)
    return jnp.concatenate([rot(xf[..., :half], rows), rot(xf[..., half:], cols)], axis=-1).astype(x.dtype)


def queries(p, q_norm_g):
    B, L = p.shape[0], p.shape[1]
    return rms_norm(p[..., C_Q:C_K].reshape(B, L, N_HEADS, HEAD_DIM), q_norm_g)


def kv_heads(pkv, k_norm_g):
    B, L = pkv.shape[0], pkv.shape[1]
    k = rms_norm(pkv[..., :KV_W].reshape(B, L, N_KV_HEADS, HEAD_DIM), k_norm_g)
    v = pkv[..., KV_W:].reshape(B, L, N_KV_HEADS, HEAD_DIM)
    return k, v


def block_attention(q, k, v):
    B, Lq = q.shape[0], q.shape[1]
    nb = Lq // Q_BLOCK
    qb = q.reshape(B, nb, Q_BLOCK, N_KV_HEADS, GQA_GROUP, HEAD_DIM).transpose(1, 0, 2, 3, 4, 5)
    kf = k.astype(jnp.float32)
    vf = v.astype(jnp.float32)
    scale = HEAD_DIM ** -0.5

    def one_block(qi):
        s = jnp.einsum('bqkgd,btkd->bkgqt', qi.astype(jnp.float32), kf) * scale
        pr = jax.nn.softmax(s, axis=-1)
        return jnp.einsum('bkgqt,btkd->bqkgd', pr, vf)

    o = lax.map(one_block, qb)
    return o.transpose(1, 0, 2, 3, 4, 5).reshape(B, Lq, Q_W).astype(q.dtype)


def hyena_kernel(L, w1, b1, freq, w2, b2, w3):
    f32 = jnp.float32
    t_idx = jnp.arange(L, dtype=f32)
    t01 = t_idx / max(L - 1, 1)
    bands = jnp.linspace(1e-4, HY_BANDS - 1, HY_BANDS, dtype=f32)
    ang = (2.0 * math.pi / L) * t_idx[:, None] * bands[None, :]
    feats = jnp.concatenate([t01[:, None], jnp.cos(ang), -jnp.sin(ang)], axis=-1)
    h = jnp.sin(freq * (feats @ w1 + b1))
    h = jnp.sin(freq * (h @ w2 + b2))
    h = (h @ w3).astype(f32)
    deltas = jnp.abs(jnp.linspace(HY_MIN_DECAY, HY_MAX_DECAY, HY_WIDTH, dtype=f32))
    window = jnp.exp(-t01[:, None] * deltas[None, :]) + HY_SHIFT
    h_fwd = h[:, :HY_WIDTH] * window
    h_bwd = h[:, HY_WIDTH:] * window
    kern = jnp.concatenate([h_fwd, jnp.zeros((1, HY_WIDTH), f32), h_bwd[:0:-1]], axis=0)
    return kern * lax.rsqrt(jnp.sum(jnp.square(kern), axis=0, keepdims=True) + EPS)


def hyena_mixer(u, lp):
    B, L, _ = u.shape
    pad = HY_SHORT // 2
    up = jnp.pad(u, ((0, 0), (pad, HY_SHORT - 1 - pad), (0, 0)))
    s = lp['hy_conv_b']
    for j in range(HY_SHORT):
        s = s + up[:, j:j + L] * lp['hy_conv_w'][j]
    v, x0, x1 = jnp.split(s, 3, axis=-1)
    z = (v * x1).astype(jnp.float32)
    kern = hyena_kernel(L, lp['hf_w1'], lp['hf_b1'], lp['hf_freq'], lp['hf_w2'], lp['hf_b2'], lp['hf_w3'])
    n = 2 * L
    zf = jnp.fft.rfft(z, n=n, axis=1)
    kf = jnp.fft.rfft(kern, n=n, axis=0)
    y = jnp.fft.irfft(zf * kf[None], n=n, axis=1)[:, :L]
    y = y + z * lp['hy_d'].astype(jnp.float32)
    return (y * x0.astype(jnp.float32)).astype(u.dtype)


def pool_mixer(u, pool_w, pool_scale):
    B, L, C = u.shape
    uf = u.astype(jnp.float32)
    cs = jnp.concatenate([jnp.zeros((B, 1, C), jnp.float32), jnp.cumsum(uf, axis=1)], axis=1)
    t = jnp.arange(L)
    outs = []
    for gi, w in enumerate(POOL_WINDOWS):
        before = w // 2
        after = w - 1 - before
        lo = jnp.clip(t - before, 0, L)
        hi = jnp.clip(t + after + 1, 0, L)
        csg = cs[..., gi * POOL_GROUP:(gi + 1) * POOL_GROUP]
        cnt = (hi - lo).astype(jnp.float32)[None, :, None]
        mean = (csg[:, hi] - csg[:, lo]) / cnt
        outs.append(mean - uf[..., gi * POOL_GROUP:(gi + 1) * POOL_GROUP])
    m = jnp.stack(outs, axis=2)
    y = jnp.einsum('blgc,gcd->blgd', m, pool_w.astype(jnp.float32)).reshape(B, L, C)
    return (y * pool_scale).astype(u.dtype)


def stream_mixer(p, q, k_all, v_all, lp):
    attn = block_attention(q, k_all, v_all)
    hy = hyena_mixer(p[..., C_HY:C_POOL], lp)
    pool = pool_mixer(p[..., C_POOL:C_GATE], lp['pool_w'], lp['pool_scale'])
    gates = jax.nn.sigmoid(p[..., C_GATE:].astype(jnp.float32)).astype(p.dtype)
    wb = lp['w_branch']
    merged = (gates[..., :D_MODEL] * (hy @ wb[0])
              + gates[..., D_MODEL:2 * D_MODEL] * (attn @ wb[1])
              + gates[..., 2 * D_MODEL:] * (pool @ wb[2]))
    return merged @ lp['w_out']


def swiglu(h, lp):
    return (jax.nn.silu(h @ lp['ffn_w1']) * (h @ lp['ffn_w3'])) @ lp['ffn_w2']


def trunk_layer(xl, xc, c, c_ctx, rows, cols, lp, last):
    mod_l = (jax.nn.silu(c) @ lp['w_ada'] + lp['b_ada'])[:, None, :]
    mod_c = jax.nn.silu(c_ctx) @ lp['w_ada'] + lp['b_ada']
    sh1_l, sc1_l, g1_l, sh2_l, sc2_l, g2_l = jnp.split(mod_l, 6, axis=-1)
    sh1_c, sc1_c, g1_c, sh2_c, sc2_c, g2_c = jnp.split(mod_c, 6, axis=-1)

    hl = xl * (1 + sc1_l) + sh1_l
    hc = xc * (1 + sc1_c) + sh1_c
    pl = hl @ lp['w_in']
    if last:
        kc, vc = kv_heads(hc @ lp['w_in'][:, C_K:C_HY], lp['k_norm_g'])
    else:
        pc = hc @ lp['w_in']
        kc, vc = kv_heads(pc[..., C_K:C_HY], lp['k_norm_g'])

    ql = axial_rope(queries(pl, lp['q_norm_g']), rows, cols)
    kl, vl = kv_heads(pl[..., C_K:C_HY], lp['k_norm_g'])
    kl = axial_rope(kl, rows, cols)
    k_all = jnp.concatenate([kc, kl], axis=1)
    v_all = jnp.concatenate([vc, vl], axis=1)
    xl = layer_norm(DN_ALPHA * xl + g1_l * stream_mixer(pl, ql, k_all, v_all, lp), lp['ln1_g'], lp['ln1_b'])
    xl = layer_norm(DN_ALPHA * xl + g2_l * swiglu(xl * (1 + sc2_l) + sh2_l, lp), lp['ln2_g'], lp['ln2_b'])

    if not last:
        qc = queries(pc, lp['q_norm_g'])
        xc = layer_norm(DN_ALPHA * xc + g1_c * stream_mixer(pc, qc, kc, vc, lp), lp['ln1_g'], lp['ln1_b'])
        xc = layer_norm(DN_ALPHA * xc + g2_c * swiglu(xc * (1 + sc2_c) + sh2_c, lp), lp['ln2_g'], lp['ln2_b'])
    return xl, xc


def setup_inputs(seed: int = 0) -> dict:
    key = jax.random.key(seed)
    ks = jax.random.split(key, 32)
    f32 = jnp.float32
    L = DEPTH
    D = D_MODEL

    def nrm(k, shape, s):
        return jax.random.normal(k, shape, f32) * s

    return {
        'x': nrm(ks[0], (BATCH, SEQ, D), 1.0),
        'c': nrm(ks[1], (BATCH, D), 1.0),
        'ctx': nrm(ks[2], (BATCH, CTX_LEN, D), 1.0),
        'c_ctx': nrm(ks[3], (D,), 1.0),
        'w_ada': nrm(ks[4], (L, D, 6 * D), 0.5 * D ** -0.5),
        'b_ada': nrm(ks[5], (L, 6 * D), 0.01),
        'w_in': nrm(ks[6], (L, D, IN_WIDTH), D ** -0.5),
        'q_norm_g': 1.0 + nrm(ks[7], (L, HEAD_DIM), 0.02),
        'k_norm_g': 1.0 + nrm(ks[8], (L, HEAD_DIM), 0.02),
        'hy_conv_w': nrm(ks[9], (L, HY_SHORT, 3 * HY_WIDTH), HY_SHORT ** -0.5),
        'hy_conv_b': nrm(ks[10], (L, 3 * HY_WIDTH), 0.02),
        'hf_w1': nrm(ks[11], (L, HY_EMB, HY_FILTER_HIDDEN), HY_EMB ** -0.5),
        'hf_b1': nrm(ks[12], (L, HY_FILTER_HIDDEN), 0.02),
        'hf_freq': 1.0 + nrm(ks[13], (L, HY_FILTER_HIDDEN), 0.02),
        'hf_w2': nrm(ks[14], (L, HY_FILTER_HIDDEN, HY_FILTER_HIDDEN), HY_FILTER_HIDDEN ** -0.5),
        'hf_b2': nrm(ks[15], (L, HY_FILTER_HIDDEN), 0.02),
        'hf_w3': nrm(ks[16], (L, HY_FILTER_HIDDEN, 2 * HY_WIDTH), HY_FILTER_HIDDEN ** -0.5),
        'hy_d': nrm(ks[17], (L, HY_WIDTH), 0.5),
        'pool_w': nrm(ks[18], (L, len(POOL_WINDOWS), POOL_GROUP, POOL_GROUP), POOL_GROUP ** -0.5),
        'pool_scale': 1.0 + nrm(ks[19], (L, POOL_WIDTH), 0.05),
        'w_branch': nrm(ks[20], (L, N_BRANCH, D, D), D ** -0.5),
        'w_out': nrm(ks[21], (L, D, D), D ** -0.5 * DN_BETA),
        'ln1_g': 1.0 + nrm(ks[22], (L, D), 0.02),
        'ln1_b': nrm(ks[23], (L, D), 0.02),
        'ln2_g': 1.0 + nrm(ks[24], (L, D), 0.02),
        'ln2_b': nrm(ks[25], (L, D), 0.02),
        'ffn_w1': nrm(ks[26], (L, D, D_FF), D ** -0.5),
        'ffn_w3': nrm(ks[27], (L, D, D_FF), D ** -0.5),
        'ffn_w2': nrm(ks[28], (L, D_FF, D), D_FF ** -0.5 * DN_BETA),
    }


def reference(x, c, ctx, c_ctx, w_ada, b_ada, w_in, q_norm_g, k_norm_g, hy_conv_w, hy_conv_b,
              hf_w1, hf_b1, hf_freq, hf_w2, hf_b2, hf_w3, hy_d, pool_w, pool_scale, w_branch, w_out,
              ln1_g, ln1_b, ln2_g, ln2_b, ffn_w1, ffn_w3, ffn_w2):
    n_tok = x.shape[1]
    ROWS = n_tok // GRID_W
    rows = jnp.repeat(jnp.arange(ROWS), GRID_W)
    cols = jnp.tile(jnp.arange(GRID_W), ROWS)
    xl, xc = x, ctx
    for l in range(DEPTH):
        lp = dict(w_ada=w_ada[l], b_ada=b_ada[l], w_in=w_in[l], q_norm_g=q_norm_g[l], k_norm_g=k_norm_g[l],
                  hy_conv_w=hy_conv_w[l], hy_conv_b=hy_conv_b[l], hf_w1=hf_w1[l], hf_b1=hf_b1[l],
                  hf_freq=hf_freq[l], hf_w2=hf_w2[l], hf_b2=hf_b2[l], hf_w3=hf_w3[l], hy_d=hy_d[l],
                  pool_w=pool_w[l], pool_scale=pool_scale[l], w_branch=w_branch[l], w_out=w_out[l],
                  ln1_g=ln1_g[l], ln1_b=ln1_b[l], ln2_g=ln2_g[l], ln2_b=ln2_b[l],
                  ffn_w1=ffn_w1[l], ffn_w3=ffn_w3[l], ffn_w2=ffn_w2[l])
        xl, xc = trunk_layer(xl, xc, c, c_ctx, rows, cols, lp, l == DEPTH - 1)
    return xl
```

```python
import functools
import math

import numpy as np
import jax
import jax.numpy as jnp
from jax import lax
from jax.experimental import pallas as pl
from jax.experimental.pallas import tpu as pltpu

F32 = jnp.float32
BF = jnp.bfloat16
HIGHEST = lax.Precision.HIGHEST

D_MODEL = 1024
GRID_W = 64
N_HEADS = 8
N_KV_HEADS = 2
HEAD_DIM = 128
GQA_GROUP = N_HEADS // N_KV_HEADS
ROPE_THETA = 10000.0
Q_W = N_HEADS * HEAD_DIM
KV_W = N_KV_HEADS * HEAD_DIM
HY_WIDTH = D_MODEL
HY_EMB = 33
HY_BANDS = (HY_EMB - 1) // 2
HY_FILTER_HIDDEN = 64
HY_TARGET = 1e-2
HY_MAX_DECAY = math.log(HY_TARGET) / 0.3
HY_MIN_DECAY = math.log(HY_TARGET) / 1.5
HY_SHIFT = 0.05
POOL_WINDOWS = (2, 4, 8, 16)
POOL_GROUP = D_MODEL // len(POOL_WINDOWS)
D_FF = 2816
DEPTH = 2
DN_ALPHA = (2 * DEPTH) ** 0.25
EPS = 1e-6

C_Q = 0
C_K = C_Q + Q_W
C_V = C_K + KV_W
C_HY = C_V + KV_W
C_POOL = C_HY + 3 * HY_WIDTH
C_GATE = C_POOL + D_MODEL
IN_WIDTH = C_GATE + 3 * D_MODEL

P_GATE = 0
P_HY = 3 * D_MODEL
P_Q = P_HY + 3 * HY_WIDTH
P_POOL = P_Q + Q_W
P_K = P_POOL + D_MODEL
P_V = P_K + KV_W

VMEM_LIMIT = 60 * 1024 * 1024
FFT_INNER = 128


def _cparams(n_axes):
    return pltpu.CompilerParams(dimension_semantics=("arbitrary",) * n_axes, vmem_limit_bytes=VMEM_LIMIT)


def _const_spec(shape, n_grid):
    nd = len(shape)
    return pl.BlockSpec(shape, lambda *g, _nd=nd: (0,) * _nd)


def _layer_norm(r, g, b):
    mu = jnp.mean(r, axis=-1, keepdims=True)
    d = r - mu
    var = jnp.mean(d * d, axis=-1, keepdims=True)
    return d * lax.rsqrt(var + EPS) * g + b


def _ada_body(c_ref, w_ref, b_ref, o_ref):
    c = c_ref[...]
    s = c * jax.nn.sigmoid(c)
    o_ref[...] = jnp.dot(s, w_ref[...], preferred_element_type=F32, precision=HIGHEST) + b_ref[...]


def _ada(c_all, w_ada, b_ada):
    rows, d = c_all.shape
    n = w_ada.shape[1]
    tn = 512
    return pl.pallas_call(
        _ada_body,
        out_shape=jax.ShapeDtypeStruct((rows, n), F32),
        grid=(n // tn,),
        in_specs=[pl.BlockSpec((rows, d), lambda j: (0, 0)),
                  pl.BlockSpec((d, tn), lambda j: (0, j)),
                  pl.BlockSpec((1, tn), lambda j: (0, j))],
        out_specs=pl.BlockSpec((rows, tn), lambda j: (0, j)),
        compiler_params=_cparams(1),
        name="ada_mod",
    )(c_all, w_ada, b_ada.reshape(1, n))


def _inproj_body(x_ref, mod_ref, w_ref, o_ref, h_scr):
    @pl.when(pl.program_id(2) == 0)
    def _():
        sh = mod_ref[0:1, :]
        sc = mod_ref[1:2, :]
        h_scr[...] = (x_ref[...] * (1.0 + sc) + sh).astype(BF)

    o_ref[...] = jnp.dot(h_scr[...], w_ref[...], preferred_element_type=F32).astype(BF)


def _in_proj(x, mod, w_in_bf):
    b, l, d = x.shape
    n = w_in_bf.shape[1]
    tm = min(l, 1024)
    tn = 512
    return pl.pallas_call(
        _inproj_body,
        out_shape=jax.ShapeDtypeStruct((b, l, n), BF),
        grid=(b, l // tm, n // tn),
        in_specs=[pl.BlockSpec((None, tm, d), lambda bi, i, j: (bi, i, 0)),
                  pl.BlockSpec((None, 6, d), lambda bi, i, j: (bi, 0, 0)),
                  pl.BlockSpec((d, tn), lambda bi, i, j: (0, j))],
        out_specs=pl.BlockSpec((None, tm, tn), lambda bi, i, j: (bi, i, j)),
        scratch_shapes=[pltpu.VMEM((tm, d), BF)],
        compiler_params=_cparams(3),
        name="in_proj",
    )(x, mod, w_in_bf)


def _norm_rope(y, g, cos_t, sin_t, scale):
    ms = jnp.mean(y * y, axis=-1, keepdims=True)
    y = y * lax.rsqrt(ms + EPS) * g
    lane = lax.broadcasted_iota(jnp.int32, y.shape, 1)
    up = pltpu.roll(y, HEAD_DIM - 32, 1)
    dn = pltpu.roll(y, 32, 1)
    partner = jnp.where((lane % 64) < 32, up, dn)
    out = y * cos_t + partner * sin_t
    if scale != 1.0:
        out = out * scale
    return out.astype(BF)


def _qkprep_body(q_ref, k_ref, cos_ref, sin_ref, gq_ref, gk_ref, qo_ref, ko_ref):
    cos_t = cos_ref[...]
    sin_t = sin_ref[...]
    for h in range(N_HEADS):
        sl = slice(h * HEAD_DIM, (h + 1) * HEAD_DIM)
        qo_ref[:, sl] = _norm_rope(q_ref[:, sl].astype(F32), gq_ref[...], cos_t, sin_t, HEAD_DIM ** -0.5)
    for h in range(N_KV_HEADS):
        sl = slice(h * HEAD_DIM, (h + 1) * HEAD_DIM)
        ko_ref[:, sl] = _norm_rope(k_ref[:, sl].astype(F32), gk_ref[...], cos_t, sin_t, 1.0)


def _qk_prep(p, cos_t, sin_t, gq, gk):
    b, l, _ = p.shape
    tm = min(l, 512)
    return pl.pallas_call(
        _qkprep_body,
        out_shape=(jax.ShapeDtypeStruct((b, l, Q_W), BF), jax.ShapeDtypeStruct((b, l, KV_W), BF)),
        grid=(b, l // tm),
        in_specs=[pl.BlockSpec((None, tm, Q_W), lambda bi, i: (bi, i, P_Q // Q_W)),
                  pl.BlockSpec((None, tm, KV_W), lambda bi, i: (bi, i, P_K // KV_W)),
                  pl.BlockSpec((tm, HEAD_DIM), lambda bi, i: (i, 0)),
                  pl.BlockSpec((tm, HEAD_DIM), lambda bi, i: (i, 0)),
                  pl.BlockSpec((1, HEAD_DIM), lambda bi, i: (0, 0)),
                  pl.BlockSpec((1, HEAD_DIM), lambda bi, i: (0, 0))],
        out_specs=(pl.BlockSpec((None, tm, Q_W), lambda bi, i: (bi, i, 0)),
                   pl.BlockSpec((None, tm, KV_W), lambda bi, i: (bi, i, 0))),
        compiler_params=_cparams(2),
        name="qk_prep",
    )(p, p, cos_t, sin_t, gq.reshape(1, HEAD_DIM), gk.reshape(1, HEAD_DIM))


def _rope_tables(l, with_positions):
    quarter = HEAD_DIM // 4
    if not with_positions:
        return jnp.ones((l, HEAD_DIM), F32), jnp.zeros((l, HEAD_DIM), F32)
    t = jnp.arange(l)
    rows = (t // GRID_W).astype(F32)
    cols = (t % GRID_W).astype(F32)
    inv = jnp.power(ROPE_THETA, -jnp.arange(quarter, dtype=F32) / quarter)
    ar = rows[:, None] * inv[None, :]
    ac = cols[:, None] * inv[None, :]
    cos_t = jnp.concatenate([jnp.cos(ar), jnp.cos(ar), jnp.cos(ac), jnp.cos(ac)], axis=-1)
    sin_t = jnp.concatenate([-jnp.sin(ar), jnp.sin(ar), -jnp.sin(ac), jnp.sin(ac)], axis=-1)
    return cos_t, sin_t


def _flash_body(tk, q_ref, k_ref, v_ref, o_ref):
    tq = q_ref.shape[0]
    nk = k_ref.shape[0] // tk
    for h in range(GQA_GROUP):
        sl = slice(h * HEAD_DIM, (h + 1) * HEAD_DIM)
        q = q_ref[:, sl]

        def step(i, carry, q=q):
            m, l, acc = carry
            r0 = pl.multiple_of(i * tk, tk)
            k = k_ref[pl.ds(r0, tk), :]
            v = v_ref[pl.ds(r0, tk), :]
            s = lax.dot_general(q, k, (((1,), (1,)), ((), ())), preferred_element_type=F32)
            m_new = jnp.maximum(m, jnp.max(s, axis=-1, keepdims=True))
            alpha = jnp.exp(m - m_new)
            p = jnp.exp(s - m_new)
            l = alpha * l + jnp.sum(p, axis=-1, keepdims=True)
            acc = alpha * acc + jnp.dot(p.astype(BF), v, preferred_element_type=F32)
            return m_new, l, acc

        init = (jnp.full((tq, 1), -jnp.inf, F32), jnp.zeros((tq, 1), F32), jnp.zeros((tq, HEAD_DIM), F32))
        m, l, acc = lax.fori_loop(0, nk, step, init)
        o_ref[:, sl] = (acc / l).astype(BF)


def _attention(q, k, v):
    b, lq, _ = q.shape
    lk = k.shape[1]
    tq = min(lq, 256)
    tk = 768 if lk % 768 == 0 else 256
    gw = GQA_GROUP * HEAD_DIM
    return pl.pallas_call(
        functools.partial(_flash_body, tk),
        out_shape=jax.ShapeDtypeStruct((b, lq, Q_W), BF),
        grid=(b, N_KV_HEADS, lq // tq),
        in_specs=[pl.BlockSpec((None, tq, gw), lambda bi, g, i: (bi, i, g)),
                  pl.BlockSpec((None, lk, HEAD_DIM), lambda bi, g, i: (bi, 0, g)),
                  pl.BlockSpec((None, lk, HEAD_DIM), lambda bi, g, i: (bi, 0, g))],
        out_specs=pl.BlockSpec((None, tq, gw), lambda bi, g, i: (bi, i, g)),
        compiler_params=_cparams(3),
        name="gqa_attention",
    )(q, k, v)


def _pack(lo, hi):
    ul = lax.bitcast_convert_type(lo.astype(BF).astype(F32), jnp.uint32)
    uh = lax.bitcast_convert_type(hi.astype(BF).astype(F32), jnp.uint32)
    return lax.bitcast_convert_type(uh | (ul >> 16), jnp.int32)


def _unpack_lo(x):
    return lax.bitcast_convert_type(lax.bitcast_convert_type(x, jnp.uint32) << 16, F32)


def _unpack_hi(x):
    return lax.bitcast_convert_type(lax.bitcast_convert_type(x, jnp.uint32) & jnp.uint32(0xFFFF0000), F32)


def _hypre_body(rc, uv_ref, u0_ref, u1_ref, wv_ref, w0_ref, w1_ref, bv_ref, b0_ref, b1_ref, z_ref, x0_ref):
    l = uv_ref.shape[1]
    n_chunks = l // rc

    def conv(u_ref, half, r0, w_ref, b_ref):
        cur = u_ref[half, pl.ds(r0, rc), :].astype(F32)
        p0 = jnp.maximum(r0 - 16, 0)
        n0 = jnp.minimum(r0 + rc, l - 16)
        prev_row = u_ref[half, pl.ds(pl.multiple_of(p0, 16), 16), :].astype(F32)[15:16]
        next_row = u_ref[half, pl.ds(pl.multiple_of(n0, 16), 16), :].astype(F32)[0:1]
        prev_row = jnp.where(r0 > 0, prev_row, 0.0)
        next_row = jnp.where(r0 + rc < l, next_row, 0.0)
        row = lax.broadcasted_iota(jnp.int32, cur.shape, 0)
        x_prev = jnp.where(row == 0, prev_row, pltpu.roll(cur, 1, 0))
        x_next = jnp.where(row == rc - 1, next_row, pltpu.roll(cur, rc - 1, 0))
        return b_ref[...] + x_prev * w_ref[0:1, :] + cur * w_ref[1:2, :] + x_next * w_ref[2:3, :]

    def chunk(i, carry):
        r0 = pl.multiple_of(i * rc, rc)
        zs = []
        for half in range(2):
            v = conv(uv_ref, half, r0, wv_ref, bv_ref)
            x1 = conv(u1_ref, half, r0, w1_ref, b1_ref)
            x0 = conv(u0_ref, half, r0, w0_ref, b0_ref)
            x0_ref[half, pl.ds(r0, rc), :] = x0.astype(BF)
            zs.append(v * x1)
        z_ref[pl.ds(r0, rc), :] = _pack(zs[0], zs[1])
        return carry

    lax.fori_loop(0, n_chunks, chunk, 0)


def _hyena_pre(p, conv_w, conv_b):
    b, l, n = p.shape
    hp = b // 2
    tc = 128
    rc = min(l, 512)
    p4 = p.reshape(2, hp, l, n)
    c = HY_WIDTH
    nb = c // tc
    base = P_HY // tc

    def u_spec(g):
        return pl.BlockSpec((2, None, l, tc), lambda pi, j, _g=g: (0, pi, 0, base + _g * nb + j))

    def w_spec(g):
        return pl.BlockSpec((3, tc), lambda pi, j, _g=g: (0, _g * nb + j))

    def b_spec(g):
        return pl.BlockSpec((1, tc), lambda pi, j, _g=g: (0, _g * nb + j))

    cb = conv_b.reshape(1, 3 * c)
    z, x0 = pl.pallas_call(
        functools.partial(_hypre_body, rc),
        out_shape=(jax.ShapeDtypeStruct((hp, l, c), jnp.int32), jax.ShapeDtypeStruct((2, hp, l, c), BF)),
        grid=(hp, nb),
        in_specs=[u_spec(0), u_spec(1), u_spec(2), w_spec(0), w_spec(1), w_spec(2), b_spec(0), b_spec(1), b_spec(2)],
        out_specs=(pl.BlockSpec((None, l, tc), lambda pi, j: (pi, 0, j)),
                   pl.BlockSpec((2, None, l, tc), lambda pi, j: (0, pi, 0, j))),
        compiler_params=_cparams(2),
        name="hyena_pre",
    )(p4, p4, p4, conv_w, conv_w, conv_w, cb, cb, cb)
    return z, x0.reshape(b, l, c)


def _filter_body(l, tr, f_ref, w1_ref, b1_ref, fr_ref, w2_ref, b2_ref, w3_ref, dl_ref, k_ref, ss_ref):
    i = pl.program_id(0)
    feats = f_ref[...]
    freq = fr_ref[...]
    h = jnp.sin(freq * (jnp.dot(feats, w1_ref[...], preferred_element_type=F32, precision=HIGHEST) + b1_ref[...]))
    h = jnp.sin(freq * (jnp.dot(h, w2_ref[...], preferred_element_type=F32, precision=HIGHEST) + b2_ref[...]))
    h = jnp.dot(h, w3_ref[...], preferred_element_type=F32, precision=HIGHEST)
    t01 = feats[:, 0:1]
    window = jnp.exp(-t01 * dl_ref[...]) + HY_SHIFT
    row = i * tr + lax.broadcasted_iota(jnp.int32, h.shape, 0)
    kern = jnp.where(row == l, 0.0, h * window)
    k_ref[...] = kern

    @pl.when(i == 0)
    def _():
        ss_ref[...] = jnp.zeros_like(ss_ref)

    ss_ref[...] += jnp.sum(kern * kern, axis=0, keepdims=True)


def _hyena_filter(l, w1, b1, freq, w2, b2, w3):
    n = 2 * l
    c = HY_WIDTH
    hid = HY_FILTER_HIDDEN
    tr = min(l, 512)
    j = jnp.arange(n)
    lag = jnp.where(j < l, j, n - j).astype(F32)
    t01 = lag / max(l - 1, 1)
    bands = jnp.linspace(1e-4, HY_BANDS - 1, HY_BANDS, dtype=F32)
    ang = (2.0 * math.pi / l) * lag[:, None] * bands[None, :]
    feats = jnp.concatenate([t01[:, None], jnp.cos(ang), -jnp.sin(ang)], axis=-1)
    feats = jnp.pad(feats, ((0, 0), (0, hid - HY_EMB)))
    w1p = jnp.pad(w1, ((0, hid - HY_EMB), (0, 0)))
    deltas = jnp.abs(jnp.linspace(HY_MIN_DECAY, HY_MAX_DECAY, c, dtype=F32)).reshape(1, c)
    nt = l // tr
    return pl.pallas_call(
        functools.partial(_filter_body, l, tr),
        out_shape=(jax.ShapeDtypeStruct((n, c), F32), jax.ShapeDtypeStruct((1, c), F32)),
        grid=(n // tr,),
        in_specs=[pl.BlockSpec((tr, hid), lambda i: (i, 0)),
                  pl.BlockSpec((hid, hid), lambda i: (0, 0)),
                  pl.BlockSpec((1, hid), lambda i: (0, 0)),
                  pl.BlockSpec((1, hid), lambda i: (0, 0)),
                  pl.BlockSpec((hid, hid), lambda i: (0, 0)),
                  pl.BlockSpec((1, hid), lambda i: (0, 0)),
                  pl.BlockSpec((hid, c), lambda i: (0, i // nt)),
                  pl.BlockSpec((1, c), lambda i: (0, 0))],
        out_specs=(pl.BlockSpec((tr, c), lambda i: (i, 0)),
                   pl.BlockSpec((1, c), lambda i: (0, 0))),
        compiler_params=_cparams(1),
        name="hyena_filter",
    )(feats, w1p, b1.reshape(1, hid), freq.reshape(1, hid), w2, b2.reshape(1, hid), w3, deltas)


def _split(l):
    n = 2 * l
    n1 = n // FFT_INNER if n > 4 * FFT_INNER else 1
    return n1, n // n1


def _interleave_cols(a, b):
    r, k = a.shape
    return np.stack([a, b], axis=-1).reshape(r, 2 * k)


@functools.lru_cache(maxsize=None)
def _fft_tables(l):
    n1, n2 = _split(l)
    n = 2 * l
    f64 = np.float64
    t = {}
    n2in = n2 if n1 > 1 else l
    k2 = np.arange(n2, dtype=f64)[:, None]
    t2 = np.arange(n2in, dtype=f64)[None, :]
    ang2 = -2.0 * np.pi * k2 * t2 / n2
    f2re, f2im = np.cos(ang2), np.sin(ang2)
    t['cre2'] = _interleave_cols(f2re, f2re)
    t['cim2'] = _interleave_cols(f2im, f2im)
    k1 = np.arange(n1, dtype=f64)[:, None]
    angw = -2.0 * np.pi * k1 * np.arange(n2in, dtype=f64)[None, :] / n
    wre, wim = np.cos(angw), np.sin(angw)
    t['w4'] = np.stack([_interleave_cols(wre, -wim), _interleave_cols(-wim, -wre),
                        _interleave_cols(wim, wre), _interleave_cols(wre, -wim)], axis=1)
    n2out = n2 if n1 > 1 else l
    tt = np.arange(n2out, dtype=f64)[:, None]
    kk = np.arange(n2, dtype=f64)[None, :]
    angc = -2.0 * np.pi * tt * kk / n2
    cr, ci = np.cos(angc), np.sin(angc)
    t['gc'] = np.block([[cr, ci], [-ci, cr]])
    k2f = np.arange(n2, dtype=f64)[:, None]
    t2f = np.arange(n2, dtype=f64)[None, :]
    angf = -2.0 * np.pi * k2f * t2f / n2
    t['f2re'], t['f2im'] = np.cos(angf), np.sin(angf)
    angwf = -2.0 * np.pi * k1 * np.arange(n2, dtype=f64)[None, :] / n
    t['w2'] = np.stack([np.cos(angwf), np.sin(angwf)], axis=1)
    if n1 > 1:
        h = n1 // 2
        kk1 = np.arange(n1, dtype=f64)[:, None]
        ang1 = -2.0 * np.pi * kk1 * np.arange(h, dtype=f64)[None, :] / n1
        f1re, f1im = np.cos(ang1), np.sin(ang1)
        t['m1'] = np.concatenate([_interleave_cols(f1re, -f1im), _interleave_cols(f1im, f1re)], axis=0)
        d1re, d1im = f1re.T, f1im.T
        t['d1re2'] = _interleave_cols(d1re, d1re)
        t['d1im2'] = _interleave_cols(d1im, d1im)
        wre_t, wim_t = wre.T, wim.T
        t['v4'] = np.stack([_interleave_cols(wre_t, wim_t), _interleave_cols(-wim_t, wre_t),
                            _interleave_cols(-wim_t, wre_t), _interleave_cols(-wre_t, -wim_t)], axis=1)
        ang1f = -2.0 * np.pi * kk1 * np.arange(n1, dtype=f64)[None, :] / n1
        t['m1k'] = np.concatenate([np.cos(ang1f), np.sin(ang1f)], axis=0)
    return {k: np.asarray(v, np.float32) for k, v in t.items()}


def _kfft_body(l, *refs):
    n1, n2 = _split(l)
    n = 2 * l
    if n1 > 1:
        k_ref, ss_ref, m1k_ref, f2re_ref, f2im_ref, w2_ref, o_ref, are_scr, aim_scr = refs
    else:
        k_ref, ss_ref, f2re_ref, f2im_ref, o_ref = refs
    scale = lax.rsqrt(ss_ref[...] + EPS) * (1.0 / n)

    if n1 > 1:
        def stage1(t2, carry):
            xin = k_ref[pl.ds(t2, n1, stride=n2), :]
            o = jnp.dot(m1k_ref[...], xin, preferred_element_type=F32, precision=HIGHEST)
            are_scr[pl.ds(t2, n1, stride=n2), :] = o[:n1]
            aim_scr[pl.ds(t2, n1, stride=n2), :] = o[n1:]
            return carry

        lax.fori_loop(0, n2, stage1, 0)

        def slab(k1, carry):
            r0 = pl.multiple_of(k1 * n2, n2)
            s = jnp.concatenate([are_scr[pl.ds(r0, n2), :], aim_scr[pl.ds(r0, n2), :]], axis=0)
            w2 = w2_ref[k1]
            wre, wim = w2[0:1], w2[1:2]
            hre = f2re_ref[...] * wre - f2im_ref[...] * wim
            him = f2re_ref[...] * wim + f2im_ref[...] * wre
            hblk = jnp.concatenate([jnp.concatenate([hre, -him], axis=1),
                                    jnp.concatenate([him, hre], axis=1)], axis=0)
            x = jnp.dot(hblk, s, preferred_element_type=F32, precision=HIGHEST)
            o_ref[k1] = _pack(x[:n2] * scale, x[n2:] * scale)
            return carry

        lax.fori_loop(0, n1, slab, 0)
    else:
        f = jnp.concatenate([f2re_ref[...], f2im_ref[...]], axis=0)
        x = jnp.dot(f, k_ref[...], preferred_element_type=F32, precision=HIGHEST)
        o_ref[0] = _pack(x[:n2] * scale, x[n2:] * scale)


def _kernel_spectrum(l, kern, sumsq):
    n1, n2 = _split(l)
    n = 2 * l
    c = kern.shape[1]
    ct = 128
    tb = _fft_tables(l)
    names = (['m1k'] if n1 > 1 else []) + ['f2re', 'f2im'] + (['w2'] if n1 > 1 else [])
    consts = [jnp.asarray(tb[k]) for k in names]
    scratch = [pltpu.VMEM((n, ct), F32), pltpu.VMEM((n, ct), F32)] if n1 > 1 else []
    return pl.pallas_call(
        functools.partial(_kfft_body, l),
        out_shape=jax.ShapeDtypeStruct((n1, n2, c), jnp.int32),
        grid=(c // ct,),
        in_specs=[pl.BlockSpec((n, ct), lambda j: (0, j), pipeline_mode=pl.Buffered(1)),
                  pl.BlockSpec((1, ct), lambda j: (0, j))] + [_const_spec(a.shape, 1) for a in consts],
        out_specs=pl.BlockSpec((n1, n2, ct), lambda j: (0, 0, j)),
        scratch_shapes=scratch,
        compiler_params=_cparams(1),
        name="hyena_filter_fft",
    )(kern, sumsq, *consts)


def _fft_body(l, *refs):
    n1, n2 = _split(l)
    if n1 > 1:
        (z_ref, kf_ref, cre2_ref, cim2_ref, w4_ref, gc_ref, m1_ref, d1re2_ref, d1im2_ref, v4_ref,
         out_ref, a_scr) = refs
        h = n1 // 2

        def stage1(t2, carry):
            xin = z_ref[pl.ds(t2, h, stride=n2), :]
            xb = pltpu.bitcast(xin, BF)
            o = jnp.dot(m1_ref[...], xb, preferred_element_type=F32)
            a_scr[pl.ds(t2, n1, stride=n2), :] = _pack(o[:n1], o[n1:])
            return carry

        lax.fori_loop(0, n2, stage1, 0)

        def slab(k1, carry):
            r0 = pl.multiple_of(k1 * n2, n2)
            s = pltpu.bitcast(a_scr[pl.ds(r0, n2), :], BF)
            w4 = w4_ref[k1]
            cre2 = cre2_ref[...]
            cim2 = cim2_ref[...]
            top = cre2 * w4[0:1] + cim2 * w4[1:2]
            bot = cre2 * w4[2:3] + cim2 * w4[3:4]
            hblk = jnp.concatenate([top, bot], axis=0).astype(BF)
            x = jnp.dot(hblk, s, preferred_element_type=F32)
            kf = kf_ref[k1]
            kre, kim = _unpack_lo(kf), _unpack_hi(kf)
            xre, xim = x[:n2], x[n2:]
            y = jnp.concatenate([xre * kre - xim * kim, xre * kim + xim * kre], axis=0).astype(BF)
            b = jnp.dot(gc_ref[...], y, preferred_element_type=F32)
            a_scr[pl.ds(r0, n2), :] = _pack(b[:n2], b[n2:])
            return carry

        lax.fori_loop(0, n1, slab, 0)

        def stage3(t2, carry):
            bin_ = pltpu.bitcast(a_scr[pl.ds(t2, n1, stride=n2), :], BF)
            v4 = v4_ref[t2]
            d1re2 = d1re2_ref[...]
            d1im2 = d1im2_ref[...]
            top = d1re2 * v4[0:1] + d1im2 * v4[1:2]
            bot = d1re2 * v4[2:3] + d1im2 * v4[3:4]
            m3 = jnp.concatenate([top, bot], axis=0).astype(BF)
            o = jnp.dot(m3, bin_, preferred_element_type=F32)
            out_ref[pl.ds(t2, h, stride=n2), :] = _pack(o[:h], o[h:])
            return carry

        lax.fori_loop(0, n2, stage3, 0)
    else:
        z_ref, kf_ref, cre2_ref, cim2_ref, w4_ref, gc_ref, out_ref = refs
        s = pltpu.bitcast(z_ref[...], BF)
        w4 = w4_ref[0]
        top = cre2_ref[...] * w4[0:1] + cim2_ref[...] * w4[1:2]
        bot = cre2_ref[...] * w4[2:3] + cim2_ref[...] * w4[3:4]
        hblk = jnp.concatenate([top, bot], axis=0).astype(BF)
        x = jnp.dot(hblk, s, preferred_element_type=F32)
        kf = kf_ref[0]
        kre, kim = _unpack_lo(kf), _unpack_hi(kf)
        xre, xim = x[:n2], x[n2:]
        y = jnp.concatenate([xre * kre - xim * kim, xre * kim + xim * kre], axis=0).astype(BF)
        b = jnp.dot(gc_ref[...], y, preferred_element_type=F32)
        out_ref[...] = _pack(b[:l], b[l:])


def _fft_conv(zp, kf):
    hp, l, c = zp.shape
    n1, n2 = _split(l)
    ct = 128
    tb = _fft_tables(l)
    names = ['cre2', 'cim2', 'w4', 'gc'] + (['m1', 'd1re2', 'd1im2', 'v4'] if n1 > 1 else [])
    consts = [jnp.asarray(tb[k]).astype(BF) if k in ('gc', 'm1') else jnp.asarray(tb[k]) for k in names]
    scratch = [pltpu.VMEM((n1 * n2, ct), jnp.int32)] if n1 > 1 else []
    return pl.pallas_call(
        functools.partial(_fft_body, l),
        out_shape=jax.ShapeDtypeStruct((hp, l, c), jnp.int32),
        grid=(c // ct, hp),
        in_specs=[pl.BlockSpec((None, l, ct), lambda j, p: (p, 0, j)),
                  pl.BlockSpec((n1, n2, ct), lambda j, p: (0, 0, j))] + [_const_spec(a.shape, 2) for a in consts],
        out_specs=pl.BlockSpec((None, l, ct), lambda j, p: (p, 0, j)),
        scratch_shapes=scratch,
        compiler_params=_cparams(2),
        name="hyena_fft_conv",
    )(zp, kf, *consts)


def _pool_body(rc, u_ref, w_ref, sc_ref, o_ref):
    l = u_ref.shape[0]
    g = pl.program_id(1)
    win = jnp.left_shift(2, g)
    before = win // 2
    after = win - 1 - before
    n_chunks = l // rc
    halo = 64

    def chunk(i, carry):
        r0 = pl.multiple_of(i * rc, rc)
        p0 = pl.multiple_of(jnp.maximum(r0 - halo, 0), halo)
        n0 = pl.multiple_of(jnp.minimum(r0 + rc, l - halo), halo)
        cur = u_ref[pl.ds(r0, rc), :]
        ext = jnp.concatenate([u_ref[pl.ds(p0, halo), :], cur, u_ref[pl.ds(n0, halo), :]], axis=0)
        t = r0 + lax.broadcasted_iota(jnp.int32, (rc, rc + 2 * halo), 0)
        s = r0 - halo + lax.broadcasted_iota(jnp.int32, (rc, rc + 2 * halo), 1)
        inside = (s >= t - before) & (s <= t + after) & (s >= 0) & (s < l)
        band = jnp.where(inside, 1.0, 0.0).astype(BF)
        sums = jnp.dot(band, ext, preferred_element_type=F32)
        tr = r0 + lax.broadcasted_iota(jnp.int32, (rc, POOL_GROUP), 0)
        cnt = jnp.minimum(tr + after + 1, l) - jnp.maximum(tr - before, 0)
        m = sums / cnt.astype(F32) - cur.astype(F32)
        y = jnp.dot(m.astype(BF), w_ref[...], preferred_element_type=F32) * sc_ref[...]
        o_ref[pl.ds(r0, rc), :] = y.astype(BF)
        return carry

    lax.fori_loop(0, n_chunks, chunk, 0)


def _pool(p, pool_w_bf, pool_scale):
    b, l, _ = p.shape
    rc = min(l, 512)
    ng = len(POOL_WINDOWS)
    base = P_POOL // POOL_GROUP
    return pl.pallas_call(
        functools.partial(_pool_body, rc),
        out_shape=jax.ShapeDtypeStruct((b, l, D_MODEL), BF),
        grid=(b, ng),
        in_specs=[pl.BlockSpec((None, l, POOL_GROUP), lambda bi, g: (bi, 0, base + g)),
                  pl.BlockSpec((None, POOL_GROUP, POOL_GROUP), lambda bi, g: (g, 0, 0)),
                  pl.BlockSpec((1, POOL_GROUP), lambda bi, g: (0, g))],
        out_specs=pl.BlockSpec((None, l, POOL_GROUP), lambda bi, g: (bi, 0, g)),
        compiler_params=_cparams(2),
        name="pool_mixer",
    )(p, pool_w_bf, pool_scale.reshape(1, D_MODEL))


def _mix_body(hp, x_ref, gate_ref, y_ref, z_ref, x0_ref, hd_ref, at_ref, po_ref, wb_ref, wo_ref,
              mod_ref, g_ref, b_ref, o_ref):
    hi = pl.program_id(0) >= hp
    yw = y_ref[...]
    zw = z_ref[...]
    y = jnp.where(hi, _unpack_hi(yw), _unpack_lo(yw))
    z = jnp.where(hi, _unpack_hi(zw), _unpack_lo(zw))
    hy = ((y + z * hd_ref[...]) * x0_ref[...].astype(F32)).astype(BF)
    d = D_MODEL
    merged = jax.nn.sigmoid(gate_ref[:, 0:d].astype(F32)) * jnp.dot(hy, wb_ref[0], preferred_element_type=F32)
    merged += jax.nn.sigmoid(gate_ref[:, d:2 * d].astype(F32)) * jnp.dot(at_ref[...], wb_ref[1],
                                                                         preferred_element_type=F32)
    merged += jax.nn.sigmoid(gate_ref[:, 2 * d:].astype(F32)) * jnp.dot(po_ref[...], wb_ref[2],
                                                                        preferred_element_type=F32)
    out = jnp.dot(merged.astype(BF), wo_ref[...], preferred_element_type=F32)
    r = DN_ALPHA * x_ref[...] + mod_ref[2:3, :] * out
    o_ref[...] = _layer_norm(r, g_ref[...], b_ref[...])


def _mix(x, p, y_pair, z_pair, x0c, hy_d, attn, pool, wb_bf, wo_bf, mod, ln_g, ln_b):
    b, l, d = x.shape
    hp = b // 2
    tm = min(l, 512)
    row = lambda bi, i: (bi, i, 0)
    pair = lambda bi, i: (bi % hp, i, 0)
    return pl.pallas_call(
        functools.partial(_mix_body, hp),
        out_shape=jax.ShapeDtypeStruct((b, l, d), F32),
        grid=(b, l // tm),
        in_specs=[pl.BlockSpec((None, tm, d), row),
                  pl.BlockSpec((None, tm, 3 * d), lambda bi, i: (bi, i, P_GATE // (3 * d))),
                  pl.BlockSpec((None, tm, d), pair),
                  pl.BlockSpec((None, tm, d), pair),
                  pl.BlockSpec((None, tm, d), row),
                  pl.BlockSpec((1, d), lambda bi, i: (0, 0)),
                  pl.BlockSpec((None, tm, d), row),
                  pl.BlockSpec((None, tm, d), row),
                  pl.BlockSpec((3, d, d), lambda bi, i: (0, 0, 0), pipeline_mode=pl.Buffered(1)),
                  pl.BlockSpec((d, d), lambda bi, i: (0, 0), pipeline_mode=pl.Buffered(1)),
                  pl.BlockSpec((None, 6, d), lambda bi, i: (bi, 0, 0)),
                  pl.BlockSpec((1, d), lambda bi, i: (0, 0)),
                  pl.BlockSpec((1, d), lambda bi, i: (0, 0))],
        out_specs=pl.BlockSpec((None, tm, d), row),
        compiler_params=_cparams(2),
        name="branch_mix",
    )(x, p, y_pair, z_pair, x0c, hy_d.reshape(1, d), attn, pool, wb_bf, wo_bf, mod,
      ln_g.reshape(1, d), ln_b.reshape(1, d))


def _ffn_body(n_chunks, x_ref, mod_ref, w1_ref, w3_ref, w2_ref, g_ref, b_ref, o_ref):
    x = x_ref[...]
    h = (x * (1.0 + mod_ref[4:5, :]) + mod_ref[3:4, :]).astype(BF)
    fc = D_FF // n_chunks
    acc = jnp.zeros(x.shape, F32)
    for c in range(n_chunks):
        sl = slice(c * fc, (c + 1) * fc)
        a = jnp.dot(h, w1_ref[:, sl], preferred_element_type=F32)
        bb = jnp.dot(h, w3_ref[:, sl], preferred_element_type=F32)
        gg = (a * jax.nn.sigmoid(a) * bb).astype(BF)
        acc += jnp.dot(gg, w2_ref[sl, :], preferred_element_type=F32)
    r = DN_ALPHA * x + mod_ref[5:6, :] * acc
    o_ref[...] = _layer_norm(r, g_ref[...], b_ref[...])


def _ffn(x, mod, w1_bf, w3_bf, w2_bf, ln_g, ln_b):
    b, l, d = x.shape
    tm = min(l, 512)
    row = lambda bi, i: (bi, i, 0)
    return pl.pallas_call(
        functools.partial(_ffn_body, 2),
        out_shape=jax.ShapeDtypeStruct((b, l, d), F32),
        grid=(b, l // tm),
        in_specs=[pl.BlockSpec((None, tm, d), row),
                  pl.BlockSpec((None, 6, d), lambda bi, i: (bi, 0, 0)),
                  pl.BlockSpec((d, D_FF), lambda bi, i: (0, 0), pipeline_mode=pl.Buffered(1)),
                  pl.BlockSpec((d, D_FF), lambda bi, i: (0, 0), pipeline_mode=pl.Buffered(1)),
                  pl.BlockSpec((D_FF, d), lambda bi, i: (0, 0), pipeline_mode=pl.Buffered(1)),
                  pl.BlockSpec((1, d), lambda bi, i: (0, 0)),
                  pl.BlockSpec((1, d), lambda bi, i: (0, 0))],
        out_specs=pl.BlockSpec((None, tm, d), row),
        compiler_params=_cparams(2),
        name="swiglu_ffn",
    )(x, mod, w1_bf, w3_bf, w2_bf, ln_g.reshape(1, d), ln_b.reshape(1, d))


def _permute_w_in(w):
    return jnp.concatenate([w[:, C_GATE:], w[:, C_HY:C_POOL], w[:, C_Q:C_K], w[:, C_POOL:C_GATE],
                            w[:, C_K:C_V], w[:, C_V:C_HY]], axis=1).astype(BF)


def _v_cols(p):
    return p[..., P_V:P_V + KV_W]


def _stream_block(x, p, q_hat, k_all, v_all, kf, mod, lw):
    attn = _attention(q_hat, k_all, v_all)
    z_pair, x0c = _hyena_pre(p, lw['hy_conv_w'], lw['hy_conv_b'])
    y_pair = _fft_conv(z_pair, kf)
    pool = _pool(p, lw['pool_w'], lw['pool_scale'])
    x = _mix(x, p, y_pair, z_pair, x0c, lw['hy_d'], attn, pool, lw['w_branch'], lw['w_out'], mod,
             lw['ln1_g'], lw['ln1_b'])
    return _ffn(x, mod, lw['ffn_w1'], lw['ffn_w3'], lw['ffn_w2'], lw['ln2_g'], lw['ln2_b'])


def kernel(x, c, ctx, c_ctx, w_ada, b_ada, w_in, q_norm_g, k_norm_g, hy_conv_w, hy_conv_b, hf_w1, hf_b1, hf_freq,
           hf_w2, hf_b2, hf_w3, hy_d, pool_w, pool_scale, w_branch, w_out, ln1_g, ln1_b, ln2_g, ln2_b,
           ffn_w1, ffn_w3, ffn_w2):
    b, l, d = x.shape
    lc = ctx.shape[1]
    depth = w_ada.shape[0]
    assert b % 2 == 0 and l % GRID_W == 0

    rows = 16
    c_all = jnp.zeros((rows, d), F32).at[:b].set(c).at[b].set(c_ctx)
    cos_l, sin_l = _rope_tables(l, True)
    cos_c, sin_c = _rope_tables(lc, False)

    xl, xc = x, ctx
    for li in range(depth):
        last = li == depth - 1
        lw = dict(hy_conv_w=hy_conv_w[li], hy_conv_b=hy_conv_b[li], hy_d=hy_d[li],
                  pool_w=pool_w[li].astype(BF), pool_scale=pool_scale[li],
                  w_branch=w_branch[li].astype(BF), w_out=w_out[li].astype(BF),
                  ln1_g=ln1_g[li], ln1_b=ln1_b[li], ln2_g=ln2_g[li], ln2_b=ln2_b[li],
                  ffn_w1=ffn_w1[li].astype(BF), ffn_w3=ffn_w3[li].astype(BF), ffn_w2=ffn_w2[li].astype(BF))
        w_in_p = _permute_w_in(w_in[li])

        mod = _ada(c_all, w_ada[li], b_ada[li]).reshape(rows, 6, d)
        mod_l = mod[:b]
        mod_c = jnp.broadcast_to(mod[b:b + 1], (b, 6, d))

        filt = (hf_w1[li], hf_b1[li], hf_freq[li], hf_w2[li], hf_b2[li], hf_w3[li])
        kf_l = _kernel_spectrum(l, *_hyena_filter(l, *filt))

        p_l = _in_proj(xl, mod_l, w_in_p)
        p_c = _in_proj(xc, mod_c, w_in_p)
        q_l, k_l = _qk_prep(p_l, cos_l, sin_l, q_norm_g[li], k_norm_g[li])
        q_c, k_c = _qk_prep(p_c, cos_c, sin_c, q_norm_g[li], k_norm_g[li])
        v_c = _v_cols(p_c)
        k_all = jnp.concatenate([k_c, k_l], axis=1)
        v_all = jnp.concatenate([v_c, _v_cols(p_l)], axis=1)

        xl = _stream_block(xl, p_l, q_l, k_all, v_all, kf_l, mod_l, lw)
        if not last:
            kf_c = _kernel_spectrum(lc, *_hyena_filter(lc, *filt))
            xc = _stream_block(xc, p_c, q_c, k_c, v_c, kf_c, mod_c, lw)
    return xl
```

```python
import functools
import math

import numpy as np
import jax
import jax.numpy as jnp
from jax import lax
from jax.experimental import pallas as pl
from jax.experimental.pallas import tpu as pltpu

F32 = jnp.float32
BF = jnp.bfloat16
HIGHEST = lax.Precision.HIGHEST

D_MODEL = 1024
GRID_W = 64
N_HEADS = 8
N_KV_HEADS = 2
HEAD_DIM = 128
GQA_GROUP = N_HEADS // N_KV_HEADS
ROPE_THETA = 10000.0
Q_W = N_HEADS * HEAD_DIM
KV_W = N_KV_HEADS * HEAD_DIM
HY_WIDTH = D_MODEL
HY_EMB = 33
HY_BANDS = (HY_EMB - 1) // 2
HY_FILTER_HIDDEN = 64
HY_TARGET = 1e-2
HY_MAX_DECAY = math.log(HY_TARGET) / 0.3
HY_MIN_DECAY = math.log(HY_TARGET) / 1.5
HY_SHIFT = 0.05
POOL_WINDOWS = (2, 4, 8, 16)
POOL_GROUP = D_MODEL // len(POOL_WINDOWS)
D_FF = 2816
DEPTH = 2
DN_ALPHA = (2 * DEPTH) ** 0.25
EPS = 1e-6

C_Q = 0
C_K = C_Q + Q_W
C_V = C_K + KV_W
C_HY = C_V + KV_W
C_POOL = C_HY + 3 * HY_WIDTH
C_GATE = C_POOL + D_MODEL
IN_WIDTH = C_GATE + 3 * D_MODEL

P_GATE = 0
P_HY = 3 * D_MODEL
P_Q = P_HY + 3 * HY_WIDTH
P_POOL = P_Q + Q_W
P_K = P_POOL + D_MODEL
P_V = P_K + KV_W

Q_SCALE = HEAD_DIM ** -0.5 * math.log2(math.e)

ATT_TQ = 256
ATT_TK = 768
ATT_FLAGS = None
FFT_CT = 256
FFT_UNROLL = 8
KF_UNROLL = 4
KF_PRECISION = lax.Precision.HIGHEST

VMEM_LIMIT = 60 * 1024 * 1024
FFT_INNER = 128
LANES = 128


def _cparams(n_axes):
    return pltpu.CompilerParams(dimension_semantics=("arbitrary",) * n_axes, vmem_limit_bytes=VMEM_LIMIT)


def _const_spec(shape, n_grid):
    nd = len(shape)
    return pl.BlockSpec(shape, lambda *g, _nd=nd: (0,) * _nd)


def _layer_norm(r, g, b):
    mu = jnp.mean(r, axis=-1, keepdims=True)
    d = r - mu
    var = jnp.mean(d * d, axis=-1, keepdims=True)
    return d * lax.rsqrt(var + EPS) * g + b


def _ada_body(c_ref, w_ref, b_ref, o_ref):
    c = c_ref[...]
    s = c * jax.nn.sigmoid(c)
    o_ref[...] = jnp.dot(s, w_ref[...], preferred_element_type=F32, precision=HIGHEST) + b_ref[...]


def _ada(c_all, w_ada, b_ada):
    rows, d = c_all.shape
    n = w_ada.shape[1]
    tn = 512
    return pl.pallas_call(
        _ada_body,
        out_shape=jax.ShapeDtypeStruct((rows, n), F32),
        grid=(n // tn,),
        in_specs=[pl.BlockSpec((rows, d), lambda j: (0, 0)),
                  pl.BlockSpec((d, tn), lambda j: (0, j)),
                  pl.BlockSpec((1, tn), lambda j: (0, j))],
        out_specs=pl.BlockSpec((rows, tn), lambda j: (0, j)),
        compiler_params=_cparams(1),
        name="ada_mod",
    )(c_all, w_ada, b_ada.reshape(1, n))


def _inproj_body(tn, x_ref, mod_ref, w_ref, o_ref):
    h = (x_ref[...] * (1.0 + mod_ref[1:2, :]) + mod_ref[0:1, :]).astype(BF)
    for j in range(w_ref.shape[1] // tn):
        sl = slice(j * tn, (j + 1) * tn)
        o_ref[:, sl] = jnp.dot(h, w_ref[:, sl], preferred_element_type=F32).astype(BF)


def _in_proj(x, mod, w_in_bf):
    b, l, d = x.shape
    n = w_in_bf.shape[1]
    tm = min(l, 512)
    tn = 512
    return pl.pallas_call(
        functools.partial(_inproj_body, tn),
        out_shape=jax.ShapeDtypeStruct((b, l, n), BF),
        grid=(b, l // tm),
        in_specs=[pl.BlockSpec((None, tm, d), lambda bi, i: (bi, i, 0)),
                  pl.BlockSpec((None, 6, d), lambda bi, i: (bi, 0, 0)),
                  pl.BlockSpec((d, n), lambda bi, i: (0, 0), pipeline_mode=pl.Buffered(1))],
        out_specs=pl.BlockSpec((None, tm, n), lambda bi, i: (bi, i, 0)),
        compiler_params=_cparams(2),
        name="in_proj",
    )(x, mod, w_in_bf)


def _norm_rope(y, g, cos_t, sin_t, scale):
    ms = jnp.mean(y * y, axis=-1, keepdims=True)
    y = y * lax.rsqrt(ms + EPS) * g
    lane = lax.broadcasted_iota(jnp.int32, y.shape, 1)
    up = pltpu.roll(y, HEAD_DIM - 32, 1)
    dn = pltpu.roll(y, 32, 1)
    partner = jnp.where((lane % 64) < 32, up, dn)
    out = y * cos_t + partner * sin_t
    if scale != 1.0:
        out = out * scale
    return out.astype(BF)


def _qkprep_body(q_ref, k_ref, cos_ref, sin_ref, gq_ref, gk_ref, qo_ref, ko_ref):
    cos_t = cos_ref[...]
    sin_t = sin_ref[...]
    for h in range(N_HEADS):
        sl = slice(h * HEAD_DIM, (h + 1) * HEAD_DIM)
        qo_ref[:, sl] = _norm_rope(q_ref[:, sl].astype(F32), gq_ref[...], cos_t, sin_t, Q_SCALE)
    for h in range(N_KV_HEADS):
        sl = slice(h * HEAD_DIM, (h + 1) * HEAD_DIM)
        ko_ref[:, sl] = _norm_rope(k_ref[:, sl].astype(F32), gk_ref[...], cos_t, sin_t, 1.0)


def _qk_prep(p, cos_t, sin_t, gq, gk):
    b, l, _ = p.shape
    tm = min(l, 512)
    return pl.pallas_call(
        _qkprep_body,
        out_shape=(jax.ShapeDtypeStruct((b, l, Q_W), BF), jax.ShapeDtypeStruct((b, l, KV_W), BF)),
        grid=(b, l // tm),
        in_specs=[pl.BlockSpec((None, tm, Q_W), lambda bi, i: (bi, i, P_Q // Q_W)),
                  pl.BlockSpec((None, tm, KV_W), lambda bi, i: (bi, i, P_K // KV_W)),
                  pl.BlockSpec((tm, HEAD_DIM), lambda bi, i: (i, 0)),
                  pl.BlockSpec((tm, HEAD_DIM), lambda bi, i: (i, 0)),
                  pl.BlockSpec((1, HEAD_DIM), lambda bi, i: (0, 0)),
                  pl.BlockSpec((1, HEAD_DIM), lambda bi, i: (0, 0))],
        out_specs=(pl.BlockSpec((None, tm, Q_W), lambda bi, i: (bi, i, 0)),
                   pl.BlockSpec((None, tm, KV_W), lambda bi, i: (bi, i, 0))),
        compiler_params=_cparams(2),
        name="qk_prep",
    )(p, p, cos_t, sin_t, gq.reshape(1, HEAD_DIM), gk.reshape(1, HEAD_DIM))


def _rope_tables(l, with_positions):
    quarter = HEAD_DIM // 4
    if not with_positions:
        return jnp.ones((l, HEAD_DIM), F32), jnp.zeros((l, HEAD_DIM), F32)
    t = jnp.arange(l)
    rows = (t // GRID_W).astype(F32)
    cols = (t % GRID_W).astype(F32)
    inv = jnp.power(ROPE_THETA, -jnp.arange(quarter, dtype=F32) / quarter)
    ar = rows[:, None] * inv[None, :]
    ac = cols[:, None] * inv[None, :]
    cos_t = jnp.concatenate([jnp.cos(ar), jnp.cos(ar), jnp.cos(ac), jnp.cos(ac)], axis=-1)
    sin_t = jnp.concatenate([-jnp.sin(ar), jnp.sin(ar), -jnp.sin(ac), jnp.sin(ac)], axis=-1)
    return cos_t, sin_t


def _flash_body(tk, q_ref, kt_ref, v_ref, o_ref, qs_scr, sa_scr, sb_scr, m_scr, accl_scr):
    tq = q_ref.shape[0]
    nk = kt_ref.shape[1] // tk
    for h in range(GQA_GROUP):
        qs_scr[h * tq:(h + 1) * tq, :] = q_ref[:, h * HEAD_DIM:(h + 1) * HEAD_DIM]
    m_scr[...] = jnp.full(m_scr.shape, -jnp.inf, F32)
    accl_scr[...] = jnp.zeros(accl_scr.shape, F32)

    def scores(i, s_scr):
        c0 = pl.multiple_of(i * tk, tk)
        s_scr[...] = jnp.dot(qs_scr[...], kt_ref[:, pl.ds(c0, tk)], preferred_element_type=F32)

    def consume(i, s_scr):
        r0 = pl.multiple_of(i * tk, tk)
        s = s_scr[...]
        m_prev = m_scr[...]
        m_new = jnp.maximum(m_prev, jnp.max(s, axis=-1, keepdims=True))
        alpha = jnp.exp2(m_prev - m_new)
        p = jnp.concatenate([jnp.exp2(s[:, t * HEAD_DIM:(t + 1) * HEAD_DIM] - m_new)
                             for t in range(tk // HEAD_DIM)], axis=1).astype(BF)
        v = v_ref[pl.ds(r0, tk), :]
        upd = jnp.dot(p, jnp.concatenate([v, jnp.ones_like(v)], axis=1), preferred_element_type=F32)
        accl_scr[...] = jnp.concatenate([alpha, alpha], axis=1) * accl_scr[...] + upd
        m_scr[...] = m_new

    scores(0, sa_scr)

    def pair(j, carry):
        scores(2 * j + 1, sb_scr)
        consume(2 * j, sa_scr)
        scores(2 * j + 2, sa_scr)
        consume(2 * j + 1, sb_scr)
        return carry

    lax.fori_loop(0, (nk - 1) // 2, pair, 0)
    if nk % 2 == 1:
        consume(nk - 1, sa_scr)
    else:
        scores(nk - 1, sb_scr)
        consume(nk - 2, sa_scr)
        consume(nk - 1, sb_scr)
    for h in range(GQA_GROUP):
        rows = slice(h * tq, (h + 1) * tq)
        o_ref[:, h * HEAD_DIM:(h + 1) * HEAD_DIM] = (accl_scr[rows, :HEAD_DIM] / accl_scr[rows, HEAD_DIM:]).astype(BF)


def _attention(q, kt, v):
    b, lq, _ = q.shape
    lk = kt.shape[3]
    tq = min(lq, ATT_TQ)
    tk = ATT_TK if lk % ATT_TK == 0 else 256
    gw = GQA_GROUP * HEAD_DIM
    m = GQA_GROUP * tq
    return pl.pallas_call(
        functools.partial(_flash_body, tk),
        out_shape=jax.ShapeDtypeStruct((b, lq, Q_W), BF),
        grid=(b, N_KV_HEADS, lq // tq),
        in_specs=[pl.BlockSpec((None, tq, gw), lambda bi, g, i: (bi, i, g)),
                  pl.BlockSpec((None, None, HEAD_DIM, lk), lambda bi, g, i: (bi, g, 0, 0)),
                  pl.BlockSpec((None, lk, HEAD_DIM), lambda bi, g, i: (bi, 0, g))],
        out_specs=pl.BlockSpec((None, tq, gw), lambda bi, g, i: (bi, i, g)),
        scratch_shapes=[pltpu.VMEM((m, HEAD_DIM), BF),
                        pltpu.VMEM((m, tk), F32),
                        pltpu.VMEM((m, tk), F32),
                        pltpu.VMEM((m, HEAD_DIM), F32),
                        pltpu.VMEM((m, 2 * HEAD_DIM), F32)],
        compiler_params=pltpu.CompilerParams(dimension_semantics=("arbitrary",) * 3,
                                             vmem_limit_bytes=VMEM_LIMIT, flags=ATT_FLAGS),
        name="gqa_attention",
    )(q, kt, v)


def _pack(lo, hi):
    ul = lax.bitcast_convert_type(lo.astype(BF).astype(F32), jnp.uint32)
    uh = lax.bitcast_convert_type(hi.astype(BF).astype(F32), jnp.uint32)
    return lax.bitcast_convert_type(uh | (ul >> 16), jnp.int32)


def _unpack_lo(x):
    return lax.bitcast_convert_type(lax.bitcast_convert_type(x, jnp.uint32) << 16, F32)


def _unpack_hi(x):
    return lax.bitcast_convert_type(lax.bitcast_convert_type(x, jnp.uint32) & jnp.uint32(0xFFFF0000), F32)


def _hypre_body(rc, uv_ref, u0_ref, u1_ref, wv_ref, w0_ref, w1_ref, bv_ref, b0_ref, b1_ref, z_ref, x0_ref):
    l = uv_ref.shape[1]
    n_chunks = l // rc

    def conv(u_ref, half, r0, w_ref, b_ref):
        cur = u_ref[half, pl.ds(r0, rc), :].astype(F32)
        p0 = jnp.maximum(r0 - 16, 0)
        n0 = jnp.minimum(r0 + rc, l - 16)
        prev_row = u_ref[half, pl.ds(pl.multiple_of(p0, 16), 16), :].astype(F32)[15:16]
        next_row = u_ref[half, pl.ds(pl.multiple_of(n0, 16), 16), :].astype(F32)[0:1]
        prev_row = jnp.where(r0 > 0, prev_row, 0.0)
        next_row = jnp.where(r0 + rc < l, next_row, 0.0)
        row = lax.broadcasted_iota(jnp.int32, cur.shape, 0)
        x_prev = jnp.where(row == 0, prev_row, pltpu.roll(cur, 1, 0))
        x_next = jnp.where(row == rc - 1, next_row, pltpu.roll(cur, rc - 1, 0))
        return b_ref[...] + x_prev * w_ref[0:1, :] + cur * w_ref[1:2, :] + x_next * w_ref[2:3, :]

    def chunk(i, carry):
        r0 = pl.multiple_of(i * rc, rc)
        zs = []
        for half in range(2):
            v = conv(uv_ref, half, r0, wv_ref, bv_ref)
            x1 = conv(u1_ref, half, r0, w1_ref, b1_ref)
            x0 = conv(u0_ref, half, r0, w0_ref, b0_ref)
            x0_ref[half, pl.ds(r0, rc), :] = x0.astype(BF)
            zs.append(v * x1)
        z_ref[pl.ds(r0, rc), :] = _pack(zs[0], zs[1])
        return carry

    lax.fori_loop(0, n_chunks, chunk, 0)


def _hyena_pre(p, conv_w, conv_b):
    b, l, n = p.shape
    hp = b // 2
    tc = LANES
    rc = min(l, 512)
    p4 = p.reshape(2, hp, l, n)
    c = HY_WIDTH
    nb = c // tc
    base = P_HY // tc

    def u_spec(g):
        return pl.BlockSpec((2, None, l, tc), lambda pi, j, _g=g: (0, pi, 0, base + _g * nb + j))

    def w_spec(g):
        return pl.BlockSpec((3, tc), lambda pi, j, _g=g: (0, _g * nb + j))

    def b_spec(g):
        return pl.BlockSpec((1, tc), lambda pi, j, _g=g: (0, _g * nb + j))

    cb = conv_b.reshape(1, 3 * c)
    z, x0 = pl.pallas_call(
        functools.partial(_hypre_body, rc),
        out_shape=(jax.ShapeDtypeStruct((hp, nb, l, tc), jnp.int32), jax.ShapeDtypeStruct((2, hp, l, c), BF)),
        grid=(hp, nb),
        in_specs=[u_spec(0), u_spec(1), u_spec(2), w_spec(0), w_spec(1), w_spec(2), b_spec(0), b_spec(1), b_spec(2)],
        out_specs=(pl.BlockSpec((None, None, l, tc), lambda pi, j: (pi, j, 0, 0)),
                   pl.BlockSpec((2, None, l, tc), lambda pi, j: (0, pi, 0, j))),
        compiler_params=_cparams(2),
        name="hyena_pre",
    )(p4, p4, p4, conv_w, conv_w, conv_w, cb, cb, cb)
    return z, x0.reshape(b, l, c)


def _filter_body(l, tr, f_ref, w1_ref, b1_ref, fr_ref, w2_ref, b2_ref, w3_ref, dl_ref, k_ref, ss_ref):
    i = pl.program_id(0)
    feats = f_ref[...]
    freq = fr_ref[...]
    h = jnp.sin(freq * (jnp.dot(feats, w1_ref[...], preferred_element_type=F32, precision=HIGHEST) + b1_ref[...]))
    h = jnp.sin(freq * (jnp.dot(h, w2_ref[...], preferred_element_type=F32, precision=HIGHEST) + b2_ref[...]))
    h = jnp.dot(h, w3_ref[...], preferred_element_type=F32, precision=HIGHEST)
    t01 = feats[:, 0:1]
    window = jnp.exp(-t01 * dl_ref[...]) + HY_SHIFT
    row = i * tr + lax.broadcasted_iota(jnp.int32, h.shape, 0)
    kern = jnp.where(row == l, 0.0, h * window)
    k_ref[...] = kern

    @pl.when(i == 0)
    def _():
        ss_ref[...] = jnp.zeros_like(ss_ref)

    ss_ref[...] += jnp.sum(kern * kern, axis=0, keepdims=True)


def _hyena_filter(l, w1, b1, freq, w2, b2, w3):
    n = 2 * l
    c = HY_WIDTH
    hid = HY_FILTER_HIDDEN
    tr = min(l, 512)
    j = jnp.arange(n)
    lag = jnp.where(j < l, j, n - j).astype(F32)
    t01 = lag / max(l - 1, 1)
    bands = jnp.linspace(1e-4, HY_BANDS - 1, HY_BANDS, dtype=F32)
    ang = (2.0 * math.pi / l) * lag[:, None] * bands[None, :]
    feats = jnp.concatenate([t01[:, None], jnp.cos(ang), -jnp.sin(ang)], axis=-1)
    feats = jnp.pad(feats, ((0, 0), (0, hid - HY_EMB)))
    w1p = jnp.pad(w1, ((0, hid - HY_EMB), (0, 0)))
    deltas = jnp.abs(jnp.linspace(HY_MIN_DECAY, HY_MAX_DECAY, c, dtype=F32)).reshape(1, c)
    nt = l // tr
    return pl.pallas_call(
        functools.partial(_filter_body, l, tr),
        out_shape=(jax.ShapeDtypeStruct((n, c), F32), jax.ShapeDtypeStruct((1, c), F32)),
        grid=(n // tr,),
        in_specs=[pl.BlockSpec((tr, hid), lambda i: (i, 0)),
                  pl.BlockSpec((hid, hid), lambda i: (0, 0)),
                  pl.BlockSpec((1, hid), lambda i: (0, 0)),
                  pl.BlockSpec((1, hid), lambda i: (0, 0)),
                  pl.BlockSpec((hid, hid), lambda i: (0, 0)),
                  pl.BlockSpec((1, hid), lambda i: (0, 0)),
                  pl.BlockSpec((hid, c), lambda i: (0, i // nt)),
                  pl.BlockSpec((1, c), lambda i: (0, 0))],
        out_specs=(pl.BlockSpec((tr, c), lambda i: (i, 0)),
                   pl.BlockSpec((1, c), lambda i: (0, 0))),
        compiler_params=_cparams(1),
        name="hyena_filter",
    )(feats, w1p, b1.reshape(1, hid), freq.reshape(1, hid), w2, b2.reshape(1, hid), w3, deltas)


def _split(l):
    n = 2 * l
    n1 = n // FFT_INNER if n > 4 * FFT_INNER else 1
    return n1, n // n1


def _interleave_cols(a, b):
    r, k = a.shape
    return np.stack([a, b], axis=-1).reshape(r, 2 * k)


@functools.lru_cache(maxsize=None)
def _fft_tables(l):
    n1, n2 = _split(l)
    n = 2 * l
    f64 = np.float64
    t = {}
    n2in = n2 if n1 > 1 else l
    k2 = np.arange(n2, dtype=f64)[:, None]
    t2 = np.arange(n2in, dtype=f64)[None, :]
    ang2 = -2.0 * np.pi * k2 * t2 / n2
    f2re, f2im = np.cos(ang2), np.sin(ang2)
    t['cre2'] = _interleave_cols(f2re, f2re)
    t['cim2'] = _interleave_cols(f2im, f2im)
    k1 = np.arange(n1, dtype=f64)[:, None]
    angw = -2.0 * np.pi * k1 * np.arange(n2in, dtype=f64)[None, :] / n
    wre, wim = np.cos(angw), np.sin(angw)
    t['w4'] = np.stack([_interleave_cols(wre, -wim), _interleave_cols(-wim, -wre),
                        _interleave_cols(wim, wre), _interleave_cols(wre, -wim)], axis=1)
    n2out = n2 if n1 > 1 else l
    tt = np.arange(n2out, dtype=f64)[:, None]
    kk = np.arange(n2, dtype=f64)[None, :]
    angc = -2.0 * np.pi * tt * kk / n2
    cr, ci = np.cos(angc), np.sin(angc)
    t['gc'] = np.block([[cr, ci], [-ci, cr]])
    k2f = np.arange(n2, dtype=f64)[:, None]
    t2f = np.arange(n2, dtype=f64)[None, :]
    angf = -2.0 * np.pi * k2f * t2f / n2
    t['f2re'], t['f2im'] = np.cos(angf), np.sin(angf)
    angwf = -2.0 * np.pi * k1 * np.arange(n2, dtype=f64)[None, :] / n
    t['w2'] = np.stack([np.cos(angwf), np.sin(angwf)], axis=1)
    if n1 > 1:
        h = n1 // 2
        kk1 = np.arange(n1, dtype=f64)[:, None]
        ang1 = -2.0 * np.pi * kk1 * np.arange(h, dtype=f64)[None, :] / n1
        f1re, f1im = np.cos(ang1), np.sin(ang1)
        t['m1'] = np.concatenate([_interleave_cols(f1re, -f1im), _interleave_cols(f1im, f1re)], axis=0)
        d1re, d1im = f1re.T, f1im.T
        t['d1re2'] = _interleave_cols(d1re, d1re)
        t['d1im2'] = _interleave_cols(d1im, d1im)
        wre_t, wim_t = wre.T, wim.T
        t['v4'] = np.stack([_interleave_cols(wre_t, wim_t), _interleave_cols(-wim_t, wre_t),
                            _interleave_cols(-wim_t, wre_t), _interleave_cols(-wre_t, -wim_t)], axis=1)
        ang1f = -2.0 * np.pi * kk1 * np.arange(n1, dtype=f64)[None, :] / n1
        t['m1k'] = np.concatenate([np.cos(ang1f), np.sin(ang1f)], axis=0)
    return {k: np.asarray(v, np.float32) for k, v in t.items()}


def _kfft_body(l, *refs):
    n1, n2 = _split(l)
    n = 2 * l
    if n1 > 1:
        k_ref, ss_ref, m1k_ref, f2re_ref, f2im_ref, w2_ref, o_ref, are_scr, aim_scr = refs
    else:
        k_ref, ss_ref, f2re_ref, f2im_ref, o_ref = refs
    scale = lax.rsqrt(ss_ref[...] + EPS) * (1.0 / n)

    if n1 > 1:
        pitch = are_scr.shape[0] // n1

        def stage1(t2, carry):
            xin = k_ref[pl.ds(t2, n1, stride=n2), :]
            o = jnp.dot(m1k_ref[...], xin, preferred_element_type=F32, precision=KF_PRECISION)
            are_scr[pl.ds(t2, n1, stride=pitch), :] = o[:n1]
            aim_scr[pl.ds(t2, n1, stride=pitch), :] = o[n1:]
            return carry

        lax.fori_loop(0, n2, stage1, 0, unroll=KF_UNROLL)

        def slab(k1, carry):
            r0 = pl.multiple_of(k1 * pitch, 8)
            s = jnp.concatenate([are_scr[pl.ds(r0, n2), :], aim_scr[pl.ds(r0, n2), :]], axis=0)
            w2 = w2_ref[k1]
            wre, wim = w2[0:1], w2[1:2]
            hre = f2re_ref[...] * wre - f2im_ref[...] * wim
            him = f2re_ref[...] * wim + f2im_ref[...] * wre
            hblk = jnp.concatenate([jnp.concatenate([hre, -him], axis=1),
                                    jnp.concatenate([him, hre], axis=1)], axis=0)
            x = jnp.dot(hblk, s, preferred_element_type=F32, precision=KF_PRECISION)
            o_ref[k1] = _pack(x[:n2] * scale, x[n2:] * scale)
            return carry

        lax.fori_loop(0, n1, slab, 0, unroll=KF_UNROLL)
    else:
        f = jnp.concatenate([f2re_ref[...], f2im_ref[...]], axis=0)
        x = jnp.dot(f, k_ref[...], preferred_element_type=F32, precision=KF_PRECISION)
        o_ref[0] = _pack(x[:n2] * scale, x[n2:] * scale)


def _kernel_spectrum(l, kern, sumsq):
    n1, n2 = _split(l)
    n = 2 * l
    c = kern.shape[1]
    ct = 128
    tb = _fft_tables(l)
    names = (['m1k'] if n1 > 1 else []) + ['f2re', 'f2im'] + (['w2'] if n1 > 1 else [])
    consts = [jnp.asarray(tb[k]) for k in names]
    scratch = [pltpu.VMEM((n1 * (n2 + 8), ct), F32)] * 2 if n1 > 1 else []
    return pl.pallas_call(
        functools.partial(_kfft_body, l),
        out_shape=jax.ShapeDtypeStruct((n1, n2, c), jnp.int32),
        grid=(c // ct,),
        in_specs=[pl.BlockSpec((n, ct), lambda j: (0, j), pipeline_mode=pl.Buffered(1)),
                  pl.BlockSpec((1, ct), lambda j: (0, j))] + [_const_spec(a.shape, 1) for a in consts],
        out_specs=pl.BlockSpec((n1, n2, ct), lambda j: (0, 0, j)),
        scratch_shapes=scratch,
        compiler_params=_cparams(1),
        name="hyena_filter_fft",
    )(kern, sumsq, *consts)


def _fft_body(l, *refs):
    n1, n2 = _split(l)
    if n1 > 1:
        (z_ref, kf_ref, cre2_ref, cim2_ref, w4_ref, gc_ref, m1_ref, d1re2_ref, d1im2_ref, v4_ref,
         out_ref, a_scr) = refs
        h = n1 // 2
        ctn = z_ref.shape[0]
        a_pitch = a_scr.shape[1] // n1

        def load(ref, rows):
            return jnp.concatenate([ref[c, rows, :] for c in range(ctn)], axis=1)

        def store(ref, rows, val):
            for c in range(ctn):
                ref[c, rows, :] = val[:, c * LANES:(c + 1) * LANES]

        def stage1(t2, carry):
            xin = load(z_ref, pl.ds(t2, h, stride=n2))
            xb = pltpu.bitcast(xin, BF)
            o = jnp.dot(m1_ref[...], xb, preferred_element_type=F32)
            store(a_scr, pl.ds(t2, n1, stride=a_pitch), _pack(o[:n1], o[n1:]))
            return carry

        lax.fori_loop(0, n2, stage1, 0, unroll=FFT_UNROLL)

        def slab(k1, carry):
            r0 = pl.multiple_of(k1 * a_pitch, 8)
            s = pltpu.bitcast(load(a_scr, pl.ds(r0, n2)), BF)
            w4 = w4_ref[k1]
            cre2 = cre2_ref[...]
            cim2 = cim2_ref[...]
            top = cre2 * w4[0:1] + cim2 * w4[1:2]
            bot = cre2 * w4[2:3] + cim2 * w4[3:4]
            hblk = jnp.concatenate([top, bot], axis=0).astype(BF)
            x = jnp.dot(hblk, s, preferred_element_type=F32)
            kf = kf_ref[k1]
            kre, kim = _unpack_lo(kf), _unpack_hi(kf)
            xre, xim = x[:n2], x[n2:]
            y = jnp.concatenate([xre * kre - xim * kim, xre * kim + xim * kre], axis=0).astype(BF)
            b = jnp.dot(gc_ref[...], y, preferred_element_type=F32)
            store(a_scr, pl.ds(r0, n2), _pack(b[:n2], b[n2:]))
            return carry

        lax.fori_loop(0, n1, slab, 0, unroll=FFT_UNROLL)

        def stage3(t2, carry):
            bin_ = pltpu.bitcast(load(a_scr, pl.ds(t2, n1, stride=a_pitch)), BF)
            v4 = v4_ref[t2]
            d1re2 = d1re2_ref[...]
            d1im2 = d1im2_ref[...]
            top = d1re2 * v4[0:1] + d1im2 * v4[1:2]
            bot = d1re2 * v4[2:3] + d1im2 * v4[3:4]
            m3 = jnp.concatenate([top, bot], axis=0).astype(BF)
            o = jnp.dot(m3, bin_, preferred_element_type=F32)
            store(out_ref, pl.ds(t2, h, stride=n2), _pack(o[:h], o[h:]))
            return carry

        lax.fori_loop(0, n2, stage3, 0, unroll=FFT_UNROLL)
    else:
        z_ref, kf_ref, cre2_ref, cim2_ref, w4_ref, gc_ref, out_ref = refs
        ctn = z_ref.shape[0]
        s = pltpu.bitcast(jnp.concatenate([z_ref[c] for c in range(ctn)], axis=1), BF)
        w4 = w4_ref[0]
        top = cre2_ref[...] * w4[0:1] + cim2_ref[...] * w4[1:2]
        bot = cre2_ref[...] * w4[2:3] + cim2_ref[...] * w4[3:4]
        hblk = jnp.concatenate([top, bot], axis=0).astype(BF)
        x = jnp.dot(hblk, s, preferred_element_type=F32)
        kf = kf_ref[0]
        kre, kim = _unpack_lo(kf), _unpack_hi(kf)
        xre, xim = x[:n2], x[n2:]
        y = jnp.concatenate([xre * kre - xim * kim, xre * kim + xim * kre], axis=0).astype(BF)
        b = jnp.dot(gc_ref[...], y, preferred_element_type=F32)
        packed = _pack(b[:l], b[l:])
        for c in range(ctn):
            out_ref[c] = packed[:, c * LANES:(c + 1) * LANES]


def _fft_conv(zp, kf):
    hp, nct, l, _ = zp.shape
    n1, n2 = _split(l)
    ct = FFT_CT
    ctn = ct // LANES
    tb = _fft_tables(l)
    names = ['cre2', 'cim2', 'w4', 'gc'] + (['m1', 'd1re2', 'd1im2', 'v4'] if n1 > 1 else [])
    consts = [jnp.asarray(tb[k]).astype(BF) if k in ('gc', 'm1') else jnp.asarray(tb[k]) for k in names]
    scratch = [pltpu.VMEM((ctn, n1 * (n2 + 8), LANES), jnp.int32)] if n1 > 1 else []
    one = dict(pipeline_mode=pl.Buffered(1))
    return pl.pallas_call(
        functools.partial(_fft_body, l),
        out_shape=jax.ShapeDtypeStruct((hp, nct, l, LANES), jnp.int32),
        grid=(nct // ctn, hp),
        in_specs=[pl.BlockSpec((None, ctn, l, LANES), lambda j, p: (p, j, 0, 0), **one),
                  pl.BlockSpec((n1, n2, ct), lambda j, p: (0, 0, j), **one)] + [_const_spec(a.shape, 2) for a in consts],
        out_specs=pl.BlockSpec((None, ctn, l, LANES), lambda j, p: (p, j, 0, 0), **one),
        scratch_shapes=scratch,
        compiler_params=_cparams(2),
        name="hyena_fft_conv",
    )(zp, kf, *consts)


def _pool_body(rc, u_ref, w_ref, sc_ref, o_ref):
    l = u_ref.shape[0]
    g = pl.program_id(1)
    win = jnp.left_shift(2, g)
    before = win // 2
    after = win - 1 - before
    n_chunks = l // rc
    halo = 64

    def chunk(i, carry):
        r0 = pl.multiple_of(i * rc, rc)
        p0 = pl.multiple_of(jnp.maximum(r0 - halo, 0), halo)
        n0 = pl.multiple_of(jnp.minimum(r0 + rc, l - halo), halo)
        cur = u_ref[pl.ds(r0, rc), :]
        ext = jnp.concatenate([u_ref[pl.ds(p0, halo), :], cur, u_ref[pl.ds(n0, halo), :]], axis=0)
        t = r0 + lax.broadcasted_iota(jnp.int32, (rc, rc + 2 * halo), 0)
        s = r0 - halo + lax.broadcasted_iota(jnp.int32, (rc, rc + 2 * halo), 1)
        inside = (s >= t - before) & (s <= t + after) & (s >= 0) & (s < l)
        band = jnp.where(inside, 1.0, 0.0).astype(BF)
        sums = jnp.dot(band, ext, preferred_element_type=F32)
        tr = r0 + lax.broadcasted_iota(jnp.int32, (rc, POOL_GROUP), 0)
        cnt = jnp.minimum(tr + after + 1, l) - jnp.maximum(tr - before, 0)
        m = sums / cnt.astype(F32) - cur.astype(F32)
        y = jnp.dot(m.astype(BF), w_ref[...], preferred_element_type=F32) * sc_ref[...]
        o_ref[pl.ds(r0, rc), :] = y.astype(BF)
        return carry

    lax.fori_loop(0, n_chunks, chunk, 0)


def _pool(p, pool_w_bf, pool_scale):
    b, l, _ = p.shape
    rc = min(l, 512)
    ng = len(POOL_WINDOWS)
    base = P_POOL // POOL_GROUP
    return pl.pallas_call(
        functools.partial(_pool_body, rc),
        out_shape=jax.ShapeDtypeStruct((b, l, D_MODEL), BF),
        grid=(b, ng),
        in_specs=[pl.BlockSpec((None, l, POOL_GROUP), lambda bi, g: (bi, 0, base + g)),
                  pl.BlockSpec((None, POOL_GROUP, POOL_GROUP), lambda bi, g: (g, 0, 0)),
                  pl.BlockSpec((1, POOL_GROUP), lambda bi, g: (0, g))],
        out_specs=pl.BlockSpec((None, l, POOL_GROUP), lambda bi, g: (bi, 0, g)),
        compiler_params=_cparams(2),
        name="pool_mixer",
    )(p, pool_w_bf, pool_scale.reshape(1, D_MODEL))


def _mix_body(hp, x_ref, gate_ref, y_ref, z_ref, x0_ref, hd_ref, at_ref, po_ref, wb_ref, wo_ref,
              mod_ref, g_ref, b_ref, o_ref):
    hi = pl.program_id(0) >= hp
    nct = y_ref.shape[0]
    yw = jnp.concatenate([y_ref[c] for c in range(nct)], axis=1)
    zw = jnp.concatenate([z_ref[c] for c in range(nct)], axis=1)
    y = jnp.where(hi, _unpack_hi(yw), _unpack_lo(yw))
    z = jnp.where(hi, _unpack_hi(zw), _unpack_lo(zw))
    hy = ((y + z * hd_ref[...]) * x0_ref[...].astype(F32)).astype(BF)
    d = D_MODEL
    merged = jax.nn.sigmoid(gate_ref[:, 0:d].astype(F32)) * jnp.dot(hy, wb_ref[0], preferred_element_type=F32)
    merged += jax.nn.sigmoid(gate_ref[:, d:2 * d].astype(F32)) * jnp.dot(at_ref[...], wb_ref[1],
                                                                         preferred_element_type=F32)
    merged += jax.nn.sigmoid(gate_ref[:, 2 * d:].astype(F32)) * jnp.dot(po_ref[...], wb_ref[2],
                                                                        preferred_element_type=F32)
    out = jnp.dot(merged.astype(BF), wo_ref[...], preferred_element_type=F32)
    r = DN_ALPHA * x_ref[...] + mod_ref[2:3, :] * out
    o_ref[...] = _layer_norm(r, g_ref[...], b_ref[...])


def _mix(x, p, y_pair, z_pair, x0c, hy_d, attn, pool, wb_bf, wo_bf, mod, ln_g, ln_b):
    b, l, d = x.shape
    hp = b // 2
    tm = min(l, 512)
    row = lambda bi, i: (bi, i, 0)
    pair = lambda bi, i: (bi % hp, 0, i, 0)
    nct = d // LANES
    return pl.pallas_call(
        functools.partial(_mix_body, hp),
        out_shape=jax.ShapeDtypeStruct((b, l, d), F32),
        grid=(b, l // tm),
        in_specs=[pl.BlockSpec((None, tm, d), row),
                  pl.BlockSpec((None, tm, 3 * d), lambda bi, i: (bi, i, P_GATE // (3 * d))),
                  pl.BlockSpec((None, nct, tm, LANES), pair),
                  pl.BlockSpec((None, nct, tm, LANES), pair),
                  pl.BlockSpec((None, tm, d), row),
                  pl.BlockSpec((1, d), lambda bi, i: (0, 0)),
                  pl.BlockSpec((None, tm, d), row),
                  pl.BlockSpec((None, tm, d), row),
                  pl.BlockSpec((3, d, d), lambda bi, i: (0, 0, 0), pipeline_mode=pl.Buffered(1)),
                  pl.BlockSpec((d, d), lambda bi, i: (0, 0), pipeline_mode=pl.Buffered(1)),
                  pl.BlockSpec((None, 6, d), lambda bi, i: (bi, 0, 0)),
                  pl.BlockSpec((1, d), lambda bi, i: (0, 0)),
                  pl.BlockSpec((1, d), lambda bi, i: (0, 0))],
        out_specs=pl.BlockSpec((None, tm, d), row),
        compiler_params=_cparams(2),
        name="branch_mix",
    )(x, p, y_pair, z_pair, x0c, hy_d.reshape(1, d), attn, pool, wb_bf, wo_bf, mod,
      ln_g.reshape(1, d), ln_b.reshape(1, d))


def _ffn_body(n_chunks, x_ref, mod_ref, w1_ref, w3_ref, w2_ref, g_ref, b_ref, o_ref):
    x = x_ref[...]
    h = (x * (1.0 + mod_ref[4:5, :]) + mod_ref[3:4, :]).astype(BF)
    fc = D_FF // n_chunks
    acc = jnp.zeros(x.shape, F32)
    for c in range(n_chunks):
        sl = slice(c * fc, (c + 1) * fc)
        a = jnp.dot(h, w1_ref[:, sl], preferred_element_type=F32)
        bb = jnp.dot(h, w3_ref[:, sl], preferred_element_type=F32)
        gg = (a * jax.nn.sigmoid(a) * bb).astype(BF)
        acc += jnp.dot(gg, w2_ref[sl, :], preferred_element_type=F32)
    r = DN_ALPHA * x + mod_ref[5:6, :] * acc
    o_ref[...] = _layer_norm(r, g_ref[...], b_ref[...])


def _ffn(x, mod, w1_bf, w3_bf, w2_bf, ln_g, ln_b):
    b, l, d = x.shape
    tm = min(l, 512)
    row = lambda bi, i: (bi, i, 0)
    return pl.pallas_call(
        functools.partial(_ffn_body, 2),
        out_shape=jax.ShapeDtypeStruct((b, l, d), F32),
        grid=(b, l // tm),
        in_specs=[pl.BlockSpec((None, tm, d), row),
                  pl.BlockSpec((None, 6, d), lambda bi, i: (bi, 0, 0)),
                  pl.BlockSpec((d, D_FF), lambda bi, i: (0, 0), pipeline_mode=pl.Buffered(1)),
                  pl.BlockSpec((d, D_FF), lambda bi, i: (0, 0), pipeline_mode=pl.Buffered(1)),
                  pl.BlockSpec((D_FF, d), lambda bi, i: (0, 0), pipeline_mode=pl.Buffered(1)),
                  pl.BlockSpec((1, d), lambda bi, i: (0, 0)),
                  pl.BlockSpec((1, d), lambda bi, i: (0, 0))],
        out_specs=pl.BlockSpec((None, tm, d), row),
        compiler_params=_cparams(2),
        name="swiglu_ffn",
    )(x, mod, w1_bf, w3_bf, w2_bf, ln_g.reshape(1, d), ln_b.reshape(1, d))


def _permute_w_in(w):
    return jnp.concatenate([w[:, C_GATE:], w[:, C_HY:C_POOL], w[:, C_Q:C_K], w[:, C_POOL:C_GATE],
                            w[:, C_K:C_V], w[:, C_V:C_HY]], axis=1).astype(BF)


def _v_cols(p):
    return p[..., P_V:P_V + KV_W]


def _keys_transposed(k):
    b, lk, _ = k.shape
    return k.reshape(b, lk, N_KV_HEADS, HEAD_DIM).transpose(0, 2, 3, 1)


def _stream_block(x, p, q_hat, k_all, v_all, kf, mod, lw):
    attn = _attention(q_hat, _keys_transposed(k_all), v_all)
    z_pair, x0c = _hyena_pre(p, lw['hy_conv_w'], lw['hy_conv_b'])
    y_pair = _fft_conv(z_pair, kf)
    pool = _pool(p, lw['pool_w'], lw['pool_scale'])
    x = _mix(x, p, y_pair, z_pair, x0c, lw['hy_d'], attn, pool, lw['w_branch'], lw['w_out'], mod,
             lw['ln1_g'], lw['ln1_b'])
    return _ffn(x, mod, lw['ffn_w1'], lw['ffn_w3'], lw['ffn_w2'], lw['ln2_g'], lw['ln2_b'])


def kernel(x, c, ctx, c_ctx, w_ada, b_ada, w_in, q_norm_g, k_norm_g, hy_conv_w, hy_conv_b, hf_w1, hf_b1, hf_freq,
           hf_w2, hf_b2, hf_w3, hy_d, pool_w, pool_scale, w_branch, w_out, ln1_g, ln1_b, ln2_g, ln2_b,
           ffn_w1, ffn_w3, ffn_w2):
    b, l, d = x.shape
    lc = ctx.shape[1]
    depth = w_ada.shape[0]
    assert b % 2 == 0 and l % GRID_W == 0

    rows = 16
    c_all = jnp.zeros((rows, d), F32).at[:b].set(c).at[b].set(c_ctx)
    cos_l, sin_l = _rope_tables(l, True)
    cos_c, sin_c = _rope_tables(lc, False)

    xl, xc = x, ctx
    for li in range(depth):
        last = li == depth - 1
        lw = dict(hy_conv_w=hy_conv_w[li], hy_conv_b=hy_conv_b[li], hy_d=hy_d[li],
                  pool_w=pool_w[li].astype(BF), pool_scale=pool_scale[li],
                  w_branch=w_branch[li].astype(BF), w_out=w_out[li].astype(BF),
                  ln1_g=ln1_g[li], ln1_b=ln1_b[li], ln2_g=ln2_g[li], ln2_b=ln2_b[li],
                  ffn_w1=ffn_w1[li].astype(BF), ffn_w3=ffn_w3[li].astype(BF), ffn_w2=ffn_w2[li].astype(BF))
        w_in_p = _permute_w_in(w_in[li])

        mod = _ada(c_all, w_ada[li], b_ada[li]).reshape(rows, 6, d)
        mod_l = mod[:b]
        mod_c = jnp.broadcast_to(mod[b:b + 1], (b, 6, d))

        filt = (hf_w1[li], hf_b1[li], hf_freq[li], hf_w2[li], hf_b2[li], hf_w3[li])
        kf_l = _kernel_spectrum(l, *_hyena_filter(l, *filt))

        p_l = _in_proj(xl, mod_l, w_in_p)
        p_c = _in_proj(xc, mod_c, w_in_p)
        q_l, k_l = _qk_prep(p_l, cos_l, sin_l, q_norm_g[li], k_norm_g[li])
        q_c, k_c = _qk_prep(p_c, cos_c, sin_c, q_norm_g[li], k_norm_g[li])
        v_c = _v_cols(p_c)
        k_all = jnp.concatenate([k_c, k_l], axis=1)
        v_all = jnp.concatenate([v_c, _v_cols(p_l)], axis=1)

        xl = _stream_block(xl, p_l, q_l, k_all, v_all, kf_l, mod_l, lw)
        if not last:
            kf_c = _kernel_spectrum(lc, *_hyena_filter(lc, *filt))
            xc = _stream_block(xc, p_c, q_c, k_c, v_c, kf_c, mod_c, lw)
    return xl
```

```python
import functools
import math

import numpy as np
import jax
import jax.numpy as jnp
from jax import lax
from jax.experimental import pallas as pl
from jax.experimental.pallas import tpu as pltpu

F32 = jnp.float32
BF = jnp.bfloat16
HIGHEST = lax.Precision.HIGHEST

D_MODEL = 1024
GRID_W = 64
N_HEADS = 8
N_KV_HEADS = 2
HEAD_DIM = 128
GQA_GROUP = N_HEADS // N_KV_HEADS
ROPE_THETA = 10000.0
Q_W = N_HEADS * HEAD_DIM
KV_W = N_KV_HEADS * HEAD_DIM
HY_WIDTH = D_MODEL
HY_EMB = 33
HY_BANDS = (HY_EMB - 1) // 2
HY_FILTER_HIDDEN = 64
HY_TARGET = 1e-2
HY_MAX_DECAY = math.log(HY_TARGET) / 0.3
HY_MIN_DECAY = math.log(HY_TARGET) / 1.5
HY_SHIFT = 0.05
POOL_WINDOWS = (2, 4, 8, 16)
POOL_GROUP = D_MODEL // len(POOL_WINDOWS)
D_FF = 2816
DEPTH = 2
DN_ALPHA = (2 * DEPTH) ** 0.25
EPS = 1e-6

C_Q = 0
C_K = C_Q + Q_W
C_V = C_K + KV_W
C_HY = C_V + KV_W
C_POOL = C_HY + 3 * HY_WIDTH
C_GATE = C_POOL + D_MODEL
IN_WIDTH = C_GATE + 3 * D_MODEL

P_GATE = 0
P_HY = 3 * D_MODEL
P_Q = P_HY + 3 * HY_WIDTH
P_POOL = P_Q + Q_W
P_K = P_POOL + D_MODEL
P_V = P_K + KV_W

Q_SCALE = HEAD_DIM ** -0.5 * math.log2(math.e)

ATT_TQ = 256
ATT_TK = 768
ATT_FLAGS = None
FFT_CT = 256
FFT_UNROLL = 8
KF_UNROLL = 4

VMEM_LIMIT = 60 * 1024 * 1024
FFT_INNER = 128
LANES = 128


def _cparams(n_axes):
    return pltpu.CompilerParams(dimension_semantics=("arbitrary",) * n_axes, vmem_limit_bytes=VMEM_LIMIT)


def _const_spec(shape, n_grid):
    nd = len(shape)
    return pl.BlockSpec(shape, lambda *g, _nd=nd: (0,) * _nd)


def _layer_norm(r, g, b):
    mu = jnp.mean(r, axis=-1, keepdims=True)
    d = r - mu
    var = jnp.mean(d * d, axis=-1, keepdims=True)
    return d * lax.rsqrt(var + EPS) * g + b


def _ada_body(c_ref, w_ref, b_ref, o_ref):
    c = c_ref[...]
    s = c * jax.nn.sigmoid(c)
    o_ref[...] = jnp.dot(s, w_ref[...], preferred_element_type=F32, precision=HIGHEST) + b_ref[...]


def _ada(c_all, w_ada, b_ada):
    rows, d = c_all.shape
    n = w_ada.shape[1]
    tn = 512
    return pl.pallas_call(
        _ada_body,
        out_shape=jax.ShapeDtypeStruct((rows, n), F32),
        grid=(n // tn,),
        in_specs=[pl.BlockSpec((rows, d), lambda j: (0, 0)),
                  pl.BlockSpec((d, tn), lambda j: (0, j)),
                  pl.BlockSpec((1, tn), lambda j: (0, j))],
        out_specs=pl.BlockSpec((rows, tn), lambda j: (0, j)),
        compiler_params=_cparams(1),
        name="ada_mod",
    )(c_all, w_ada, b_ada.reshape(1, n))


def _norm_rope(y, g, cos_t, sin_t, scale):
    ms = jnp.mean(y * y, axis=-1, keepdims=True)
    y = y * lax.rsqrt(ms + EPS) * g
    lane = lax.broadcasted_iota(jnp.int32, y.shape, 1)
    up = pltpu.roll(y, HEAD_DIM - 32, 1)
    dn = pltpu.roll(y, 32, 1)
    partner = jnp.where((lane % 64) < 32, up, dn)
    out = y * cos_t + partner * sin_t
    if scale != 1.0:
        out = out * scale
    return out


def _inproj_body(tn, x_ref, mod_ref, w_ref, cos_ref, sin_ref, gq_ref, gk_ref, o_ref):
    h = (x_ref[...] * (1.0 + mod_ref[1:2, :]) + mod_ref[0:1, :]).astype(BF)
    chunks = list(range(w_ref.shape[1] // tn))
    with_epilogue = [j for j in chunks if j * tn < P_Q + Q_W and (j + 1) * tn > P_Q or j * tn < P_K + KV_W and (j + 1) * tn > P_K]
    for j in with_epilogue + [j for j in chunks if j not in with_epilogue]:
        c0 = j * tn
        r = jnp.dot(h, w_ref[:, c0:c0 + tn], preferred_element_type=F32)
        heads = []
        for hh in range(tn // HEAD_DIM):
            col = c0 + hh * HEAD_DIM
            y = r[:, hh * HEAD_DIM:(hh + 1) * HEAD_DIM]
            if P_Q <= col < P_Q + Q_W:
                y = _norm_rope(y, gq_ref[...], cos_ref[...], sin_ref[...], Q_SCALE)
            elif P_K <= col < P_K + KV_W:
                y = _norm_rope(y, gk_ref[...], cos_ref[...], sin_ref[...], 1.0)
            heads.append(y)
        o_ref[:, c0:c0 + tn] = jnp.concatenate(heads, axis=1).astype(BF)


def _in_proj(x, mod, w_in_bf, cos_t, sin_t, gq, gk):
    b, l, d = x.shape
    n = w_in_bf.shape[1]
    tm = min(l, 512)
    tn = 512
    vec = pl.BlockSpec((1, HEAD_DIM), lambda bi, i: (0, 0))
    tab = pl.BlockSpec((tm, HEAD_DIM), lambda bi, i: (i, 0))
    return pl.pallas_call(
        functools.partial(_inproj_body, tn),
        out_shape=jax.ShapeDtypeStruct((b, l, n), BF),
        grid=(b, l // tm),
        in_specs=[pl.BlockSpec((None, tm, d), lambda bi, i: (bi, i, 0)),
                  pl.BlockSpec((None, 6, d), lambda bi, i: (bi, 0, 0)),
                  pl.BlockSpec((d, n), lambda bi, i: (0, 0), pipeline_mode=pl.Buffered(1)),
                  tab, tab, vec, vec],
        out_specs=pl.BlockSpec((None, tm, n), lambda bi, i: (bi, i, 0)),
        compiler_params=_cparams(2),
        name="in_proj",
    )(x, mod, w_in_bf, cos_t, sin_t, gq.reshape(1, HEAD_DIM), gk.reshape(1, HEAD_DIM))


def _rope_tables(l, with_positions):
    quarter = HEAD_DIM // 4
    if not with_positions:
        return jnp.ones((l, HEAD_DIM), F32), jnp.zeros((l, HEAD_DIM), F32)
    t = jnp.arange(l)
    rows = (t // GRID_W).astype(F32)
    cols = (t % GRID_W).astype(F32)
    inv = jnp.power(ROPE_THETA, -jnp.arange(quarter, dtype=F32) / quarter)
    ar = rows[:, None] * inv[None, :]
    ac = cols[:, None] * inv[None, :]
    cos_t = jnp.concatenate([jnp.cos(ar), jnp.cos(ar), jnp.cos(ac), jnp.cos(ac)], axis=-1)
    sin_t = jnp.concatenate([-jnp.sin(ar), jnp.sin(ar), -jnp.sin(ac), jnp.sin(ac)], axis=-1)
    return cos_t, sin_t


def _flash_body(tk, q_ref, kt_ref, v_ref, o_ref, qs_scr, sa_scr, sb_scr, mpa_scr, mpb_scr, m_scr, accl_scr):
    tq = q_ref.shape[0]
    nk = kt_ref.shape[1] // tk
    for h in range(GQA_GROUP):
        qs_scr[h * tq:(h + 1) * tq, :] = q_ref[:, h * HEAD_DIM:(h + 1) * HEAD_DIM]
    m_scr[...] = jnp.full(m_scr.shape, -jnp.inf, F32)
    accl_scr[...] = jnp.zeros(accl_scr.shape, F32)

    def scores(i, s_scr, mp_scr):
        c0 = pl.multiple_of(i * tk, tk)
        s = jnp.dot(qs_scr[...], kt_ref[:, pl.ds(c0, tk)], preferred_element_type=F32)
        s_scr[...] = s
        mp = s[:, :HEAD_DIM]
        for t in range(1, tk // HEAD_DIM):
            mp = jnp.maximum(mp, s[:, t * HEAD_DIM:(t + 1) * HEAD_DIM])
        mp_scr[...] = mp

    def consume(i, s_scr, mp_scr):
        r0 = pl.multiple_of(i * tk, tk)
        s = s_scr[...]
        m_prev = m_scr[...]
        m_new = jnp.maximum(m_prev, jnp.max(mp_scr[...], axis=-1, keepdims=True))
        alpha = jnp.exp2(m_prev - m_new)
        p = jnp.concatenate([jnp.exp2(s[:, t * HEAD_DIM:(t + 1) * HEAD_DIM] - m_new)
                             for t in range(tk // HEAD_DIM)], axis=1).astype(BF)
        v = v_ref[pl.ds(r0, tk), :]
        upd = jnp.dot(p, jnp.concatenate([v, jnp.ones_like(v)], axis=1), preferred_element_type=F32)
        accl_scr[...] = jnp.concatenate([alpha, alpha], axis=1) * accl_scr[...] + upd
        m_scr[...] = m_new

    scores(0, sa_scr, mpa_scr)

    def pair(j, carry):
        scores(2 * j + 1, sb_scr, mpb_scr)
        consume(2 * j, sa_scr, mpa_scr)
        scores(2 * j + 2, sa_scr, mpa_scr)
        consume(2 * j + 1, sb_scr, mpb_scr)
        return carry

    lax.fori_loop(0, (nk - 1) // 2, pair, 0, unroll=True)
    if nk % 2 == 1:
        consume(nk - 1, sa_scr, mpa_scr)
    else:
        scores(nk - 1, sb_scr, mpb_scr)
        consume(nk - 2, sa_scr, mpa_scr)
        consume(nk - 1, sb_scr, mpb_scr)
    for h in range(GQA_GROUP):
        rows = slice(h * tq, (h + 1) * tq)
        o_ref[:, h * HEAD_DIM:(h + 1) * HEAD_DIM] = (accl_scr[rows, :HEAD_DIM] / accl_scr[rows, HEAD_DIM:]).astype(BF)


def _attention(p, kt, v):
    b, lq, _ = p.shape
    lk = kt.shape[3]
    tq = min(lq, ATT_TQ)
    tk = ATT_TK if lk % ATT_TK == 0 else 256
    gw = GQA_GROUP * HEAD_DIM
    m = GQA_GROUP * tq
    return pl.pallas_call(
        functools.partial(_flash_body, tk),
        out_shape=jax.ShapeDtypeStruct((b, lq, Q_W), BF),
        grid=(b, N_KV_HEADS, lq // tq),
        in_specs=[pl.BlockSpec((None, tq, gw), lambda bi, g, i: (bi, i, P_Q // gw + g)),
                  pl.BlockSpec((None, None, HEAD_DIM, lk), lambda bi, g, i: (bi, g, 0, 0)),
                  pl.BlockSpec((None, lk, HEAD_DIM), lambda bi, g, i: (bi, 0, g))],
        out_specs=pl.BlockSpec((None, tq, gw), lambda bi, g, i: (bi, i, g)),
        scratch_shapes=[pltpu.VMEM((m, HEAD_DIM), BF),
                        pltpu.VMEM((m, tk), F32),
                        pltpu.VMEM((m, tk), F32),
                        pltpu.VMEM((m, HEAD_DIM), F32),
                        pltpu.VMEM((m, HEAD_DIM), F32),
                        pltpu.VMEM((m, HEAD_DIM), F32),
                        pltpu.VMEM((m, 2 * HEAD_DIM), F32)],
        compiler_params=pltpu.CompilerParams(dimension_semantics=("arbitrary",) * 3,
                                             vmem_limit_bytes=VMEM_LIMIT, flags=ATT_FLAGS),
        name="gqa_attention",
    )(p, kt, v)


def _pack(lo, hi):
    ul = lax.bitcast_convert_type(lo.astype(BF).astype(F32), jnp.uint32)
    uh = lax.bitcast_convert_type(hi.astype(BF).astype(F32), jnp.uint32)
    return lax.bitcast_convert_type(uh | (ul >> 16), jnp.int32)


def _unpack_lo(x):
    return lax.bitcast_convert_type(lax.bitcast_convert_type(x, jnp.uint32) << 16, F32)


def _unpack_hi(x):
    return lax.bitcast_convert_type(lax.bitcast_convert_type(x, jnp.uint32) & jnp.uint32(0xFFFF0000), F32)


def _hypre_body(rc, uv_ref, u0_ref, u1_ref, wv_ref, w0_ref, w1_ref, bv_ref, b0_ref, b1_ref, z_ref, x0_ref):
    l = uv_ref.shape[1]
    n_chunks = l // rc

    def conv(u_ref, half, r0, w_ref, b_ref):
        cur = u_ref[half, pl.ds(r0, rc), :].astype(F32)
        p0 = jnp.maximum(r0 - 16, 0)
        n0 = jnp.minimum(r0 + rc, l - 16)
        prev_row = u_ref[half, pl.ds(pl.multiple_of(p0, 16), 16), :].astype(F32)[15:16]
        next_row = u_ref[half, pl.ds(pl.multiple_of(n0, 16), 16), :].astype(F32)[0:1]
        prev_row = jnp.where(r0 > 0, prev_row, 0.0)
        next_row = jnp.where(r0 + rc < l, next_row, 0.0)
        row = lax.broadcasted_iota(jnp.int32, cur.shape, 0)
        x_prev = jnp.where(row == 0, prev_row, pltpu.roll(cur, 1, 0))
        x_next = jnp.where(row == rc - 1, next_row, pltpu.roll(cur, rc - 1, 0))
        return b_ref[...] + x_prev * w_ref[0:1, :] + cur * w_ref[1:2, :] + x_next * w_ref[2:3, :]

    def chunk(i, carry):
        r0 = pl.multiple_of(i * rc, rc)
        zs = []
        for half in range(2):
            v = conv(uv_ref, half, r0, wv_ref, bv_ref)
            x1 = conv(u1_ref, half, r0, w1_ref, b1_ref)
            x0 = conv(u0_ref, half, r0, w0_ref, b0_ref)
            x0_ref[half, pl.ds(r0, rc), :] = x0.astype(BF)
            zs.append(v * x1)
        z_ref[pl.ds(r0, rc), :] = _pack(zs[0], zs[1])
        return carry

    lax.fori_loop(0, n_chunks, chunk, 0)


def _hyena_pre(p, conv_w, conv_b):
    b, l, n = p.shape
    hp = b // 2
    tc = LANES
    rc = min(l, 512)
    p4 = p.reshape(2, hp, l, n)
    c = HY_WIDTH
    nb = c // tc
    base = P_HY // tc

    def u_spec(g):
        return pl.BlockSpec((2, None, l, tc), lambda pi, j, _g=g: (0, pi, 0, base + _g * nb + j))

    def w_spec(g):
        return pl.BlockSpec((3, tc), lambda pi, j, _g=g: (0, _g * nb + j))

    def b_spec(g):
        return pl.BlockSpec((1, tc), lambda pi, j, _g=g: (0, _g * nb + j))

    cb = conv_b.reshape(1, 3 * c)
    z, x0 = pl.pallas_call(
        functools.partial(_hypre_body, rc),
        out_shape=(jax.ShapeDtypeStruct((hp, nb, l, tc), jnp.int32), jax.ShapeDtypeStruct((2, hp, l, c), BF)),
        grid=(hp, nb),
        in_specs=[u_spec(0), u_spec(1), u_spec(2), w_spec(0), w_spec(1), w_spec(2), b_spec(0), b_spec(1), b_spec(2)],
        out_specs=(pl.BlockSpec((None, None, l, tc), lambda pi, j: (pi, j, 0, 0)),
                   pl.BlockSpec((2, None, l, tc), lambda pi, j: (0, pi, 0, j))),
        compiler_params=_cparams(2),
        name="hyena_pre",
    )(p4, p4, p4, conv_w, conv_w, conv_w, cb, cb, cb)
    return z, x0.reshape(b, l, c)


def _filter_body(l, tr, f_ref, w1_ref, b1_ref, fr_ref, w2_ref, b2_ref, w3_ref, dl_ref, k_ref, ss_ref):
    i = pl.program_id(0)
    feats = f_ref[...]
    freq = fr_ref[...]
    h = jnp.sin(freq * (jnp.dot(feats, w1_ref[...], preferred_element_type=F32, precision=HIGHEST) + b1_ref[...]))
    h = jnp.sin(freq * (jnp.dot(h, w2_ref[...], preferred_element_type=F32, precision=HIGHEST) + b2_ref[...]))
    h = jnp.dot(h, w3_ref[...], preferred_element_type=F32, precision=HIGHEST)
    t01 = feats[:, 0:1]
    window = jnp.exp(-t01 * dl_ref[...]) + HY_SHIFT
    row = i * tr + lax.broadcasted_iota(jnp.int32, h.shape, 0)
    kern = jnp.where(row == l, 0.0, h * window)
    k_ref[...] = kern

    @pl.when(i == 0)
    def _():
        ss_ref[...] = jnp.zeros_like(ss_ref)

    ss_ref[...] += jnp.sum(kern * kern, axis=0, keepdims=True)


def _hyena_filter(l, w1, b1, freq, w2, b2, w3):
    n = 2 * l
    c = HY_WIDTH
    hid = HY_FILTER_HIDDEN
    tr = min(l, 512)
    j = jnp.arange(n)
    lag = jnp.where(j < l, j, n - j).astype(F32)
    t01 = lag / max(l - 1, 1)
    bands = jnp.linspace(1e-4, HY_BANDS - 1, HY_BANDS, dtype=F32)
    ang = (2.0 * math.pi / l) * lag[:, None] * bands[None, :]
    feats = jnp.concatenate([t01[:, None], jnp.cos(ang), -jnp.sin(ang)], axis=-1)
    feats = jnp.pad(feats, ((0, 0), (0, hid - HY_EMB)))
    w1p = jnp.pad(w1, ((0, hid - HY_EMB), (0, 0)))
    deltas = jnp.abs(jnp.linspace(HY_MIN_DECAY, HY_MAX_DECAY, c, dtype=F32)).reshape(1, c)
    nt = l // tr
    return pl.pallas_call(
        functools.partial(_filter_body, l, tr),
        out_shape=(jax.ShapeDtypeStruct((n, c), F32), jax.ShapeDtypeStruct((1, c), F32)),
        grid=(n // tr,),
        in_specs=[pl.BlockSpec((tr, hid), lambda i: (i, 0)),
                  pl.BlockSpec((hid, hid), lambda i: (0, 0)),
                  pl.BlockSpec((1, hid), lambda i: (0, 0)),
                  pl.BlockSpec((1, hid), lambda i: (0, 0)),
                  pl.BlockSpec((hid, hid), lambda i: (0, 0)),
                  pl.BlockSpec((1, hid), lambda i: (0, 0)),
                  pl.BlockSpec((hid, c), lambda i: (0, i // nt)),
                  pl.BlockSpec((1, c), lambda i: (0, 0))],
        out_specs=(pl.BlockSpec((tr, c), lambda i: (i, 0)),
                   pl.BlockSpec((1, c), lambda i: (0, 0))),
        compiler_params=_cparams(1),
        name="hyena_filter",
    )(feats, w1p, b1.reshape(1, hid), freq.reshape(1, hid), w2, b2.reshape(1, hid), w3, deltas)


def _split(l):
    n = 2 * l
    n1 = n // FFT_INNER if n > 4 * FFT_INNER else 1
    return n1, n // n1


def _interleave_cols(a, b):
    r, k = a.shape
    return np.stack([a, b], axis=-1).reshape(r, 2 * k)


@functools.lru_cache(maxsize=None)
def _fft_tables(l):
    n1, n2 = _split(l)
    n = 2 * l
    f64 = np.float64
    t = {}
    n2in = n2 if n1 > 1 else l
    k2 = np.arange(n2, dtype=f64)[:, None]
    t2 = np.arange(n2in, dtype=f64)[None, :]
    ang2 = -2.0 * np.pi * k2 * t2 / n2
    f2re, f2im = np.cos(ang2), np.sin(ang2)
    t['cre2'] = _interleave_cols(f2re, f2re)
    t['cim2'] = _interleave_cols(f2im, f2im)
    k1 = np.arange(n1, dtype=f64)[:, None]
    angw = -2.0 * np.pi * k1 * np.arange(n2in, dtype=f64)[None, :] / n
    wre, wim = np.cos(angw), np.sin(angw)
    t['w4'] = np.stack([_interleave_cols(wre, -wim), _interleave_cols(-wim, -wre),
                        _interleave_cols(wim, wre), _interleave_cols(wre, -wim)], axis=1)
    n2out = n2 if n1 > 1 else l
    tt = np.arange(n2out, dtype=f64)[:, None]
    kk = np.arange(n2, dtype=f64)[None, :]
    angc = -2.0 * np.pi * tt * kk / n2
    cr, ci = np.cos(angc), np.sin(angc)
    t['gc'] = np.block([[cr, ci], [-ci, cr]])
    k2f = np.arange(n2, dtype=f64)[:, None]
    t2f = np.arange(n2, dtype=f64)[None, :]
    angf = -2.0 * np.pi * k2f * t2f / n2
    t['f2re'], t['f2im'] = np.cos(angf), np.sin(angf)
    angwf = -2.0 * np.pi * k1 * np.arange(n2, dtype=f64)[None, :] / n
    t['w2'] = np.stack([np.cos(angwf), np.sin(angwf)], axis=1)
    if n1 > 1:
        h = n1 // 2
        kk1 = np.arange(n1, dtype=f64)[:, None]
        ang1 = -2.0 * np.pi * kk1 * np.arange(h, dtype=f64)[None, :] / n1
        f1re, f1im = np.cos(ang1), np.sin(ang1)
        t['m1'] = np.concatenate([_interleave_cols(f1re, -f1im), _interleave_cols(f1im, f1re)], axis=0)
        d1re, d1im = f1re.T, f1im.T
        t['d1re2'] = _interleave_cols(d1re, d1re)
        t['d1im2'] = _interleave_cols(d1im, d1im)
        wre_t, wim_t = wre.T, wim.T
        t['v4'] = np.stack([_interleave_cols(wre_t, wim_t), _interleave_cols(-wim_t, wre_t),
                            _interleave_cols(-wim_t, wre_t), _interleave_cols(-wre_t, -wim_t)], axis=1)
        ang1f = -2.0 * np.pi * kk1 * np.arange(n1, dtype=f64)[None, :] / n1
        t['m1k'] = np.concatenate([np.cos(ang1f), np.sin(ang1f)], axis=0)
    return {k: np.asarray(v, np.float32) for k, v in t.items()}


def _dot3(a, b):
    ah = a.astype(BF)
    al = (a - ah.astype(F32)).astype(BF)
    bh = b.astype(BF)
    bl = (b - bh.astype(F32)).astype(BF)
    return jnp.dot(jnp.concatenate([ah, ah, al], axis=1), jnp.concatenate([bh, bl, bh], axis=0),
                   preferred_element_type=F32)


def _kfft_body(l, *refs):
    n1, n2 = _split(l)
    n = 2 * l
    if n1 > 1:
        k_ref, ss_ref, m1k_ref, f2re_ref, f2im_ref, w2_ref, o_ref, are_scr, aim_scr = refs
    else:
        k_ref, ss_ref, f2re_ref, f2im_ref, o_ref = refs
    scale = lax.rsqrt(ss_ref[...] + EPS) * (1.0 / n)

    if n1 > 1:
        pitch = are_scr.shape[0] // n1

        def stage1(t2, carry):
            xin = k_ref[pl.ds(t2, n1, stride=n2), :]
            o = _dot3(m1k_ref[...], xin)
            are_scr[pl.ds(t2, n1, stride=pitch), :] = o[:n1]
            aim_scr[pl.ds(t2, n1, stride=pitch), :] = o[n1:]
            return carry

        lax.fori_loop(0, n2, stage1, 0, unroll=KF_UNROLL)

        def slab(k1, carry):
            r0 = pl.multiple_of(k1 * pitch, 8)
            s = jnp.concatenate([are_scr[pl.ds(r0, n2), :], aim_scr[pl.ds(r0, n2), :]], axis=0)
            w2 = w2_ref[k1]
            wre, wim = w2[0:1], w2[1:2]
            hre = f2re_ref[...] * wre - f2im_ref[...] * wim
            him = f2re_ref[...] * wim + f2im_ref[...] * wre
            hblk = jnp.concatenate([jnp.concatenate([hre, -him], axis=1),
                                    jnp.concatenate([him, hre], axis=1)], axis=0)
            x = _dot3(hblk, s)
            o_ref[k1] = _pack(x[:n2] * scale, x[n2:] * scale)
            return carry

        lax.fori_loop(0, n1, slab, 0, unroll=KF_UNROLL)
    else:
        f = jnp.concatenate([f2re_ref[...], f2im_ref[...]], axis=0)
        x = _dot3(f, k_ref[...])
        o_ref[0] = _pack(x[:n2] * scale, x[n2:] * scale)


def _kernel_spectrum(l, kern, sumsq):
    n1, n2 = _split(l)
    n = 2 * l
    c = kern.shape[1]
    ct = 128
    tb = _fft_tables(l)
    names = (['m1k'] if n1 > 1 else []) + ['f2re', 'f2im'] + (['w2'] if n1 > 1 else [])
    consts = [jnp.asarray(tb[k]) for k in names]
    scratch = [pltpu.VMEM((n1 * (n2 + 8), ct), F32)] * 2 if n1 > 1 else []
    return pl.pallas_call(
        functools.partial(_kfft_body, l),
        out_shape=jax.ShapeDtypeStruct((n1, n2, c), jnp.int32),
        grid=(c // ct,),
        in_specs=[pl.BlockSpec((n, ct), lambda j: (0, j), pipeline_mode=pl.Buffered(1)),
                  pl.BlockSpec((1, ct), lambda j: (0, j))] + [_const_spec(a.shape, 1) for a in consts],
        out_specs=pl.BlockSpec((n1, n2, ct), lambda j: (0, 0, j)),
        scratch_shapes=scratch,
        compiler_params=_cparams(1),
        name="hyena_filter_fft",
    )(kern, sumsq, *consts)


def _fft_body(l, *refs):
    n1, n2 = _split(l)
    if n1 > 1:
        (z_ref, kf_ref, cre2_ref, cim2_ref, w4_ref, gc_ref, m1_ref, d1re2_ref, d1im2_ref, v4_ref,
         out_ref, a_scr) = refs
        h = n1 // 2
        ctn = z_ref.shape[0]
        a_pitch = a_scr.shape[1] // n1

        def load(ref, rows):
            return jnp.concatenate([ref[c, rows, :] for c in range(ctn)], axis=1)

        def store(ref, rows, val):
            for c in range(ctn):
                ref[c, rows, :] = val[:, c * LANES:(c + 1) * LANES]

        def stage1(t2, carry):
            xin = load(z_ref, pl.ds(t2, h, stride=n2))
            xb = pltpu.bitcast(xin, BF)
            o = jnp.dot(m1_ref[...], xb, preferred_element_type=F32)
            store(a_scr, pl.ds(t2, n1, stride=a_pitch), _pack(o[:n1], o[n1:]))
            return carry

        lax.fori_loop(0, n2, stage1, 0, unroll=FFT_UNROLL)

        def slab(k1, carry):
            r0 = pl.multiple_of(k1 * a_pitch, 8)
            s = pltpu.bitcast(load(a_scr, pl.ds(r0, n2)), BF)
            w4 = w4_ref[k1]
            cre2 = cre2_ref[...]
            cim2 = cim2_ref[...]
            top = cre2 * w4[0:1] + cim2 * w4[1:2]
            bot = cre2 * w4[2:3] + cim2 * w4[3:4]
            hblk = jnp.concatenate([top, bot], axis=0).astype(BF)
            x = jnp.dot(hblk, s, preferred_element_type=F32)
            kf = kf_ref[k1]
            kre, kim = _unpack_lo(kf), _unpack_hi(kf)
            xre, xim = x[:n2], x[n2:]
            y = jnp.concatenate([xre * kre - xim * kim, xre * kim + xim * kre], axis=0).astype(BF)
            b = jnp.dot(gc_ref[...], y, preferred_element_type=F32)
            store(a_scr, pl.ds(r0, n2), _pack(b[:n2], b[n2:]))
            return carry

        lax.fori_loop(0, n1, slab, 0, unroll=FFT_UNROLL)

        def stage3(t2, carry):
            bin_ = pltpu.bitcast(load(a_scr, pl.ds(t2, n1, stride=a_pitch)), BF)
            v4 = v4_ref[t2]
            d1re2 = d1re2_ref[...]
            d1im2 = d1im2_ref[...]
            top = d1re2 * v4[0:1] + d1im2 * v4[1:2]
            bot = d1re2 * v4[2:3] + d1im2 * v4[3:4]
            m3 = jnp.concatenate([top, bot], axis=0).astype(BF)
            o = jnp.dot(m3, bin_, preferred_element_type=F32)
            store(out_ref, pl.ds(t2, h, stride=n2), _pack(o[:h], o[h:]))
            return carry

        lax.fori_loop(0, n2, stage3, 0, unroll=FFT_UNROLL)
    else:
        z_ref, kf_ref, cre2_ref, cim2_ref, w4_ref, gc_ref, out_ref = refs
        ctn = z_ref.shape[0]
        s = pltpu.bitcast(jnp.concatenate([z_ref[c] for c in range(ctn)], axis=1), BF)
        w4 = w4_ref[0]
        top = cre2_ref[...] * w4[0:1] + cim2_ref[...] * w4[1:2]
        bot = cre2_ref[...] * w4[2:3] + cim2_ref[...] * w4[3:4]
        hblk = jnp.concatenate([top, bot], axis=0).astype(BF)
        x = jnp.dot(hblk, s, preferred_element_type=F32)
        kf = kf_ref[0]
        kre, kim = _unpack_lo(kf), _unpack_hi(kf)
        xre, xim = x[:n2], x[n2:]
        y = jnp.concatenate([xre * kre - xim * kim, xre * kim + xim * kre], axis=0).astype(BF)
        b = jnp.dot(gc_ref[...], y, preferred_element_type=F32)
        packed = _pack(b[:l], b[l:])
        for c in range(ctn):
            out_ref[c] = packed[:, c * LANES:(c + 1) * LANES]


def _fft_conv(zp, kf):
    hp, nct, l, _ = zp.shape
    n1, n2 = _split(l)
    ct = FFT_CT
    ctn = ct // LANES
    tb = _fft_tables(l)
    names = ['cre2', 'cim2', 'w4', 'gc'] + (['m1', 'd1re2', 'd1im2', 'v4'] if n1 > 1 else [])
    consts = [jnp.asarray(tb[k]).astype(BF) if k in ('gc', 'm1') else jnp.asarray(tb[k]) for k in names]
    scratch = [pltpu.VMEM((ctn, n1 * (n2 + 8), LANES), jnp.int32)] if n1 > 1 else []
    one = dict(pipeline_mode=pl.Buffered(1))
    return pl.pallas_call(
        functools.partial(_fft_body, l),
        out_shape=jax.ShapeDtypeStruct((hp, nct, l, LANES), jnp.int32),
        grid=(nct // ctn, hp),
        in_specs=[pl.BlockSpec((None, ctn, l, LANES), lambda j, p: (p, j, 0, 0), **one),
                  pl.BlockSpec((n1, n2, ct), lambda j, p: (0, 0, j), **one)] + [_const_spec(a.shape, 2) for a in consts],
        out_specs=pl.BlockSpec((None, ctn, l, LANES), lambda j, p: (p, j, 0, 0), **one),
        scratch_shapes=scratch,
        compiler_params=_cparams(2),
        name="hyena_fft_conv",
    )(zp, kf, *consts)


def _pool_body(rc, u_ref, w_ref, sc_ref, o_ref):
    l = u_ref.shape[0]
    g = pl.program_id(1)
    win = jnp.left_shift(2, g)
    before = win // 2
    after = win - 1 - before
    n_chunks = l // rc
    halo = 64

    off = (lax.broadcasted_iota(jnp.int32, (rc, rc + 2 * halo), 1) - halo
           - lax.broadcasted_iota(jnp.int32, (rc, rc + 2 * halo), 0))
    band = jnp.where((off >= -before) & (off <= after), 1.0, 0.0).astype(BF)

    def chunk(i, carry):
        r0 = pl.multiple_of(i * rc, rc)
        p0 = pl.multiple_of(jnp.maximum(r0 - halo, 0), halo)
        n0 = pl.multiple_of(jnp.minimum(r0 + rc, l - halo), halo)
        cur = u_ref[pl.ds(r0, rc), :]
        prev = jnp.where(r0 > 0, u_ref[pl.ds(p0, halo), :], jnp.zeros((halo, POOL_GROUP), BF))
        nxt = jnp.where(r0 + rc < l, u_ref[pl.ds(n0, halo), :], jnp.zeros((halo, POOL_GROUP), BF))
        ext = jnp.concatenate([prev, cur, nxt], axis=0)
        sums = jnp.dot(band, ext, preferred_element_type=F32)
        tr = r0 + lax.broadcasted_iota(jnp.int32, (rc, POOL_GROUP), 0)
        cnt = jnp.minimum(tr + after + 1, l) - jnp.maximum(tr - before, 0)
        m = sums / cnt.astype(F32) - cur.astype(F32)
        y = jnp.dot(m.astype(BF), w_ref[...], preferred_element_type=F32) * sc_ref[...]
        o_ref[pl.ds(r0, rc), :] = y.astype(BF)
        return carry

    lax.fori_loop(0, n_chunks, chunk, 0, unroll=2)


def _pool(p, pool_w_bf, pool_scale):
    b, l, _ = p.shape
    rc = min(l, 512)
    ng = len(POOL_WINDOWS)
    base = P_POOL // POOL_GROUP
    return pl.pallas_call(
        functools.partial(_pool_body, rc),
        out_shape=jax.ShapeDtypeStruct((b, l, D_MODEL), BF),
        grid=(b, ng),
        in_specs=[pl.BlockSpec((None, l, POOL_GROUP), lambda bi, g: (bi, 0, base + g)),
                  pl.BlockSpec((None, POOL_GROUP, POOL_GROUP), lambda bi, g: (g, 0, 0)),
                  pl.BlockSpec((1, POOL_GROUP), lambda bi, g: (0, g))],
        out_specs=pl.BlockSpec((None, l, POOL_GROUP), lambda bi, g: (bi, 0, g)),
        compiler_params=_cparams(2),
        name="pool_mixer",
    )(p, pool_w_bf, pool_scale.reshape(1, D_MODEL))


def _mix_body(hp, x_ref, gate_ref, y_ref, z_ref, x0_ref, hd_ref, at_ref, po_ref, wb_ref, wo_ref,
              mod_ref, g_ref, b_ref, o_ref):
    hi = pl.program_id(0) >= hp
    nct = y_ref.shape[0]
    yw = jnp.concatenate([y_ref[c] for c in range(nct)], axis=1)
    zw = jnp.concatenate([z_ref[c] for c in range(nct)], axis=1)
    y = jnp.where(hi, _unpack_hi(yw), _unpack_lo(yw))
    z = jnp.where(hi, _unpack_hi(zw), _unpack_lo(zw))
    hy = ((y + z * hd_ref[...]) * x0_ref[...].astype(F32)).astype(BF)
    d = D_MODEL
    merged = jax.nn.sigmoid(gate_ref[:, 0:d].astype(F32)) * jnp.dot(hy, wb_ref[0], preferred_element_type=F32)
    merged += jax.nn.sigmoid(gate_ref[:, d:2 * d].astype(F32)) * jnp.dot(at_ref[...], wb_ref[1],
                                                                         preferred_element_type=F32)
    merged += jax.nn.sigmoid(gate_ref[:, 2 * d:].astype(F32)) * jnp.dot(po_ref[...], wb_ref[2],
                                                                        preferred_element_type=F32)
    out = jnp.dot(merged.astype(BF), wo_ref[...], preferred_element_type=F32)
    r = DN_ALPHA * x_ref[...] + mod_ref[2:3, :] * out
    o_ref[...] = _layer_norm(r, g_ref[...], b_ref[...])


def _mix(x, p, y_pair, z_pair, x0c, hy_d, attn, pool, wb_bf, wo_bf, mod, ln_g, ln_b):
    b, l, d = x.shape
    hp = b // 2
    tm = min(l, 512)
    row = lambda bi, i: (bi, i, 0)
    pair = lambda bi, i: (bi % hp, 0, i, 0)
    nct = d // LANES
    return pl.pallas_call(
        functools.partial(_mix_body, hp),
        out_shape=jax.ShapeDtypeStruct((b, l, d), F32),
        grid=(b, l // tm),
        in_specs=[pl.BlockSpec((None, tm, d), row),
                  pl.BlockSpec((None, tm, 3 * d), lambda bi, i: (bi, i, P_GATE // (3 * d))),
                  pl.BlockSpec((None, nct, tm, LANES), pair),
                  pl.BlockSpec((None, nct, tm, LANES), pair),
                  pl.BlockSpec((None, tm, d), row),
                  pl.BlockSpec((1, d), lambda bi, i: (0, 0)),
                  pl.BlockSpec((None, tm, d), row),
                  pl.BlockSpec((None, tm, d), row),
                  pl.BlockSpec((3, d, d), lambda bi, i: (0, 0, 0), pipeline_mode=pl.Buffered(1)),
                  pl.BlockSpec((d, d), lambda bi, i: (0, 0), pipeline_mode=pl.Buffered(1)),
                  pl.BlockSpec((None, 6, d), lambda bi, i: (bi, 0, 0)),
                  pl.BlockSpec((1, d), lambda bi, i: (0, 0)),
                  pl.BlockSpec((1, d), lambda bi, i: (0, 0))],
        out_specs=pl.BlockSpec((None, tm, d), row),
        compiler_params=_cparams(2),
        name="branch_mix",
    )(x, p, y_pair, z_pair, x0c, hy_d.reshape(1, d), attn, pool, wb_bf, wo_bf, mod,
      ln_g.reshape(1, d), ln_b.reshape(1, d))


def _ffn_body(n_chunks, x_ref, mod_ref, w1_ref, w3_ref, w2_ref, g_ref, b_ref, o_ref):
    x = x_ref[...]
    h = (x * (1.0 + mod_ref[4:5, :]) + mod_ref[3:4, :]).astype(BF)
    fc = D_FF // n_chunks
    acc = jnp.zeros(x.shape, F32)
    for c in range(n_chunks):
        sl = slice(c * fc, (c + 1) * fc)
        a = jnp.dot(h, w1_ref[:, sl], preferred_element_type=F32)
        bb = jnp.dot(h, w3_ref[:, sl], preferred_element_type=F32)
        gg = (a * jax.nn.sigmoid(a) * bb).astype(BF)
        acc += jnp.dot(gg, w2_ref[sl, :], preferred_element_type=F32)
    r = DN_ALPHA * x + mod_ref[5:6, :] * acc
    o_ref[...] = _layer_norm(r, g_ref[...], b_ref[...])


def _ffn(x, mod, w1_bf, w3_bf, w2_bf, ln_g, ln_b):
    b, l, d = x.shape
    tm = min(l, 512)
    row = lambda bi, i: (bi, i, 0)
    return pl.pallas_call(
        functools.partial(_ffn_body, 2),
        out_shape=jax.ShapeDtypeStruct((b, l, d), F32),
        grid=(b, l // tm),
        in_specs=[pl.BlockSpec((None, tm, d), row),
                  pl.BlockSpec((None, 6, d), lambda bi, i: (bi, 0, 0)),
                  pl.BlockSpec((d, D_FF), lambda bi, i: (0, 0), pipeline_mode=pl.Buffered(1)),
                  pl.BlockSpec((d, D_FF), lambda bi, i: (0, 0), pipeline_mode=pl.Buffered(1)),
                  pl.BlockSpec((D_FF, d), lambda bi, i: (0, 0), pipeline_mode=pl.Buffered(1)),
                  pl.BlockSpec((1, d), lambda bi, i: (0, 0)),
                  pl.BlockSpec((1, d), lambda bi, i: (0, 0))],
        out_specs=pl.BlockSpec((None, tm, d), row),
        compiler_params=_cparams(2),
        name="swiglu_ffn",
    )(x, mod, w1_bf, w3_bf, w2_bf, ln_g.reshape(1, d), ln_b.reshape(1, d))


def _permute_w_in(w):
    return jnp.concatenate([w[:, C_GATE:], w[:, C_HY:C_POOL], w[:, C_Q:C_K], w[:, C_POOL:C_GATE],
                            w[:, C_K:C_V], w[:, C_V:C_HY]], axis=1).astype(BF)


def _v_cols(p):
    return p[..., P_V:P_V + KV_W]


def _k_cols(p):
    return p[..., P_K:P_K + KV_W]


def _keys_transposed(k):
    b, lk, _ = k.shape
    return k.reshape(b, lk, N_KV_HEADS, HEAD_DIM).transpose(0, 2, 3, 1)


def _stream_block(x, p, k_all, v_all, kf, mod, lw):
    attn = _attention(p, _keys_transposed(k_all), v_all)
    z_pair, x0c = _hyena_pre(p, lw['hy_conv_w'], lw['hy_conv_b'])
    y_pair = _fft_conv(z_pair, kf)
    pool = _pool(p, lw['pool_w'], lw['pool_scale'])
    x = _mix(x, p, y_pair, z_pair, x0c, lw['hy_d'], attn, pool, lw['w_branch'], lw['w_out'], mod,
             lw['ln1_g'], lw['ln1_b'])
    return _ffn(x, mod, lw['ffn_w1'], lw['ffn_w3'], lw['ffn_w2'], lw['ln2_g'], lw['ln2_b'])


def kernel(x, c, ctx, c_ctx, w_ada, b_ada, w_in, q_norm_g, k_norm_g, hy_conv_w, hy_conv_b, hf_w1, hf_b1, hf_freq,
           hf_w2, hf_b2, hf_w3, hy_d, pool_w, pool_scale, w_branch, w_out, ln1_g, ln1_b, ln2_g, ln2_b,
           ffn_w1, ffn_w3, ffn_w2):
    b, l, d = x.shape
    lc = ctx.shape[1]
    depth = w_ada.shape[0]
    assert b % 2 == 0 and l % GRID_W == 0

    rows = 16
    c_all = jnp.zeros((rows, d), F32).at[:b].set(c).at[b].set(c_ctx)
    cos_l, sin_l = _rope_tables(l, True)
    cos_c, sin_c = _rope_tables(lc, False)

    xl, xc = x, ctx
    for li in range(depth):
        last = li == depth - 1
        lw = dict(hy_conv_w=hy_conv_w[li], hy_conv_b=hy_conv_b[li], hy_d=hy_d[li],
                  pool_w=pool_w[li].astype(BF), pool_scale=pool_scale[li],
                  w_branch=w_branch[li].astype(BF), w_out=w_out[li].astype(BF),
                  ln1_g=ln1_g[li], ln1_b=ln1_b[li], ln2_g=ln2_g[li], ln2_b=ln2_b[li],
                  ffn_w1=ffn_w1[li].astype(BF), ffn_w3=ffn_w3[li].astype(BF), ffn_w2=ffn_w2[li].astype(BF))
        w_in_p = _permute_w_in(w_in[li])

        mod = _ada(c_all, w_ada[li], b_ada[li]).reshape(rows, 6, d)
        mod_l = mod[:b]
        mod_c = jnp.broadcast_to(mod[b:b + 1], (b, 6, d))

        filt = (hf_w1[li], hf_b1[li], hf_freq[li], hf_w2[li], hf_b2[li], hf_w3[li])
        kf_l = _kernel_spectrum(l, *_hyena_filter(l, *filt))

        p_l = _in_proj(xl, mod_l, w_in_p, cos_l, sin_l, q_norm_g[li], k_norm_g[li])
        p_c = _in_proj(xc, mod_c, w_in_p, cos_c, sin_c, q_norm_g[li], k_norm_g[li])
        k_c, v_c = _k_cols(p_c), _v_cols(p_c)
        k_all = jnp.concatenate([k_c, _k_cols(p_l)], axis=1)
        v_all = jnp.concatenate([v_c, _v_cols(p_l)], axis=1)

        xl = _stream_block(xl, p_l, k_all, v_all, kf_l, mod_l, lw)
        if not last:
            kf_c = _kernel_spectrum(lc, *_hyena_filter(lc, *filt))
            xc = _stream_block(xc, p_c, k_c, v_c, kf_c, mod_c, lw)
    return xl
```

```python
import functools
import math

import numpy as np
import jax
import jax.numpy as jnp
from jax import lax
from jax.experimental import pallas as pl
from jax.experimental.pallas import tpu as pltpu

F32 = jnp.float32
BF = jnp.bfloat16
HIGHEST = lax.Precision.HIGHEST

D_MODEL = 1024
GRID_W = 64
N_HEADS = 8
N_KV_HEADS = 2
HEAD_DIM = 128
GQA_GROUP = N_HEADS // N_KV_HEADS
ROPE_THETA = 10000.0
Q_W = N_HEADS * HEAD_DIM
KV_W = N_KV_HEADS * HEAD_DIM
HY_WIDTH = D_MODEL
HY_EMB = 33
HY_BANDS = (HY_EMB - 1) // 2
HY_FILTER_HIDDEN = 64
HY_TARGET = 1e-2
HY_MAX_DECAY = math.log(HY_TARGET) / 0.3
HY_MIN_DECAY = math.log(HY_TARGET) / 1.5
HY_SHIFT = 0.05
POOL_WINDOWS = (2, 4, 8, 16)
POOL_GROUP = D_MODEL // len(POOL_WINDOWS)
D_FF = 2816
DEPTH = 2
DN_ALPHA = (2 * DEPTH) ** 0.25
EPS = 1e-6

C_Q = 0
C_K = C_Q + Q_W
C_V = C_K + KV_W
C_HY = C_V + KV_W
C_POOL = C_HY + 3 * HY_WIDTH
C_GATE = C_POOL + D_MODEL
IN_WIDTH = C_GATE + 3 * D_MODEL

P_GATE = 0
P_HY = 3 * D_MODEL
P_Q = P_HY + 3 * HY_WIDTH
P_POOL = P_Q + Q_W
P_K = P_POOL + D_MODEL
P_V = P_K + KV_W

Q_SCALE = HEAD_DIM ** -0.5 * math.log2(math.e)
F8 = jnp.float8_e4m3fn
QK_SHIFT = 4.0

ATT_TQ = 256
ATT_TK = 768
ATT_FLAGS = None
ATT_ROW_BLOCKS = 1
ATT_UNROLL = True
FFT_CT = 256
FFT_UNROLL = 8
KF_UNROLL = 4
ROW_SPLIT = 2

VMEM_LIMIT = 60 * 1024 * 1024
FFT_INNER = 128
LANES = 128


def _cparams(n_axes):
    return pltpu.CompilerParams(dimension_semantics=("arbitrary",) * n_axes, vmem_limit_bytes=VMEM_LIMIT)


def _const_spec(shape, n_grid):
    nd = len(shape)
    return pl.BlockSpec(shape, lambda *g, _nd=nd: (0,) * _nd)


def _layer_norm(r, g, b):
    mu = jnp.mean(r, axis=-1, keepdims=True)
    d = r - mu
    var = jnp.mean(d * d, axis=-1, keepdims=True)
    return d * lax.rsqrt(var + EPS) * g + b


def _ada_body(c_ref, w_ref, b_ref, o_ref):
    c = c_ref[...]
    s = c * jax.nn.sigmoid(c)
    o_ref[...] = jnp.dot(s, w_ref[...], preferred_element_type=F32, precision=HIGHEST) + b_ref[...]


def _ada(c_all, w_ada, b_ada):
    rows, d = c_all.shape
    n = w_ada.shape[1]
    tn = 512
    return pl.pallas_call(
        _ada_body,
        out_shape=jax.ShapeDtypeStruct((rows, n), F32),
        grid=(n // tn,),
        in_specs=[pl.BlockSpec((rows, d), lambda j: (0, 0)),
                  pl.BlockSpec((d, tn), lambda j: (0, j)),
                  pl.BlockSpec((1, tn), lambda j: (0, j))],
        out_specs=pl.BlockSpec((rows, tn), lambda j: (0, j)),
        compiler_params=_cparams(1),
        name="ada_mod",
    )(c_all, w_ada, b_ada.reshape(1, n))


def _norm_rope(y, g, cos_t, sin_t, scale):
    ms = jnp.mean(y * y, axis=-1, keepdims=True)
    y = y * lax.rsqrt(ms + EPS) * g
    lane = lax.broadcasted_iota(jnp.int32, y.shape, 1)
    up = pltpu.roll(y, HEAD_DIM - 32, 1)
    dn = pltpu.roll(y, 32, 1)
    partner = jnp.where((lane % 64) < 32, up, dn)
    out = y * cos_t + partner * sin_t
    if scale != 1.0:
        out = out * scale
    return out


def _inproj_body(tn, x_ref, mod_ref, w_ref, cos_ref, sin_ref, gq_ref, gk_ref, o_ref):
    h = (x_ref[...] * (1.0 + mod_ref[1:2, :]) + mod_ref[0:1, :]).astype(BF)
    chunks = list(range(w_ref.shape[1] // tn))
    with_epilogue = [j for j in chunks if j * tn < P_Q + Q_W and (j + 1) * tn > P_Q or j * tn < P_K + KV_W and (j + 1) * tn > P_K]
    for j in with_epilogue + [j for j in chunks if j not in with_epilogue]:
        c0 = j * tn
        r = jnp.dot(h, w_ref[:, c0:c0 + tn], preferred_element_type=F32)
        heads = []
        for hh in range(tn // HEAD_DIM):
            col = c0 + hh * HEAD_DIM
            y = r[:, hh * HEAD_DIM:(hh + 1) * HEAD_DIM]
            if P_Q <= col < P_Q + Q_W:
                y = _norm_rope(y, gq_ref[...], cos_ref[...], sin_ref[...], Q_SCALE * QK_SHIFT)
            elif P_K <= col < P_K + KV_W:
                y = _norm_rope(y, gk_ref[...], cos_ref[...], sin_ref[...], 1.0 / QK_SHIFT)
            heads.append(y)
        o_ref[:, c0:c0 + tn] = jnp.concatenate(heads, axis=1).astype(BF)


def _in_proj(x, mod, w_in_bf, cos_t, sin_t, gq, gk):
    b, l, d = x.shape
    n = w_in_bf.shape[1]
    tm = min(l, 512)
    tn = 512
    vec = pl.BlockSpec((1, HEAD_DIM), lambda bi, i: (0, 0))
    tab = pl.BlockSpec((tm, HEAD_DIM), lambda bi, i: (i, 0))
    return pl.pallas_call(
        functools.partial(_inproj_body, tn),
        out_shape=jax.ShapeDtypeStruct((b, l, n), BF),
        grid=(b, l // tm),
        in_specs=[pl.BlockSpec((None, tm, d), lambda bi, i: (bi, i, 0)),
                  pl.BlockSpec((None, 6, d), lambda bi, i: (bi, 0, 0)),
                  pl.BlockSpec((d, n), lambda bi, i: (0, 0), pipeline_mode=pl.Buffered(1)),
                  tab, tab, vec, vec],
        out_specs=pl.BlockSpec((None, tm, n), lambda bi, i: (bi, i, 0)),
        compiler_params=_cparams(2),
        name="in_proj",
    )(x, mod, w_in_bf, cos_t, sin_t, gq.reshape(1, HEAD_DIM), gk.reshape(1, HEAD_DIM))


def _rope_tables(l, with_positions):
    quarter = HEAD_DIM // 4
    if not with_positions:
        return jnp.ones((l, HEAD_DIM), F32), jnp.zeros((l, HEAD_DIM), F32)
    t = jnp.arange(l)
    rows = (t // GRID_W).astype(F32)
    cols = (t % GRID_W).astype(F32)
    inv = jnp.power(ROPE_THETA, -jnp.arange(quarter, dtype=F32) / quarter)
    ar = rows[:, None] * inv[None, :]
    ac = cols[:, None] * inv[None, :]
    cos_t = jnp.concatenate([jnp.cos(ar), jnp.cos(ar), jnp.cos(ac), jnp.cos(ac)], axis=-1)
    sin_t = jnp.concatenate([-jnp.sin(ar), jnp.sin(ar), -jnp.sin(ac), jnp.sin(ac)], axis=-1)
    return cos_t, sin_t


def _flash_body(tk, q_ref, kt_ref, v_ref, o_ref, qs_scr, sa_scr, sb_scr, mpa_scr, mpb_scr, m_scr, accl_scr):
    tq = q_ref.shape[0]
    nk = kt_ref.shape[1] // tk
    for h in range(GQA_GROUP):
        qs_scr[h * tq:(h + 1) * tq, :] = q_ref[:, h * HEAD_DIM:(h + 1) * HEAD_DIM].astype(F8)
    m_scr[...] = jnp.full(m_scr.shape, -jnp.inf, F32)
    accl_scr[...] = jnp.zeros(accl_scr.shape, F32)

    def scores(i, s_scr, mp_scr):
        c0 = pl.multiple_of(i * tk, tk)
        s = jnp.dot(qs_scr[...], kt_ref[:, pl.ds(c0, tk)].astype(F8), preferred_element_type=F32)
        s_scr[...] = s
        mp = s[:, :HEAD_DIM]
        for t in range(1, tk // HEAD_DIM):
            mp = jnp.maximum(mp, s[:, t * HEAD_DIM:(t + 1) * HEAD_DIM])
        mp_scr[...] = mp

    def consume(i, s_scr, mp_scr):
        r0 = pl.multiple_of(i * tk, tk)
        v = v_ref[pl.ds(r0, tk), :]
        vext = jnp.concatenate([v, jnp.ones_like(v)], axis=1)
        rb = m_scr.shape[0] // ATT_ROW_BLOCKS
        for blk in range(ATT_ROW_BLOCKS):
            rows = slice(blk * rb, (blk + 1) * rb)
            m_prev = m_scr[rows, :]
            m_new = jnp.maximum(m_prev, jnp.max(mp_scr[rows, :], axis=-1, keepdims=True))
            alpha = jnp.exp2(m_prev - m_new)
            p = jnp.concatenate([jnp.exp2(s_scr[rows, t * HEAD_DIM:(t + 1) * HEAD_DIM] - m_new)
                                 for t in range(tk // HEAD_DIM)], axis=1).astype(BF)
            upd = jnp.dot(p, vext, preferred_element_type=F32)
            accl_scr[rows, :] = jnp.concatenate([alpha, alpha], axis=1) * accl_scr[rows, :] + upd
            m_scr[rows, :] = m_new

    scores(0, sa_scr, mpa_scr)

    def pair(j, carry):
        scores(2 * j + 1, sb_scr, mpb_scr)
        consume(2 * j, sa_scr, mpa_scr)
        scores(2 * j + 2, sa_scr, mpa_scr)
        consume(2 * j + 1, sb_scr, mpb_scr)
        return carry

    lax.fori_loop(0, (nk - 1) // 2, pair, 0, unroll=ATT_UNROLL)
    if nk % 2 == 1:
        consume(nk - 1, sa_scr, mpa_scr)
    else:
        scores(nk - 1, sb_scr, mpb_scr)
        consume(nk - 2, sa_scr, mpa_scr)
        consume(nk - 1, sb_scr, mpb_scr)
    for h in range(GQA_GROUP):
        rows = slice(h * tq, (h + 1) * tq)
        o_ref[:, h * HEAD_DIM:(h + 1) * HEAD_DIM] = (accl_scr[rows, :HEAD_DIM] / accl_scr[rows, HEAD_DIM:]).astype(BF)


def _attention(p, kt, v):
    b, lq, _ = p.shape
    lk = kt.shape[3]
    tq = min(lq, ATT_TQ)
    tk = ATT_TK if lk % ATT_TK == 0 else 256
    gw = GQA_GROUP * HEAD_DIM
    m = GQA_GROUP * tq
    return pl.pallas_call(
        functools.partial(_flash_body, tk),
        out_shape=jax.ShapeDtypeStruct((b, lq, Q_W), BF),
        grid=(b, N_KV_HEADS, lq // tq),
        in_specs=[pl.BlockSpec((None, tq, gw), lambda bi, g, i: (bi, i, P_Q // gw + g)),
                  pl.BlockSpec((None, None, HEAD_DIM, lk), lambda bi, g, i: (bi, g, 0, 0)),
                  pl.BlockSpec((None, lk, HEAD_DIM), lambda bi, g, i: (bi, 0, g))],
        out_specs=pl.BlockSpec((None, tq, gw), lambda bi, g, i: (bi, i, g)),
        scratch_shapes=[pltpu.VMEM((m, HEAD_DIM), F8),
                        pltpu.VMEM((m, tk), F32),
                        pltpu.VMEM((m, tk), F32),
                        pltpu.VMEM((m, HEAD_DIM), F32),
                        pltpu.VMEM((m, HEAD_DIM), F32),
                        pltpu.VMEM((m, HEAD_DIM), F32),
                        pltpu.VMEM((m, 2 * HEAD_DIM), F32)],
        compiler_params=pltpu.CompilerParams(dimension_semantics=("arbitrary",) * 3,
                                             vmem_limit_bytes=VMEM_LIMIT, flags=ATT_FLAGS),
        name="gqa_attention",
    )(p, kt, v)


def _pack(lo, hi):
    ul = lax.bitcast_convert_type(lo.astype(BF).astype(F32), jnp.uint32)
    uh = lax.bitcast_convert_type(hi.astype(BF).astype(F32), jnp.uint32)
    return lax.bitcast_convert_type(uh | (ul >> 16), jnp.int32)


def _unpack_lo(x):
    return lax.bitcast_convert_type(lax.bitcast_convert_type(x, jnp.uint32) << 16, F32)


def _unpack_hi(x):
    return lax.bitcast_convert_type(lax.bitcast_convert_type(x, jnp.uint32) & jnp.uint32(0xFFFF0000), F32)


def _hypre_body(rc, uv_ref, u0_ref, u1_ref, wv_ref, w0_ref, w1_ref, bv_ref, b0_ref, b1_ref, z_ref, x0_ref):
    l = uv_ref.shape[1]
    n_chunks = l // rc

    def conv(u_ref, half, r0, w_ref, b_ref):
        cur = u_ref[half, pl.ds(r0, rc), :].astype(F32)
        p0 = jnp.maximum(r0 - 16, 0)
        n0 = jnp.minimum(r0 + rc, l - 16)
        prev_row = u_ref[half, pl.ds(pl.multiple_of(p0, 16), 16), :].astype(F32)[15:16]
        next_row = u_ref[half, pl.ds(pl.multiple_of(n0, 16), 16), :].astype(F32)[0:1]
        prev_row = jnp.where(r0 > 0, prev_row, 0.0)
        next_row = jnp.where(r0 + rc < l, next_row, 0.0)
        row = lax.broadcasted_iota(jnp.int32, cur.shape, 0)
        x_prev = jnp.where(row == 0, prev_row, pltpu.roll(cur, 1, 0))
        x_next = jnp.where(row == rc - 1, next_row, pltpu.roll(cur, rc - 1, 0))
        return b_ref[...] + x_prev * w_ref[0:1, :] + cur * w_ref[1:2, :] + x_next * w_ref[2:3, :]

    def chunk(i, carry):
        r0 = pl.multiple_of(i * rc, rc)
        zs = []
        for half in range(2):
            v = conv(uv_ref, half, r0, wv_ref, bv_ref)
            x1 = conv(u1_ref, half, r0, w1_ref, b1_ref)
            x0 = conv(u0_ref, half, r0, w0_ref, b0_ref)
            x0_ref[half, pl.ds(r0, rc), :] = x0.astype(BF)
            zs.append(v * x1)
        z_ref[pl.ds(r0, rc), :] = _pack(zs[0], zs[1])
        return carry

    lax.fori_loop(0, n_chunks, chunk, 0)


def _hyena_pre(p, conv_w, conv_b):
    b, l, n = p.shape
    hp = b // 2
    tc = LANES
    rc = min(l, 512)
    p4 = p.reshape(2, hp, l, n)
    c = HY_WIDTH
    nb = c // tc
    base = P_HY // tc

    def u_spec(g):
        return pl.BlockSpec((2, None, l, tc), lambda pi, j, _g=g: (0, pi, 0, base + _g * nb + j))

    def w_spec(g):
        return pl.BlockSpec((3, tc), lambda pi, j, _g=g: (0, _g * nb + j))

    def b_spec(g):
        return pl.BlockSpec((1, tc), lambda pi, j, _g=g: (0, _g * nb + j))

    cb = conv_b.reshape(1, 3 * c)
    z, x0 = pl.pallas_call(
        functools.partial(_hypre_body, rc),
        out_shape=(jax.ShapeDtypeStruct((hp, nb, l, tc), jnp.int32), jax.ShapeDtypeStruct((2, hp, l, c), BF)),
        grid=(hp, nb),
        in_specs=[u_spec(0), u_spec(1), u_spec(2), w_spec(0), w_spec(1), w_spec(2), b_spec(0), b_spec(1), b_spec(2)],
        out_specs=(pl.BlockSpec((None, None, l, tc), lambda pi, j: (pi, j, 0, 0)),
                   pl.BlockSpec((2, None, l, tc), lambda pi, j: (0, pi, 0, j))),
        compiler_params=_cparams(2),
        name="hyena_pre",
    )(p4, p4, p4, conv_w, conv_w, conv_w, cb, cb, cb)
    return z, x0.reshape(b, l, c)


def _filter_body(l, tr, f_ref, w1_ref, b1_ref, fr_ref, w2_ref, b2_ref, w3_ref, dl_ref, k_ref, ss_ref):
    i = pl.program_id(0)
    feats = f_ref[...]
    freq = fr_ref[...]
    h = jnp.sin(freq * (jnp.dot(feats, w1_ref[...], preferred_element_type=F32, precision=HIGHEST) + b1_ref[...]))
    h = jnp.sin(freq * (jnp.dot(h, w2_ref[...], preferred_element_type=F32, precision=HIGHEST) + b2_ref[...]))
    h = jnp.dot(h, w3_ref[...], preferred_element_type=F32, precision=HIGHEST)
    t01 = feats[:, 0:1]
    window = jnp.exp(-t01 * dl_ref[...]) + HY_SHIFT
    row = i * tr + lax.broadcasted_iota(jnp.int32, h.shape, 0)
    kern = jnp.where(row == l, 0.0, h * window)
    k_ref[...] = kern

    @pl.when(i == 0)
    def _():
        ss_ref[...] = jnp.zeros_like(ss_ref)

    ss_ref[...] += jnp.sum(kern * kern, axis=0, keepdims=True)


def _hyena_filter(l, w1, b1, freq, w2, b2, w3):
    n = 2 * l
    c = HY_WIDTH
    hid = HY_FILTER_HIDDEN
    tr = min(l, 512)
    j = jnp.arange(n)
    lag = jnp.where(j < l, j, n - j).astype(F32)
    t01 = lag / max(l - 1, 1)
    bands = jnp.linspace(1e-4, HY_BANDS - 1, HY_BANDS, dtype=F32)
    ang = (2.0 * math.pi / l) * lag[:, None] * bands[None, :]
    feats = jnp.concatenate([t01[:, None], jnp.cos(ang), -jnp.sin(ang)], axis=-1)
    feats = jnp.pad(feats, ((0, 0), (0, hid - HY_EMB)))
    w1p = jnp.pad(w1, ((0, hid - HY_EMB), (0, 0)))
    deltas = jnp.abs(jnp.linspace(HY_MIN_DECAY, HY_MAX_DECAY, c, dtype=F32)).reshape(1, c)
    nt = l // tr
    return pl.pallas_call(
        functools.partial(_filter_body, l, tr),
        out_shape=(jax.ShapeDtypeStruct((n, c), F32), jax.ShapeDtypeStruct((1, c), F32)),
        grid=(n // tr,),
        in_specs=[pl.BlockSpec((tr, hid), lambda i: (i, 0)),
                  pl.BlockSpec((hid, hid), lambda i: (0, 0)),
                  pl.BlockSpec((1, hid), lambda i: (0, 0)),
                  pl.BlockSpec((1, hid), lambda i: (0, 0)),
                  pl.BlockSpec((hid, hid), lambda i: (0, 0)),
                  pl.BlockSpec((1, hid), lambda i: (0, 0)),
                  pl.BlockSpec((hid, c), lambda i: (0, i // nt)),
                  pl.BlockSpec((1, c), lambda i: (0, 0))],
        out_specs=(pl.BlockSpec((tr, c), lambda i: (i, 0)),
                   pl.BlockSpec((1, c), lambda i: (0, 0))),
        compiler_params=_cparams(1),
        name="hyena_filter",
    )(feats, w1p, b1.reshape(1, hid), freq.reshape(1, hid), w2, b2.reshape(1, hid), w3, deltas)


def _split(l):
    n = 2 * l
    n1 = n // FFT_INNER if n > 4 * FFT_INNER else 1
    return n1, n // n1


def _interleave_cols(a, b):
    r, k = a.shape
    return np.stack([a, b], axis=-1).reshape(r, 2 * k)


@functools.lru_cache(maxsize=None)
def _fft_tables(l):
    n1, n2 = _split(l)
    n = 2 * l
    f64 = np.float64
    t = {}
    n2in = n2 if n1 > 1 else l
    k2 = np.arange(n2, dtype=f64)[:, None]
    t2 = np.arange(n2in, dtype=f64)[None, :]
    ang2 = -2.0 * np.pi * k2 * t2 / n2
    f2re, f2im = np.cos(ang2), np.sin(ang2)
    t['cre2'] = _interleave_cols(f2re, f2re)
    t['cim2'] = _interleave_cols(f2im, f2im)
    k1 = np.arange(n1, dtype=f64)[:, None]
    angw = -2.0 * np.pi * k1 * np.arange(n2in, dtype=f64)[None, :] / n
    wre, wim = np.cos(angw), np.sin(angw)
    t['w4'] = np.stack([_interleave_cols(wre, -wim), _interleave_cols(-wim, -wre),
                        _interleave_cols(wim, wre), _interleave_cols(wre, -wim)], axis=1)
    n2out = n2 if n1 > 1 else l
    tt = np.arange(n2out, dtype=f64)[:, None]
    kk = np.arange(n2, dtype=f64)[None, :]
    angc = -2.0 * np.pi * tt * kk / n2
    cr, ci = np.cos(angc), np.sin(angc)
    t['gc'] = np.block([[cr, ci], [-ci, cr]])
    k2f = np.arange(n2, dtype=f64)[:, None]
    t2f = np.arange(n2, dtype=f64)[None, :]
    angf = -2.0 * np.pi * k2f * t2f / n2
    t['f2re'], t['f2im'] = np.cos(angf), np.sin(angf)
    angwf = -2.0 * np.pi * k1 * np.arange(n2, dtype=f64)[None, :] / n
    t['w2'] = np.stack([np.cos(angwf), np.sin(angwf)], axis=1)
    if n1 > 1:
        h = n1 // 2
        kk1 = np.arange(n1, dtype=f64)[:, None]
        ang1 = -2.0 * np.pi * kk1 * np.arange(h, dtype=f64)[None, :] / n1
        f1re, f1im = np.cos(ang1), np.sin(ang1)
        t['m1'] = np.concatenate([_interleave_cols(f1re, -f1im), _interleave_cols(f1im, f1re)], axis=0)
        d1re, d1im = f1re.T, f1im.T
        t['d1re2'] = _interleave_cols(d1re, d1re)
        t['d1im2'] = _interleave_cols(d1im, d1im)
        wre_t, wim_t = wre.T, wim.T
        t['v4'] = np.stack([_interleave_cols(wre_t, wim_t), _interleave_cols(-wim_t, wre_t),
                            _interleave_cols(-wim_t, wre_t), _interleave_cols(-wre_t, -wim_t)], axis=1)
        ang1f = -2.0 * np.pi * kk1 * np.arange(n1, dtype=f64)[None, :] / n1
        t['m1k'] = np.concatenate([np.cos(ang1f), np.sin(ang1f)], axis=0)
    return {k: np.asarray(v, np.float32) for k, v in t.items()}


def _dot3(a, b):
    ah = a.astype(BF)
    al = (a - ah.astype(F32)).astype(BF)
    bh = b.astype(BF)
    bl = (b - bh.astype(F32)).astype(BF)
    return jnp.dot(jnp.concatenate([ah, ah, al], axis=1), jnp.concatenate([bh, bl, bh], axis=0),
                   preferred_element_type=F32)


def _kfft_body(l, *refs):
    n1, n2 = _split(l)
    n = 2 * l
    if n1 > 1:
        k_ref, ss_ref, m1k_ref, f2re_ref, f2im_ref, w2_ref, o_ref, are_scr, aim_scr = refs
    else:
        k_ref, ss_ref, f2re_ref, f2im_ref, o_ref = refs
    scale = lax.rsqrt(ss_ref[...] + EPS) * (1.0 / n)

    if n1 > 1:
        pitch = are_scr.shape[0] // n1

        def stage1(t2, carry):
            xin = k_ref[pl.ds(t2, n1, stride=n2), :]
            o = _dot3(m1k_ref[...], xin)
            are_scr[pl.ds(t2, n1, stride=pitch), :] = o[:n1]
            aim_scr[pl.ds(t2, n1, stride=pitch), :] = o[n1:]
            return carry

        lax.fori_loop(0, n2, stage1, 0, unroll=KF_UNROLL)

        def slab(k1, carry):
            r0 = pl.multiple_of(k1 * pitch, 8)
            s = jnp.concatenate([are_scr[pl.ds(r0, n2), :], aim_scr[pl.ds(r0, n2), :]], axis=0)
            w2 = w2_ref[k1]
            wre, wim = w2[0:1], w2[1:2]
            hre = f2re_ref[...] * wre - f2im_ref[...] * wim
            him = f2re_ref[...] * wim + f2im_ref[...] * wre
            hblk = jnp.concatenate([jnp.concatenate([hre, -him], axis=1),
                                    jnp.concatenate([him, hre], axis=1)], axis=0)
            x = _dot3(hblk, s)
            o_ref[k1] = _pack(x[:n2] * scale, x[n2:] * scale)
            return carry

        lax.fori_loop(0, n1, slab, 0, unroll=KF_UNROLL)
    else:
        f = jnp.concatenate([f2re_ref[...], f2im_ref[...]], axis=0)
        x = _dot3(f, k_ref[...])
        o_ref[0] = _pack(x[:n2] * scale, x[n2:] * scale)


def _kernel_spectrum(l, kern, sumsq):
    n1, n2 = _split(l)
    n = 2 * l
    c = kern.shape[1]
    ct = 128
    tb = _fft_tables(l)
    names = (['m1k'] if n1 > 1 else []) + ['f2re', 'f2im'] + (['w2'] if n1 > 1 else [])
    consts = [jnp.asarray(tb[k]) for k in names]
    scratch = [pltpu.VMEM((n1 * (n2 + 8), ct), F32)] * 2 if n1 > 1 else []
    return pl.pallas_call(
        functools.partial(_kfft_body, l),
        out_shape=jax.ShapeDtypeStruct((n1, n2, c), jnp.int32),
        grid=(c // ct,),
        in_specs=[pl.BlockSpec((n, ct), lambda j: (0, j), pipeline_mode=pl.Buffered(1)),
                  pl.BlockSpec((1, ct), lambda j: (0, j))] + [_const_spec(a.shape, 1) for a in consts],
        out_specs=pl.BlockSpec((n1, n2, ct), lambda j: (0, 0, j)),
        scratch_shapes=scratch,
        compiler_params=_cparams(1),
        name="hyena_filter_fft",
    )(kern, sumsq, *consts)


def _fft_body(l, *refs):
    n1, n2 = _split(l)
    if n1 > 1:
        (z_ref, kf_ref, cre2_ref, cim2_ref, w4_ref, gc_ref, m1_ref, d1re2_ref, d1im2_ref, v4_ref,
         out_ref, a_scr) = refs
        h = n1 // 2
        ctn = z_ref.shape[0]
        a_pitch = a_scr.shape[1] // n1

        def load(ref, rows):
            return jnp.concatenate([ref[c, rows, :] for c in range(ctn)], axis=1)

        def store(ref, rows, val):
            for c in range(ctn):
                ref[c, rows, :] = val[:, c * LANES:(c + 1) * LANES]

        def stage1(t2, carry):
            xin = load(z_ref, pl.ds(t2, h, stride=n2))
            xb = pltpu.bitcast(xin, BF)
            o = jnp.dot(m1_ref[...], xb, preferred_element_type=F32)
            store(a_scr, pl.ds(t2, n1, stride=a_pitch), _pack(o[:n1], o[n1:]))
            return carry

        lax.fori_loop(0, n2, stage1, 0, unroll=FFT_UNROLL)

        def slab(k1, carry):
            r0 = pl.multiple_of(k1 * a_pitch, 8)
            s = pltpu.bitcast(load(a_scr, pl.ds(r0, n2)), BF)
            w4 = w4_ref[k1]
            cre2 = cre2_ref[...]
            cim2 = cim2_ref[...]
            top = cre2 * w4[0:1] + cim2 * w4[1:2]
            bot = cre2 * w4[2:3] + cim2 * w4[3:4]
            hblk = jnp.concatenate([top, bot], axis=0).astype(BF)
            x = jnp.dot(hblk, s, preferred_element_type=F32)
            kf = kf_ref[k1]
            kre, kim = _unpack_lo(kf), _unpack_hi(kf)
            xre, xim = x[:n2], x[n2:]
            y = jnp.concatenate([xre * kre - xim * kim, xre * kim + xim * kre], axis=0).astype(BF)
            b = jnp.dot(gc_ref[...], y, preferred_element_type=F32)
            store(a_scr, pl.ds(r0, n2), _pack(b[:n2], b[n2:]))
            return carry

        lax.fori_loop(0, n1, slab, 0, unroll=FFT_UNROLL)

        def stage3(t2, carry):
            bin_ = pltpu.bitcast(load(a_scr, pl.ds(t2, n1, stride=a_pitch)), BF)
            v4 = v4_ref[t2]
            d1re2 = d1re2_ref[...]
            d1im2 = d1im2_ref[...]
            top = d1re2 * v4[0:1] + d1im2 * v4[1:2]
            bot = d1re2 * v4[2:3] + d1im2 * v4[3:4]
            m3 = jnp.concatenate([top, bot], axis=0).astype(BF)
            o = jnp.dot(m3, bin_, preferred_element_type=F32)
            store(out_ref, pl.ds(t2, h, stride=n2), _pack(o[:h], o[h:]))
            return carry

        lax.fori_loop(0, n2, stage3, 0, unroll=FFT_UNROLL)
    else:
        z_ref, kf_ref, cre2_ref, cim2_ref, w4_ref, gc_ref, out_ref = refs
        ctn = z_ref.shape[0]
        s = pltpu.bitcast(jnp.concatenate([z_ref[c] for c in range(ctn)], axis=1), BF)
        w4 = w4_ref[0]
        top = cre2_ref[...] * w4[0:1] + cim2_ref[...] * w4[1:2]
        bot = cre2_ref[...] * w4[2:3] + cim2_ref[...] * w4[3:4]
        hblk = jnp.concatenate([top, bot], axis=0).astype(BF)
        x = jnp.dot(hblk, s, preferred_element_type=F32)
        kf = kf_ref[0]
        kre, kim = _unpack_lo(kf), _unpack_hi(kf)
        xre, xim = x[:n2], x[n2:]
        y = jnp.concatenate([xre * kre - xim * kim, xre * kim + xim * kre], axis=0).astype(BF)
        b = jnp.dot(gc_ref[...], y, preferred_element_type=F32)
        packed = _pack(b[:l], b[l:])
        for c in range(ctn):
            out_ref[c] = packed[:, c * LANES:(c + 1) * LANES]


def _fft_conv(zp, kf):
    hp, nct, l, _ = zp.shape
    n1, n2 = _split(l)
    ct = FFT_CT
    ctn = ct // LANES
    tb = _fft_tables(l)
    names = ['cre2', 'cim2', 'w4', 'gc'] + (['m1', 'd1re2', 'd1im2', 'v4'] if n1 > 1 else [])
    consts = [jnp.asarray(tb[k]).astype(BF) if k in ('gc', 'm1') else jnp.asarray(tb[k]) for k in names]
    scratch = [pltpu.VMEM((ctn, n1 * (n2 + 8), LANES), jnp.int32)] if n1 > 1 else []
    one = dict(pipeline_mode=pl.Buffered(1))
    return pl.pallas_call(
        functools.partial(_fft_body, l),
        out_shape=jax.ShapeDtypeStruct((hp, nct, l, LANES), jnp.int32),
        grid=(nct // ctn, hp),
        in_specs=[pl.BlockSpec((None, ctn, l, LANES), lambda j, p: (p, j, 0, 0), **one),
                  pl.BlockSpec((n1, n2, ct), lambda j, p: (0, 0, j), **one)] + [_const_spec(a.shape, 2) for a in consts],
        out_specs=pl.BlockSpec((None, ctn, l, LANES), lambda j, p: (p, j, 0, 0), **one),
        scratch_shapes=scratch,
        compiler_params=_cparams(2),
        name="hyena_fft_conv",
    )(zp, kf, *consts)


def _pool_body(rc, u_ref, w_ref, sc_ref, o_ref):
    l = u_ref.shape[0]
    g = pl.program_id(1)
    win = jnp.left_shift(2, g)
    before = win // 2
    after = win - 1 - before
    n_chunks = l // rc
    halo = 64

    off = (lax.broadcasted_iota(jnp.int32, (rc, rc + 2 * halo), 1) - halo
           - lax.broadcasted_iota(jnp.int32, (rc, rc + 2 * halo), 0))
    band = jnp.where((off >= -before) & (off <= after), 1.0, 0.0).astype(BF)

    def chunk(i, carry):
        r0 = pl.multiple_of(i * rc, rc)
        p0 = pl.multiple_of(jnp.maximum(r0 - halo, 0), halo)
        n0 = pl.multiple_of(jnp.minimum(r0 + rc, l - halo), halo)
        cur = u_ref[pl.ds(r0, rc), :]
        prev = jnp.where(r0 > 0, u_ref[pl.ds(p0, halo), :], jnp.zeros((halo, POOL_GROUP), BF))
        nxt = jnp.where(r0 + rc < l, u_ref[pl.ds(n0, halo), :], jnp.zeros((halo, POOL_GROUP), BF))
        ext = jnp.concatenate([prev, cur, nxt], axis=0)
        sums = jnp.dot(band, ext, preferred_element_type=F32)
        tr = r0 + lax.broadcasted_iota(jnp.int32, (rc, POOL_GROUP), 0)
        cnt = jnp.minimum(tr + after + 1, l) - jnp.maximum(tr - before, 0)
        m = sums / cnt.astype(F32) - cur.astype(F32)
        y = jnp.dot(m.astype(BF), w_ref[...], preferred_element_type=F32) * sc_ref[...]
        o_ref[pl.ds(r0, rc), :] = y.astype(BF)
        return carry

    lax.fori_loop(0, n_chunks, chunk, 0, unroll=2)


def _pool(p, pool_w_bf, pool_scale):
    b, l, _ = p.shape
    rc = min(l, 512)
    ng = len(POOL_WINDOWS)
    base = P_POOL // POOL_GROUP
    return pl.pallas_call(
        functools.partial(_pool_body, rc),
        out_shape=jax.ShapeDtypeStruct((b, l, D_MODEL), BF),
        grid=(b, ng),
        in_specs=[pl.BlockSpec((None, l, POOL_GROUP), lambda bi, g: (bi, 0, base + g)),
                  pl.BlockSpec((None, POOL_GROUP, POOL_GROUP), lambda bi, g: (g, 0, 0)),
                  pl.BlockSpec((1, POOL_GROUP), lambda bi, g: (0, g))],
        out_specs=pl.BlockSpec((None, l, POOL_GROUP), lambda bi, g: (bi, 0, g)),
        compiler_params=_cparams(2),
        name="pool_mixer",
    )(p, pool_w_bf, pool_scale.reshape(1, D_MODEL))


def _mix_body(hp, x_ref, gate_ref, y_ref, z_ref, x0_ref, hd_ref, at_ref, po_ref, wb_ref, wo_ref,
              mod_ref, g_ref, b_ref, o_ref):
    hi = pl.program_id(0) >= hp
    nct = y_ref.shape[0]
    d = D_MODEL
    tm = x_ref.shape[0]
    sub = tm // ROW_SPLIT
    for s in range(ROW_SPLIT):
        rows = slice(s * sub, (s + 1) * sub)
        yw = jnp.concatenate([y_ref[c, rows, :] for c in range(nct)], axis=1)
        zw = jnp.concatenate([z_ref[c, rows, :] for c in range(nct)], axis=1)
        y = jnp.where(hi, _unpack_hi(yw), _unpack_lo(yw))
        z = jnp.where(hi, _unpack_hi(zw), _unpack_lo(zw))
        hy = ((y + z * hd_ref[...]) * x0_ref[rows, :].astype(F32)).astype(BF)
        merged = jax.nn.sigmoid(gate_ref[rows, 0:d].astype(F32)) * jnp.dot(hy, wb_ref[0], preferred_element_type=F32)
        merged += jax.nn.sigmoid(gate_ref[rows, d:2 * d].astype(F32)) * jnp.dot(at_ref[rows, :], wb_ref[1],
                                                                                preferred_element_type=F32)
        merged += jax.nn.sigmoid(gate_ref[rows, 2 * d:].astype(F32)) * jnp.dot(po_ref[rows, :], wb_ref[2],
                                                                               preferred_element_type=F32)
        out = jnp.dot(merged.astype(BF), wo_ref[...], preferred_element_type=F32)
        r = DN_ALPHA * x_ref[rows, :] + mod_ref[2:3, :] * out
        o_ref[rows, :] = _layer_norm(r, g_ref[...], b_ref[...])


def _mix(x, p, y_pair, z_pair, x0c, hy_d, attn, pool, wb_bf, wo_bf, mod, ln_g, ln_b):
    b, l, d = x.shape
    hp = b // 2
    tm = min(l, 512)
    row = lambda bi, i: (bi, i, 0)
    pair = lambda bi, i: (bi % hp, 0, i, 0)
    nct = d // LANES
    return pl.pallas_call(
        functools.partial(_mix_body, hp),
        out_shape=jax.ShapeDtypeStruct((b, l, d), F32),
        grid=(b, l // tm),
        in_specs=[pl.BlockSpec((None, tm, d), row),
                  pl.BlockSpec((None, tm, 3 * d), lambda bi, i: (bi, i, P_GATE // (3 * d))),
                  pl.BlockSpec((None, nct, tm, LANES), pair),
                  pl.BlockSpec((None, nct, tm, LANES), pair),
                  pl.BlockSpec((None, tm, d), row),
                  pl.BlockSpec((1, d), lambda bi, i: (0, 0)),
                  pl.BlockSpec((None, tm, d), row),
                  pl.BlockSpec((None, tm, d), row),
                  pl.BlockSpec((3, d, d), lambda bi, i: (0, 0, 0), pipeline_mode=pl.Buffered(1)),
                  pl.BlockSpec((d, d), lambda bi, i: (0, 0), pipeline_mode=pl.Buffered(1)),
                  pl.BlockSpec((None, 6, d), lambda bi, i: (bi, 0, 0)),
                  pl.BlockSpec((1, d), lambda bi, i: (0, 0)),
                  pl.BlockSpec((1, d), lambda bi, i: (0, 0))],
        out_specs=pl.BlockSpec((None, tm, d), row),
        compiler_params=_cparams(2),
        name="branch_mix",
    )(x, p, y_pair, z_pair, x0c, hy_d.reshape(1, d), attn, pool, wb_bf, wo_bf, mod,
      ln_g.reshape(1, d), ln_b.reshape(1, d))


def _ffn_body(n_chunks, x_ref, mod_ref, w1_ref, w3_ref, w2_ref, g_ref, b_ref, o_ref):
    fc = D_FF // n_chunks
    sub = x_ref.shape[0] // ROW_SPLIT
    for s in range(ROW_SPLIT):
        rows = slice(s * sub, (s + 1) * sub)
        x = x_ref[rows, :]
        h = (x * (1.0 + mod_ref[4:5, :]) + mod_ref[3:4, :]).astype(BF)
        acc = jnp.zeros(x.shape, F32)
        for c in range(n_chunks):
            sl = slice(c * fc, (c + 1) * fc)
            a = jnp.dot(h, w1_ref[:, sl], preferred_element_type=F32)
            bb = jnp.dot(h, w3_ref[:, sl], preferred_element_type=F32)
            gg = (a * jax.nn.sigmoid(a) * bb).astype(BF)
            acc += jnp.dot(gg, w2_ref[sl, :], preferred_element_type=F32)
        r = DN_ALPHA * x + mod_ref[5:6, :] * acc
        o_ref[rows, :] = _layer_norm(r, g_ref[...], b_ref[...])


def _ffn(x, mod, w1_bf, w3_bf, w2_bf, ln_g, ln_b):
    b, l, d = x.shape
    tm = min(l, 512)
    row = lambda bi, i: (bi, i, 0)
    return pl.pallas_call(
        functools.partial(_ffn_body, 2),
        out_shape=jax.ShapeDtypeStruct((b, l, d), F32),
        grid=(b, l // tm),
        in_specs=[pl.BlockSpec((None, tm, d), row),
                  pl.BlockSpec((None, 6, d), lambda bi, i: (bi, 0, 0)),
                  pl.BlockSpec((d, D_FF), lambda bi, i: (0, 0), pipeline_mode=pl.Buffered(1)),
                  pl.BlockSpec((d, D_FF), lambda bi, i: (0, 0), pipeline_mode=pl.Buffered(1)),
                  pl.BlockSpec((D_FF, d), lambda bi, i: (0, 0), pipeline_mode=pl.Buffered(1)),
                  pl.BlockSpec((1, d), lambda bi, i: (0, 0)),
                  pl.BlockSpec((1, d), lambda bi, i: (0, 0))],
        out_specs=pl.BlockSpec((None, tm, d), row),
        compiler_params=_cparams(2),
        name="swiglu_ffn",
    )(x, mod, w1_bf, w3_bf, w2_bf, ln_g.reshape(1, d), ln_b.reshape(1, d))


def _permute_w_in(w):
    return jnp.concatenate([w[:, C_GATE:], w[:, C_HY:C_POOL], w[:, C_Q:C_K], w[:, C_POOL:C_GATE],
                            w[:, C_K:C_V], w[:, C_V:C_HY]], axis=1).astype(BF)


def _v_cols(p):
    return p[..., P_V:P_V + KV_W]


def _k_cols(p):
    return p[..., P_K:P_K + KV_W]


def _keys_transposed(k):
    b, lk, _ = k.shape
    return k.reshape(b, lk, N_KV_HEADS, HEAD_DIM).transpose(0, 2, 3, 1)


def _stream_block(x, p, k_all, v_all, kf, mod, lw):
    attn = _attention(p, _keys_transposed(k_all), v_all)
    z_pair, x0c = _hyena_pre(p, lw['hy_conv_w'], lw['hy_conv_b'])
    y_pair = _fft_conv(z_pair, kf)
    pool = _pool(p, lw['pool_w'], lw['pool_scale'])
    x = _mix(x, p, y_pair, z_pair, x0c, lw['hy_d'], attn, pool, lw['w_branch'], lw['w_out'], mod,
             lw['ln1_g'], lw['ln1_b'])
    return _ffn(x, mod, lw['ffn_w1'], lw['ffn_w3'], lw['ffn_w2'], lw['ln2_g'], lw['ln2_b'])


def kernel(x, c, ctx, c_ctx, w_ada, b_ada, w_in, q_norm_g, k_norm_g, hy_conv_w, hy_conv_b, hf_w1, hf_b1, hf_freq,
           hf_w2, hf_b2, hf_w3, hy_d, pool_w, pool_scale, w_branch, w_out, ln1_g, ln1_b, ln2_g, ln2_b,
           ffn_w1, ffn_w3, ffn_w2):
    b, l, d = x.shape
    lc = ctx.shape[1]
    depth = w_ada.shape[0]
    assert b % 2 == 0 and l % GRID_W == 0

    rows = 16
    c_all = jnp.zeros((rows, d), F32).at[:b].set(c).at[b].set(c_ctx)
    cos_l, sin_l = _rope_tables(l, True)
    cos_c, sin_c = _rope_tables(lc, False)

    xl, xc = x, ctx
    for li in range(depth):
        last = li == depth - 1
        lw = dict(hy_conv_w=hy_conv_w[li], hy_conv_b=hy_conv_b[li], hy_d=hy_d[li],
                  pool_w=pool_w[li].astype(BF), pool_scale=pool_scale[li],
                  w_branch=w_branch[li].astype(BF), w_out=w_out[li].astype(BF),
                  ln1_g=ln1_g[li], ln1_b=ln1_b[li], ln2_g=ln2_g[li], ln2_b=ln2_b[li],
                  ffn_w1=ffn_w1[li].astype(BF), ffn_w3=ffn_w3[li].astype(BF), ffn_w2=ffn_w2[li].astype(BF))
        w_in_p = _permute_w_in(w_in[li])

        mod = _ada(c_all, w_ada[li], b_ada[li]).reshape(rows, 6, d)
        mod_l = mod[:b]
        mod_c = jnp.broadcast_to(mod[b:b + 1], (b, 6, d))

        filt = (hf_w1[li], hf_b1[li], hf_freq[li], hf_w2[li], hf_b2[li], hf_w3[li])
        kf_l = _kernel_spectrum(l, *_hyena_filter(l, *filt))

        p_l = _in_proj(xl, mod_l, w_in_p, cos_l, sin_l, q_norm_g[li], k_norm_g[li])
        p_c = _in_proj(xc, mod_c, w_in_p, cos_c, sin_c, q_norm_g[li], k_norm_g[li])
        k_c, v_c = _k_cols(p_c), _v_cols(p_c)
        k_all = jnp.concatenate([k_c, _k_cols(p_l)], axis=1)
        v_all = jnp.concatenate([v_c, _v_cols(p_l)], axis=1)

        xl = _stream_block(xl, p_l, k_all, v_all, kf_l, mod_l, lw)
        if not last:
            kf_c = _kernel_spectrum(lc, *_hyena_filter(lc, *filt))
            xc = _stream_block(xc, p_c, k_c, v_c, kf_c, mod_c, lw)
    return xl
```

```python
import functools
import math

import numpy as np
import jax
import jax.numpy as jnp
from jax import lax
from jax.experimental import pallas as pl
from jax.experimental.pallas import tpu as pltpu

F32 = jnp.float32
BF = jnp.bfloat16
HIGHEST = lax.Precision.HIGHEST

D_MODEL = 1024
GRID_W = 64
N_HEADS = 8
N_KV_HEADS = 2
HEAD_DIM = 128
GQA_GROUP = N_HEADS // N_KV_HEADS
ROPE_THETA = 10000.0
Q_W = N_HEADS * HEAD_DIM
KV_W = N_KV_HEADS * HEAD_DIM
HY_WIDTH = D_MODEL
HY_EMB = 33
HY_BANDS = (HY_EMB - 1) // 2
HY_FILTER_HIDDEN = 64
HY_TARGET = 1e-2
HY_MAX_DECAY = math.log(HY_TARGET) / 0.3
HY_MIN_DECAY = math.log(HY_TARGET) / 1.5
HY_SHIFT = 0.05
POOL_WINDOWS = (2, 4, 8, 16)
POOL_GROUP = D_MODEL // len(POOL_WINDOWS)
D_FF = 2816
DEPTH = 2
DN_ALPHA = (2 * DEPTH) ** 0.25
EPS = 1e-6

C_Q = 0
C_K = C_Q + Q_W
C_V = C_K + KV_W
C_HY = C_V + KV_W
C_POOL = C_HY + 3 * HY_WIDTH
C_GATE = C_POOL + D_MODEL
IN_WIDTH = C_GATE + 3 * D_MODEL

P_GATE = 0
P_HY = 3 * D_MODEL
P_Q = P_HY + 3 * HY_WIDTH
P_POOL = P_Q + Q_W
P_K = P_POOL + D_MODEL
P_V = P_K + KV_W

Q_SCALE = HEAD_DIM ** -0.5 * math.log2(math.e)
F8 = jnp.float8_e4m3fn
QK_SHIFT = 4.0

ATT_TQ = 256
ATT_TK = 768
ATT_FLAGS = None
ATT_ROW_BLOCKS = 1
ATT_UNROLL = True
FFT_CT = 256
FFT_UNROLL = 8
KF_UNROLL = 4
ROW_SPLIT = 2

VMEM_LIMIT = 60 * 1024 * 1024
FFT_INNER = 128
LANES = 128
MXU_DIM = 256


def _cparams(n_axes):
    return pltpu.CompilerParams(dimension_semantics=("arbitrary",) * n_axes, vmem_limit_bytes=VMEM_LIMIT)


def _const_spec(shape, n_grid):
    nd = len(shape)
    return pl.BlockSpec(shape, lambda *g, _nd=nd: (0,) * _nd)


def _layer_norm(r, g, b):
    mu = jnp.mean(r, axis=-1, keepdims=True)
    d = r - mu
    var = jnp.mean(d * d, axis=-1, keepdims=True)
    return d * lax.rsqrt(var + EPS) * g + b


def _ada_body(c_ref, w_ref, b_ref, o_ref):
    c = c_ref[...]
    s = c * jax.nn.sigmoid(c)
    o_ref[...] = jnp.dot(s, w_ref[...], preferred_element_type=F32, precision=HIGHEST) + b_ref[...]


def _ada(c_all, w_ada, b_ada):
    rows, d = c_all.shape
    n = w_ada.shape[1]
    tn = 512
    return pl.pallas_call(
        _ada_body,
        out_shape=jax.ShapeDtypeStruct((rows, n), F32),
        grid=(n // tn,),
        in_specs=[pl.BlockSpec((rows, d), lambda j: (0, 0)),
                  pl.BlockSpec((d, tn), lambda j: (0, j)),
                  pl.BlockSpec((1, tn), lambda j: (0, j))],
        out_specs=pl.BlockSpec((rows, tn), lambda j: (0, j)),
        compiler_params=_cparams(1),
        name="ada_mod",
    )(c_all, w_ada, b_ada.reshape(1, n))


def _norm_rope(y, g, cos_t, sin_t, scale):
    ms = jnp.mean(y * y, axis=-1, keepdims=True)
    y = y * lax.rsqrt(ms + EPS) * g
    lane = lax.broadcasted_iota(jnp.int32, y.shape, 1)
    up = pltpu.roll(y, HEAD_DIM - 32, 1)
    dn = pltpu.roll(y, 32, 1)
    partner = jnp.where((lane % 64) < 32, up, dn)
    out = y * cos_t + partner * sin_t
    if scale != 1.0:
        out = out * scale
    return out


def _inproj_body(tn, x_ref, mod_ref, w_ref, cos_ref, sin_ref, gq_ref, gk_ref, o_ref):
    h = (x_ref[...] * (1.0 + mod_ref[1:2, :]) + mod_ref[0:1, :]).astype(BF)
    chunks = list(range(w_ref.shape[1] // tn))
    with_epilogue = [j for j in chunks if j * tn < P_Q + Q_W and (j + 1) * tn > P_Q or j * tn < P_K + KV_W and (j + 1) * tn > P_K]
    for j in with_epilogue + [j for j in chunks if j not in with_epilogue]:
        c0 = j * tn
        r = jnp.dot(h, w_ref[:, c0:c0 + tn], preferred_element_type=F32)
        heads = []
        for hh in range(tn // HEAD_DIM):
            col = c0 + hh * HEAD_DIM
            y = r[:, hh * HEAD_DIM:(hh + 1) * HEAD_DIM]
            if P_Q <= col < P_Q + Q_W:
                y = _norm_rope(y, gq_ref[...], cos_ref[...], sin_ref[...], Q_SCALE * QK_SHIFT)
            elif P_K <= col < P_K + KV_W:
                y = _norm_rope(y, gk_ref[...], cos_ref[...], sin_ref[...], 1.0 / QK_SHIFT)
            heads.append(y)
        o_ref[:, c0:c0 + tn] = jnp.concatenate(heads, axis=1).astype(BF)


def _in_proj(x, mod, w_in_bf, cos_t, sin_t, gq, gk):
    b, l, d = x.shape
    n = w_in_bf.shape[1]
    tm = min(l, 512)
    tn = 512
    vec = pl.BlockSpec((1, HEAD_DIM), lambda bi, i: (0, 0))
    tab = pl.BlockSpec((tm, HEAD_DIM), lambda bi, i: (i, 0))
    return pl.pallas_call(
        functools.partial(_inproj_body, tn),
        out_shape=jax.ShapeDtypeStruct((b, l, n), BF),
        grid=(b, l // tm),
        in_specs=[pl.BlockSpec((None, tm, d), lambda bi, i: (bi, i, 0)),
                  pl.BlockSpec((None, 6, d), lambda bi, i: (bi, 0, 0)),
                  pl.BlockSpec((d, n), lambda bi, i: (0, 0), pipeline_mode=pl.Buffered(1)),
                  tab, tab, vec, vec],
        out_specs=pl.BlockSpec((None, tm, n), lambda bi, i: (bi, i, 0)),
        compiler_params=_cparams(2),
        name="in_proj",
    )(x, mod, w_in_bf, cos_t, sin_t, gq.reshape(1, HEAD_DIM), gk.reshape(1, HEAD_DIM))


def _rope_tables(l, with_positions):
    quarter = HEAD_DIM // 4
    if not with_positions:
        return jnp.ones((l, HEAD_DIM), F32), jnp.zeros((l, HEAD_DIM), F32)
    t = jnp.arange(l)
    rows = (t // GRID_W).astype(F32)
    cols = (t % GRID_W).astype(F32)
    inv = jnp.power(ROPE_THETA, -jnp.arange(quarter, dtype=F32) / quarter)
    ar = rows[:, None] * inv[None, :]
    ac = cols[:, None] * inv[None, :]
    cos_t = jnp.concatenate([jnp.cos(ar), jnp.cos(ar), jnp.cos(ac), jnp.cos(ac)], axis=-1)
    sin_t = jnp.concatenate([-jnp.sin(ar), jnp.sin(ar), -jnp.sin(ac), jnp.sin(ac)], axis=-1)
    return cos_t, sin_t


def _flash_body(tk, q_ref, kt_ref, v_ref, o_ref, qs_scr, sa_scr, sb_scr, mpa_scr, mpb_scr, m_scr, accl_scr):
    tq = q_ref.shape[0]
    nk = kt_ref.shape[1] // tk
    for h in range(GQA_GROUP):
        qs_scr[h * tq:(h + 1) * tq, :] = q_ref[:, h * HEAD_DIM:(h + 1) * HEAD_DIM].astype(F8)
    m_scr[...] = jnp.full(m_scr.shape, -jnp.inf, F32)
    accl_scr[...] = jnp.zeros(accl_scr.shape, F32)

    def scores(i, s_scr, mp_scr):
        c0 = pl.multiple_of(i * tk, tk)
        s = jnp.dot(qs_scr[...], kt_ref[:, pl.ds(c0, tk)].astype(F8), preferred_element_type=F32)
        s_scr[...] = s
        mp = s[:, :HEAD_DIM]
        for t in range(1, tk // HEAD_DIM):
            mp = jnp.maximum(mp, s[:, t * HEAD_DIM:(t + 1) * HEAD_DIM])
        mp_scr[...] = mp

    def consume(i, s_scr, mp_scr):
        r0 = pl.multiple_of(i * tk, tk)
        v = v_ref[pl.ds(r0, tk), :]
        vext = jnp.concatenate([v, jnp.ones_like(v)], axis=1)
        rb = m_scr.shape[0] // ATT_ROW_BLOCKS
        for blk in range(ATT_ROW_BLOCKS):
            rows = slice(blk * rb, (blk + 1) * rb)
            m_prev = m_scr[rows, :]
            m_new = jnp.maximum(m_prev, jnp.max(mp_scr[rows, :], axis=-1, keepdims=True))
            alpha = jnp.exp2(m_prev - m_new)
            p = jnp.concatenate([jnp.exp2(s_scr[rows, t * HEAD_DIM:(t + 1) * HEAD_DIM] - m_new)
                                 for t in range(tk // HEAD_DIM)], axis=1).astype(BF)
            upd = jnp.dot(p, vext, preferred_element_type=F32)
            accl_scr[rows, :] = jnp.concatenate([alpha, alpha], axis=1) * accl_scr[rows, :] + upd
            m_scr[rows, :] = m_new

    scores(0, sa_scr, mpa_scr)

    def pair(j, carry):
        scores(2 * j + 1, sb_scr, mpb_scr)
        consume(2 * j, sa_scr, mpa_scr)
        scores(2 * j + 2, sa_scr, mpa_scr)
        consume(2 * j + 1, sb_scr, mpb_scr)
        return carry

    lax.fori_loop(0, (nk - 1) // 2, pair, 0, unroll=ATT_UNROLL)
    if nk % 2 == 1:
        consume(nk - 1, sa_scr, mpa_scr)
    else:
        scores(nk - 1, sb_scr, mpb_scr)
        consume(nk - 2, sa_scr, mpa_scr)
        consume(nk - 1, sb_scr, mpb_scr)
    for h in range(GQA_GROUP):
        rows = slice(h * tq, (h + 1) * tq)
        o_ref[:, h * HEAD_DIM:(h + 1) * HEAD_DIM] = (accl_scr[rows, :HEAD_DIM] / accl_scr[rows, HEAD_DIM:]).astype(BF)


def _attention(p, kt, v):
    b, lq, _ = p.shape
    lk = kt.shape[3]
    tq = min(lq, ATT_TQ)
    tk = ATT_TK if lk % ATT_TK == 0 else 256
    gw = GQA_GROUP * HEAD_DIM
    m = GQA_GROUP * tq
    return pl.pallas_call(
        functools.partial(_flash_body, tk),
        out_shape=jax.ShapeDtypeStruct((b, lq, Q_W), BF),
        grid=(b, N_KV_HEADS, lq // tq),
        in_specs=[pl.BlockSpec((None, tq, gw), lambda bi, g, i: (bi, i, P_Q // gw + g)),
                  pl.BlockSpec((None, None, HEAD_DIM, lk), lambda bi, g, i: (bi, g, 0, 0)),
                  pl.BlockSpec((None, lk, HEAD_DIM), lambda bi, g, i: (bi, 0, g))],
        out_specs=pl.BlockSpec((None, tq, gw), lambda bi, g, i: (bi, i, g)),
        scratch_shapes=[pltpu.VMEM((m, HEAD_DIM), F8),
                        pltpu.VMEM((m, tk), F32),
                        pltpu.VMEM((m, tk), F32),
                        pltpu.VMEM((m, HEAD_DIM), F32),
                        pltpu.VMEM((m, HEAD_DIM), F32),
                        pltpu.VMEM((m, HEAD_DIM), F32),
                        pltpu.VMEM((m, 2 * HEAD_DIM), F32)],
        compiler_params=pltpu.CompilerParams(dimension_semantics=("arbitrary",) * 3,
                                             vmem_limit_bytes=VMEM_LIMIT, flags=ATT_FLAGS),
        name="gqa_attention",
    )(p, kt, v)


def _pack(lo, hi):
    ul = lax.bitcast_convert_type(lo.astype(BF).astype(F32), jnp.uint32)
    uh = lax.bitcast_convert_type(hi.astype(BF).astype(F32), jnp.uint32)
    return lax.bitcast_convert_type(uh | (ul >> 16), jnp.int32)


def _unpack_lo(x):
    return lax.bitcast_convert_type(lax.bitcast_convert_type(x, jnp.uint32) << 16, F32)


def _unpack_hi(x):
    return lax.bitcast_convert_type(lax.bitcast_convert_type(x, jnp.uint32) & jnp.uint32(0xFFFF0000), F32)


def _hypre_body(rc, uv_ref, u0_ref, u1_ref, wv_ref, w0_ref, w1_ref, bv_ref, b0_ref, b1_ref, z_ref, x0_ref):
    l = uv_ref.shape[1]
    n_chunks = l // rc

    def conv(u_ref, half, r0, w_ref, b_ref):
        cur = u_ref[half, pl.ds(r0, rc), :].astype(F32)
        p0 = jnp.maximum(r0 - 16, 0)
        n0 = jnp.minimum(r0 + rc, l - 16)
        prev_row = u_ref[half, pl.ds(pl.multiple_of(p0, 16), 16), :].astype(F32)[15:16]
        next_row = u_ref[half, pl.ds(pl.multiple_of(n0, 16), 16), :].astype(F32)[0:1]
        prev_row = jnp.where(r0 > 0, prev_row, 0.0)
        next_row = jnp.where(r0 + rc < l, next_row, 0.0)
        row = lax.broadcasted_iota(jnp.int32, cur.shape, 0)
        x_prev = jnp.where(row == 0, prev_row, pltpu.roll(cur, 1, 0))
        x_next = jnp.where(row == rc - 1, next_row, pltpu.roll(cur, rc - 1, 0))
        return b_ref[...] + x_prev * w_ref[0:1, :] + cur * w_ref[1:2, :] + x_next * w_ref[2:3, :]

    blk, pitch = _seq_pitch(l)

    def chunk(i, carry):
        r0 = pl.multiple_of(i * rc, rc)
        zs = []
        for half in range(2):
            v = conv(uv_ref, half, r0, wv_ref, bv_ref)
            x1 = conv(u1_ref, half, r0, w1_ref, b1_ref)
            x0 = conv(u0_ref, half, r0, w0_ref, b0_ref)
            x0_ref[half, pl.ds(r0, rc), :] = x0.astype(BF)
            zs.append(v * x1)
        packed = _pack(zs[0], zs[1])
        for j in range(rc // blk):
            dst = pl.multiple_of((i * (rc // blk) + j) * pitch, 8)
            z_ref[pl.ds(dst, blk), :] = packed[j * blk:(j + 1) * blk]
            if pitch > blk:
                z_ref[pl.ds(dst + blk, pitch - blk), :] = jnp.zeros((pitch - blk, packed.shape[1]), jnp.int32)
        return carry

    lax.fori_loop(0, n_chunks, chunk, 0)


def _seq_pitch(l):
    n1, n2 = _split(l)
    return (n2, n2 + 8) if n1 > 1 else (l, l)


def _hyena_pre(p, conv_w, conv_b):
    b, l, n = p.shape
    hp = b // 2
    tc = LANES
    rc = min(l, 512)
    blk, pitch = _seq_pitch(l)
    rows_p = l // blk * pitch
    p4 = p.reshape(2, hp, l, n)
    c = HY_WIDTH
    nb = c // tc
    base = P_HY // tc

    def u_spec(g):
        return pl.BlockSpec((2, None, l, tc), lambda pi, j, _g=g: (0, pi, 0, base + _g * nb + j))

    def w_spec(g):
        return pl.BlockSpec((3, tc), lambda pi, j, _g=g: (0, _g * nb + j))

    def b_spec(g):
        return pl.BlockSpec((1, tc), lambda pi, j, _g=g: (0, _g * nb + j))

    cb = conv_b.reshape(1, 3 * c)
    z, x0 = pl.pallas_call(
        functools.partial(_hypre_body, rc),
        out_shape=(jax.ShapeDtypeStruct((hp, nb, rows_p, tc), jnp.int32), jax.ShapeDtypeStruct((2, hp, l, c), BF)),
        grid=(hp, nb),
        in_specs=[u_spec(0), u_spec(1), u_spec(2), w_spec(0), w_spec(1), w_spec(2), b_spec(0), b_spec(1), b_spec(2)],
        out_specs=(pl.BlockSpec((None, None, rows_p, tc), lambda pi, j: (pi, j, 0, 0)),
                   pl.BlockSpec((2, None, l, tc), lambda pi, j: (0, pi, 0, j))),
        compiler_params=_cparams(2),
        name="hyena_pre",
    )(p4, p4, p4, conv_w, conv_w, conv_w, cb, cb, cb)
    return z, x0.reshape(b, l, c)


def _filter_body(l, tr, f_ref, w1_ref, b1_ref, fr_ref, w2_ref, b2_ref, w3_ref, dl_ref, k_ref, ss_ref):
    i = pl.program_id(0)
    feats = f_ref[...]
    freq = fr_ref[...]
    h = jnp.sin(freq * (jnp.dot(feats, w1_ref[...], preferred_element_type=F32, precision=HIGHEST) + b1_ref[...]))
    h = jnp.sin(freq * (jnp.dot(h, w2_ref[...], preferred_element_type=F32, precision=HIGHEST) + b2_ref[...]))
    h = jnp.dot(h, w3_ref[...], preferred_element_type=F32, precision=HIGHEST)
    t01 = feats[:, 0:1]
    window = jnp.exp(-t01 * dl_ref[...]) + HY_SHIFT
    row = i * tr + lax.broadcasted_iota(jnp.int32, h.shape, 0)
    kern = jnp.where(row == l, 0.0, h * window)
    k_ref[...] = kern

    @pl.when(i == 0)
    def _():
        ss_ref[...] = jnp.zeros_like(ss_ref)

    ss_ref[...] += jnp.sum(kern * kern, axis=0, keepdims=True)


def _hyena_filter(l, w1, b1, freq, w2, b2, w3):
    n = 2 * l
    c = HY_WIDTH
    hid = HY_FILTER_HIDDEN
    tr = min(l, 512)
    j = jnp.arange(n)
    lag = jnp.where(j < l, j, n - j).astype(F32)
    t01 = lag / max(l - 1, 1)
    bands = jnp.linspace(1e-4, HY_BANDS - 1, HY_BANDS, dtype=F32)
    ang = (2.0 * math.pi / l) * lag[:, None] * bands[None, :]
    feats = jnp.concatenate([t01[:, None], jnp.cos(ang), -jnp.sin(ang)], axis=-1)
    feats = jnp.pad(feats, ((0, 0), (0, hid - HY_EMB)))
    w1p = jnp.pad(w1, ((0, hid - HY_EMB), (0, 0)))
    deltas = jnp.abs(jnp.linspace(HY_MIN_DECAY, HY_MAX_DECAY, c, dtype=F32)).reshape(1, c)
    nt = l // tr
    return pl.pallas_call(
        functools.partial(_filter_body, l, tr),
        out_shape=(jax.ShapeDtypeStruct((n, c), F32), jax.ShapeDtypeStruct((1, c), F32)),
        grid=(n // tr,),
        in_specs=[pl.BlockSpec((tr, hid), lambda i: (i, 0)),
                  pl.BlockSpec((hid, hid), lambda i: (0, 0)),
                  pl.BlockSpec((1, hid), lambda i: (0, 0)),
                  pl.BlockSpec((1, hid), lambda i: (0, 0)),
                  pl.BlockSpec((hid, hid), lambda i: (0, 0)),
                  pl.BlockSpec((1, hid), lambda i: (0, 0)),
                  pl.BlockSpec((hid, c), lambda i: (0, i // nt)),
                  pl.BlockSpec((1, c), lambda i: (0, 0))],
        out_specs=(pl.BlockSpec((tr, c), lambda i: (i, 0)),
                   pl.BlockSpec((1, c), lambda i: (0, 0))),
        compiler_params=_cparams(1),
        name="hyena_filter",
    )(feats, w1p, b1.reshape(1, hid), freq.reshape(1, hid), w2, b2.reshape(1, hid), w3, deltas)


def _split(l):
    n = 2 * l
    n1 = n // FFT_INNER if n > 4 * FFT_INNER else 1
    return n1, n // n1


def _interleave_cols(a, b):
    r, k = a.shape
    return np.stack([a, b], axis=-1).reshape(r, 2 * k)


@functools.lru_cache(maxsize=None)
def _fft_tables(l):
    n1, n2 = _split(l)
    n = 2 * l
    f64 = np.float64
    t = {}
    n2in = n2 if n1 > 1 else l
    k2 = np.arange(n2, dtype=f64)[:, None]
    t2 = np.arange(n2in, dtype=f64)[None, :]
    ang2 = -2.0 * np.pi * k2 * t2 / n2
    f2re, f2im = np.cos(ang2), np.sin(ang2)
    t['cre2'] = _interleave_cols(f2re, f2re)
    t['cim2'] = _interleave_cols(f2im, f2im)
    k1 = np.arange(n1, dtype=f64)[:, None]
    angw = -2.0 * np.pi * k1 * np.arange(n2in, dtype=f64)[None, :] / n
    wre, wim = np.cos(angw), np.sin(angw)
    t['w4'] = np.stack([_interleave_cols(wre, -wim), _interleave_cols(-wim, -wre),
                        _interleave_cols(wim, wre), _interleave_cols(wre, -wim)], axis=1)
    n2out = n2 if n1 > 1 else l
    tt = np.arange(n2out, dtype=f64)[:, None]
    kk = np.arange(n2, dtype=f64)[None, :]
    angc = -2.0 * np.pi * tt * kk / n2
    cr, ci = np.cos(angc), np.sin(angc)
    t['gc'] = np.block([[cr, ci], [-ci, cr]])
    k2f = np.arange(n2, dtype=f64)[:, None]
    t2f = np.arange(n2, dtype=f64)[None, :]
    angf = -2.0 * np.pi * k2f * t2f / n2
    t['f2re'], t['f2im'] = np.cos(angf), np.sin(angf)
    angwf = -2.0 * np.pi * k1 * np.arange(n2, dtype=f64)[None, :] / n
    t['w2'] = np.stack([np.cos(angwf), np.sin(angwf)], axis=1)
    if n1 > 1:
        h = n1 // 2
        kk1 = np.arange(n1, dtype=f64)[:, None]
        ang1 = -2.0 * np.pi * kk1 * np.arange(h, dtype=f64)[None, :] / n1
        f1re, f1im = np.cos(ang1), np.sin(ang1)
        t['m1'] = np.concatenate([_interleave_cols(f1re, -f1im), _interleave_cols(f1im, f1re)], axis=0)
        d1re, d1im = f1re.T, f1im.T
        t['d1re2'] = _interleave_cols(d1re, d1re)
        t['d1im2'] = _interleave_cols(d1im, d1im)
        wre_t, wim_t = wre.T, wim.T
        t['v4'] = np.stack([_interleave_cols(wre_t, wim_t), _interleave_cols(-wim_t, wre_t),
                            _interleave_cols(-wim_t, wre_t), _interleave_cols(-wre_t, -wim_t)], axis=1)
        ang1f = -2.0 * np.pi * kk1 * np.arange(n1, dtype=f64)[None, :] / n1
        t['m1k'] = np.concatenate([np.cos(ang1f), np.sin(ang1f)], axis=0)
    return {k: np.asarray(v, np.float32) for k, v in t.items()}


def _dot3(a, b):
    ah = a.astype(BF)
    al = (a - ah.astype(F32)).astype(BF)
    bh = b.astype(BF)
    bl = (b - bh.astype(F32)).astype(BF)
    return jnp.dot(jnp.concatenate([ah, ah, al], axis=1), jnp.concatenate([bh, bl, bh], axis=0),
                   preferred_element_type=F32)


def _kfft_body(l, *refs):
    n1, n2 = _split(l)
    n = 2 * l
    if n1 > 1:
        k_ref, ss_ref, m1k_ref, f2re_ref, f2im_ref, w2_ref, o_ref, are_scr, aim_scr = refs
    else:
        k_ref, ss_ref, f2re_ref, f2im_ref, o_ref = refs
    scale = lax.rsqrt(ss_ref[...] + EPS) * (1.0 / n)

    if n1 > 1:
        pitch = are_scr.shape[0] // n1

        def stage1(t2, carry):
            xin = k_ref[pl.ds(t2, n1, stride=n2), :]
            o = _dot3(m1k_ref[...], xin)
            are_scr[pl.ds(t2, n1, stride=pitch), :] = o[:n1]
            aim_scr[pl.ds(t2, n1, stride=pitch), :] = o[n1:]
            return carry

        lax.fori_loop(0, n2, stage1, 0, unroll=KF_UNROLL)

        def slab(k1, carry):
            r0 = pl.multiple_of(k1 * pitch, 8)
            s = jnp.concatenate([are_scr[pl.ds(r0, n2), :], aim_scr[pl.ds(r0, n2), :]], axis=0)
            w2 = w2_ref[k1]
            wre, wim = w2[0:1], w2[1:2]
            hre = f2re_ref[...] * wre - f2im_ref[...] * wim
            him = f2re_ref[...] * wim + f2im_ref[...] * wre
            hblk = jnp.concatenate([jnp.concatenate([hre, -him], axis=1),
                                    jnp.concatenate([him, hre], axis=1)], axis=0)
            x = _dot3(hblk, s)
            o_ref[k1] = _pack(x[:n2] * scale, x[n2:] * scale)
            return carry

        lax.fori_loop(0, n1, slab, 0, unroll=KF_UNROLL)
    else:
        f = jnp.concatenate([f2re_ref[...], f2im_ref[...]], axis=0)
        x = _dot3(f, k_ref[...])
        o_ref[0] = _pack(x[:n2] * scale, x[n2:] * scale)


def _kernel_spectrum(l, kern, sumsq):
    n1, n2 = _split(l)
    n = 2 * l
    c = kern.shape[1]
    ct = 128
    tb = _fft_tables(l)
    names = (['m1k'] if n1 > 1 else []) + ['f2re', 'f2im'] + (['w2'] if n1 > 1 else [])
    consts = [jnp.asarray(tb[k]) for k in names]
    scratch = [pltpu.VMEM((n1 * (n2 + 8), ct), F32)] * 2 if n1 > 1 else []
    return pl.pallas_call(
        functools.partial(_kfft_body, l),
        out_shape=jax.ShapeDtypeStruct((n1, n2, c), jnp.int32),
        grid=(c // ct,),
        in_specs=[pl.BlockSpec((n, ct), lambda j: (0, j), pipeline_mode=pl.Buffered(1)),
                  pl.BlockSpec((1, ct), lambda j: (0, j))] + [_const_spec(a.shape, 1) for a in consts],
        out_specs=pl.BlockSpec((n1, n2, ct), lambda j: (0, 0, j)),
        scratch_shapes=scratch,
        compiler_params=_cparams(1),
        name="hyena_filter_fft",
    )(kern, sumsq, *consts)


def _fft_body(l, *refs):
    n1, n2 = _split(l)
    if n1 > 1:
        (z_ref, kf_ref, cre2_ref, cim2_ref, w4_ref, gc_ref, m1_ref, d1re2_ref, d1im2_ref, v4_ref,
         out_ref, a_scr) = refs
        h = n1 // 2
        ctn = z_ref.shape[0]
        a_pitch = a_scr.shape[1] // n1
        io_pitch = _seq_pitch(l)[1]

        def load(ref, rows):
            return jnp.concatenate([ref[c, rows, :] for c in range(ctn)], axis=1)

        def store(ref, rows, val):
            for c in range(ctn):
                ref[c, rows, :] = val[:, c * LANES:(c + 1) * LANES]

        def stage1(t2, carry):
            xin = load(z_ref, pl.ds(t2, h, stride=io_pitch))
            xb = pltpu.bitcast(xin, BF)
            o = jnp.dot(m1_ref[...], xb, preferred_element_type=F32)
            store(a_scr, pl.ds(t2, n1, stride=a_pitch), _pack(o[:n1], o[n1:]))
            return carry

        lax.fori_loop(0, n2, stage1, 0, unroll=FFT_UNROLL)

        def slab(k1, carry):
            r0 = pl.multiple_of(k1 * a_pitch, 8)
            s = pltpu.bitcast(load(a_scr, pl.ds(r0, n2)), BF)
            w4 = w4_ref[k1]
            cre2 = cre2_ref[...]
            cim2 = cim2_ref[...]
            top = cre2 * w4[0:1] + cim2 * w4[1:2]
            bot = cre2 * w4[2:3] + cim2 * w4[3:4]
            hblk = jnp.concatenate([top, bot], axis=0).astype(BF)
            x = jnp.dot(hblk, s, preferred_element_type=F32)
            kf = kf_ref[k1]
            kre, kim = _unpack_lo(kf), _unpack_hi(kf)
            xre, xim = x[:n2], x[n2:]
            y = jnp.concatenate([xre * kre - xim * kim, xre * kim + xim * kre], axis=0).astype(BF)
            b = jnp.dot(gc_ref[...], y, preferred_element_type=F32)
            store(a_scr, pl.ds(r0, n2), _pack(b[:n2], b[n2:]))
            return carry

        lax.fori_loop(0, n1, slab, 0, unroll=FFT_UNROLL)

        def stage3(t2, carry):
            bin_ = pltpu.bitcast(load(a_scr, pl.ds(t2, n1, stride=a_pitch)), BF)
            v4 = v4_ref[t2]
            d1re2 = d1re2_ref[...]
            d1im2 = d1im2_ref[...]
            top = d1re2 * v4[0:1] + d1im2 * v4[1:2]
            bot = d1re2 * v4[2:3] + d1im2 * v4[3:4]
            m3 = jnp.concatenate([top, bot], axis=0).astype(BF)
            o = jnp.dot(m3, bin_, preferred_element_type=F32)
            store(out_ref, pl.ds(t2, h, stride=io_pitch), _pack(o[:h], o[h:]))
            return carry

        lax.fori_loop(0, n2, stage3, 0, unroll=FFT_UNROLL)
        for r in range(n2, io_pitch):
            store(out_ref, pl.ds(r, h, stride=io_pitch), jnp.zeros((h, ctn * LANES), jnp.int32))
    else:
        z_ref, kf_ref, cre2_ref, cim2_ref, w4_ref, gc_ref, out_ref = refs
        ctn = z_ref.shape[0]
        s = pltpu.bitcast(jnp.concatenate([z_ref[c] for c in range(ctn)], axis=1), BF)
        w4 = w4_ref[0]
        top = cre2_ref[...] * w4[0:1] + cim2_ref[...] * w4[1:2]
        bot = cre2_ref[...] * w4[2:3] + cim2_ref[...] * w4[3:4]
        hblk = jnp.concatenate([top, bot], axis=0).astype(BF)
        x = jnp.dot(hblk, s, preferred_element_type=F32)
        kf = kf_ref[0]
        kre, kim = _unpack_lo(kf), _unpack_hi(kf)
        xre, xim = x[:n2], x[n2:]
        y = jnp.concatenate([xre * kre - xim * kim, xre * kim + xim * kre], axis=0).astype(BF)
        b = jnp.dot(gc_ref[...], y, preferred_element_type=F32)
        packed = _pack(b[:l], b[l:])
        for c in range(ctn):
            out_ref[c] = packed[:, c * LANES:(c + 1) * LANES]


def _fft_conv(zp, kf):
    hp, nct, rows_p, _ = zp.shape
    n1, n2 = kf.shape[0], kf.shape[1]
    l = n1 * n2 // 2
    assert rows_p == l // _seq_pitch(l)[0] * _seq_pitch(l)[1]
    ct = FFT_CT
    ctn = ct // LANES
    tb = _fft_tables(l)
    names = ['cre2', 'cim2', 'w4', 'gc'] + (['m1', 'd1re2', 'd1im2', 'v4'] if n1 > 1 else [])
    consts = [jnp.asarray(tb[k]).astype(BF) if k in ('gc', 'm1') else jnp.asarray(tb[k]) for k in names]
    scratch = [pltpu.VMEM((ctn, n1 * (n2 + 8), LANES), jnp.int32)] if n1 > 1 else []
    one = dict(pipeline_mode=pl.Buffered(1))
    return pl.pallas_call(
        functools.partial(_fft_body, l),
        out_shape=jax.ShapeDtypeStruct((hp, nct, rows_p, LANES), jnp.int32),
        grid=(nct // ctn, hp),
        in_specs=[pl.BlockSpec((None, ctn, rows_p, LANES), lambda j, p: (p, j, 0, 0), **one),
                  pl.BlockSpec((n1, n2, ct), lambda j, p: (0, 0, j), **one)] + [_const_spec(a.shape, 2) for a in consts],
        out_specs=pl.BlockSpec((None, ctn, rows_p, LANES), lambda j, p: (p, j, 0, 0), **one),
        scratch_shapes=scratch,
        compiler_params=_cparams(2),
        name="hyena_fft_conv",
    )(zp, kf, *consts)


def _pool_body(rc, u_ref, w_ref, sc_ref, o_ref):
    l = u_ref.shape[0]
    g = pl.program_id(1)
    win = jnp.left_shift(2, g)
    before = win // 2
    after = win - 1 - before
    n_chunks = l // rc
    halo = 64

    off = (lax.broadcasted_iota(jnp.int32, (rc, rc + 2 * halo), 1) - halo
           - lax.broadcasted_iota(jnp.int32, (rc, rc + 2 * halo), 0))
    band = jnp.where((off >= -before) & (off <= after), 1.0, 0.0).astype(BF)

    def chunk(i, carry):
        r0 = pl.multiple_of(i * rc, rc)
        p0 = pl.multiple_of(jnp.maximum(r0 - halo, 0), halo)
        n0 = pl.multiple_of(jnp.minimum(r0 + rc, l - halo), halo)
        cur = u_ref[pl.ds(r0, rc), :]
        prev = jnp.where(r0 > 0, u_ref[pl.ds(p0, halo), :], jnp.zeros((halo, POOL_GROUP), BF))
        nxt = jnp.where(r0 + rc < l, u_ref[pl.ds(n0, halo), :], jnp.zeros((halo, POOL_GROUP), BF))
        ext = jnp.concatenate([prev, cur, nxt], axis=0)
        sums = jnp.dot(band, ext, preferred_element_type=F32)
        tr = r0 + lax.broadcasted_iota(jnp.int32, (rc, POOL_GROUP), 0)
        cnt = jnp.minimum(tr + after + 1, l) - jnp.maximum(tr - before, 0)
        m = sums / cnt.astype(F32) - cur.astype(F32)
        y = jnp.dot(m.astype(BF), w_ref[...], preferred_element_type=F32) * sc_ref[...]
        o_ref[pl.ds(r0, rc), :] = y.astype(BF)
        return carry

    lax.fori_loop(0, n_chunks, chunk, 0, unroll=2)


def _pool(p, pool_w_bf, pool_scale):
    b, l, _ = p.shape
    rc = min(l, 512)
    ng = len(POOL_WINDOWS)
    base = P_POOL // POOL_GROUP
    return pl.pallas_call(
        functools.partial(_pool_body, rc),
        out_shape=jax.ShapeDtypeStruct((b, l, D_MODEL), BF),
        grid=(b, ng),
        in_specs=[pl.BlockSpec((None, l, POOL_GROUP), lambda bi, g: (bi, 0, base + g)),
                  pl.BlockSpec((None, POOL_GROUP, POOL_GROUP), lambda bi, g: (g, 0, 0)),
                  pl.BlockSpec((1, POOL_GROUP), lambda bi, g: (0, g))],
        out_specs=pl.BlockSpec((None, l, POOL_GROUP), lambda bi, g: (bi, 0, g)),
        compiler_params=_cparams(2),
        name="pool_mixer",
    )(p, pool_w_bf, pool_scale.reshape(1, D_MODEL))


def _mix_body(hp, blk, pitch, x_ref, gate_ref, y_ref, z_ref, x0_ref, hd_ref, at_ref, po_ref, wb_ref, wo_ref,
              mod_ref, g_ref, b_ref, o_ref):
    hi = pl.program_id(0) >= hp
    nct = y_ref.shape[0]
    d = D_MODEL
    tm = x_ref.shape[0]
    sub = tm // ROW_SPLIT

    def signal_rows(ref, start, size):
        if pitch == blk:
            return jnp.concatenate([ref[c, start:start + size, :] for c in range(nct)], axis=1)
        assert start % blk == 0 and size % blk == 0
        return jnp.concatenate(
            [jnp.concatenate([ref[c, j * pitch:j * pitch + blk, :] for c in range(nct)], axis=1)
             for j in range(start // blk, (start + size) // blk)], axis=0)

    for s in range(ROW_SPLIT):
        rows = slice(s * sub, (s + 1) * sub)
        yw = signal_rows(y_ref, s * sub, sub)
        zw = signal_rows(z_ref, s * sub, sub)
        y = jnp.where(hi, _unpack_hi(yw), _unpack_lo(yw))
        z = jnp.where(hi, _unpack_hi(zw), _unpack_lo(zw))
        hy = ((y + z * hd_ref[...]) * x0_ref[rows, :].astype(F32)).astype(BF)
        merged = jax.nn.sigmoid(gate_ref[rows, 0:d].astype(F32)) * jnp.dot(hy, wb_ref[0], preferred_element_type=F32)
        merged += jax.nn.sigmoid(gate_ref[rows, d:2 * d].astype(F32)) * jnp.dot(at_ref[rows, :], wb_ref[1],
                                                                                preferred_element_type=F32)
        merged += jax.nn.sigmoid(gate_ref[rows, 2 * d:].astype(F32)) * jnp.dot(po_ref[rows, :], wb_ref[2],
                                                                               preferred_element_type=F32)
        out = jnp.dot(merged.astype(BF), wo_ref[...], preferred_element_type=F32)
        r = DN_ALPHA * x_ref[rows, :] + mod_ref[2:3, :] * out
        o_ref[rows, :] = _layer_norm(r, g_ref[...], b_ref[...])


def _mix(x, p, y_pair, z_pair, x0c, hy_d, attn, pool, wb_bf, wo_bf, mod, ln_g, ln_b):
    b, l, d = x.shape
    hp = b // 2
    tm = min(l, 512)
    row = lambda bi, i: (bi, i, 0)
    pair = lambda bi, i: (bi % hp, 0, i, 0)
    nct = d // LANES
    blk, pitch = _seq_pitch(l)
    tm_p = tm // blk * pitch
    return pl.pallas_call(
        functools.partial(_mix_body, hp, blk, pitch),
        out_shape=jax.ShapeDtypeStruct((b, l, d), F32),
        grid=(b, l // tm),
        in_specs=[pl.BlockSpec((None, tm, d), row),
                  pl.BlockSpec((None, tm, 3 * d), lambda bi, i: (bi, i, P_GATE // (3 * d))),
                  pl.BlockSpec((None, nct, tm_p, LANES), pair),
                  pl.BlockSpec((None, nct, tm_p, LANES), pair),
                  pl.BlockSpec((None, tm, d), row),
                  pl.BlockSpec((1, d), lambda bi, i: (0, 0)),
                  pl.BlockSpec((None, tm, d), row),
                  pl.BlockSpec((None, tm, d), row),
                  pl.BlockSpec((3, d, d), lambda bi, i: (0, 0, 0), pipeline_mode=pl.Buffered(1)),
                  pl.BlockSpec((d, d), lambda bi, i: (0, 0), pipeline_mode=pl.Buffered(1)),
                  pl.BlockSpec((None, 6, d), lambda bi, i: (bi, 0, 0)),
                  pl.BlockSpec((1, d), lambda bi, i: (0, 0)),
                  pl.BlockSpec((1, d), lambda bi, i: (0, 0))],
        out_specs=pl.BlockSpec((None, tm, d), row),
        compiler_params=_cparams(2),
        name="branch_mix",
    )(x, p, y_pair, z_pair, x0c, hy_d.reshape(1, d), attn, pool, wb_bf, wo_bf, mod,
      ln_g.reshape(1, d), ln_b.reshape(1, d))


def _ffn_body(n_chunks, x_ref, mod_ref, w1_ref, w3_ref, w2_ref, g_ref, b_ref, o_ref):
    tiles = D_FF // MXU_DIM
    edges = [MXU_DIM * ((tiles * c) // n_chunks) for c in range(n_chunks + 1)]
    sub = x_ref.shape[0] // ROW_SPLIT
    for s in range(ROW_SPLIT):
        rows = slice(s * sub, (s + 1) * sub)
        x = x_ref[rows, :]
        h = (x * (1.0 + mod_ref[4:5, :]) + mod_ref[3:4, :]).astype(BF)
        acc = jnp.zeros(x.shape, F32)
        for c in range(n_chunks):
            sl = slice(edges[c], edges[c + 1])
            a = jnp.dot(h, w1_ref[:, sl], preferred_element_type=F32)
            bb = jnp.dot(h, w3_ref[:, sl], preferred_element_type=F32)
            gg = (a * jax.nn.sigmoid(a) * bb).astype(BF)
            acc += jnp.dot(gg, w2_ref[sl, :], preferred_element_type=F32)
        r = DN_ALPHA * x + mod_ref[5:6, :] * acc
        o_ref[rows, :] = _layer_norm(r, g_ref[...], b_ref[...])


def _ffn(x, mod, w1_bf, w3_bf, w2_bf, ln_g, ln_b):
    b, l, d = x.shape
    tm = min(l, 512)
    row = lambda bi, i: (bi, i, 0)
    return pl.pallas_call(
        functools.partial(_ffn_body, 2),
        out_shape=jax.ShapeDtypeStruct((b, l, d), F32),
        grid=(b, l // tm),
        in_specs=[pl.BlockSpec((None, tm, d), row),
                  pl.BlockSpec((None, 6, d), lambda bi, i: (bi, 0, 0)),
                  pl.BlockSpec((d, D_FF), lambda bi, i: (0, 0), pipeline_mode=pl.Buffered(1)),
                  pl.BlockSpec((d, D_FF), lambda bi, i: (0, 0), pipeline_mode=pl.Buffered(1)),
                  pl.BlockSpec((D_FF, d), lambda bi, i: (0, 0), pipeline_mode=pl.Buffered(1)),
                  pl.BlockSpec((1, d), lambda bi, i: (0, 0)),
                  pl.BlockSpec((1, d), lambda bi, i: (0, 0))],
        out_specs=pl.BlockSpec((None, tm, d), row),
        compiler_params=_cparams(2),
        name="swiglu_ffn",
    )(x, mod, w1_bf, w3_bf, w2_bf, ln_g.reshape(1, d), ln_b.reshape(1, d))


def _permute_w_in(w):
    return jnp.concatenate([w[:, C_GATE:], w[:, C_HY:C_POOL], w[:, C_Q:C_K], w[:, C_POOL:C_GATE],
                            w[:, C_K:C_V], w[:, C_V:C_HY]], axis=1).astype(BF)


def _v_cols(p):
    return p[..., P_V:P_V + KV_W]


def _k_cols(p):
    return p[..., P_K:P_K + KV_W]


def _keys_transposed(k):
    b, lk, _ = k.shape
    return k.reshape(b, lk, N_KV_HEADS, HEAD_DIM).transpose(0, 2, 3, 1)


def _stream_block(x, p, k_all, v_all, kf, mod, lw):
    attn = _attention(p, _keys_transposed(k_all), v_all)
    z_pair, x0c = _hyena_pre(p, lw['hy_conv_w'], lw['hy_conv_b'])
    y_pair = _fft_conv(z_pair, kf)
    pool = _pool(p, lw['pool_w'], lw['pool_scale'])
    x = _mix(x, p, y_pair, z_pair, x0c, lw['hy_d'], attn, pool, lw['w_branch'], lw['w_out'], mod,
             lw['ln1_g'], lw['ln1_b'])
    return _ffn(x, mod, lw['ffn_w1'], lw['ffn_w3'], lw['ffn_w2'], lw['ln2_g'], lw['ln2_b'])


def kernel(x, c, ctx, c_ctx, w_ada, b_ada, w_in, q_norm_g, k_norm_g, hy_conv_w, hy_conv_b, hf_w1, hf_b1, hf_freq,
           hf_w2, hf_b2, hf_w3, hy_d, pool_w, pool_scale, w_branch, w_out, ln1_g, ln1_b, ln2_g, ln2_b,
           ffn_w1, ffn_w3, ffn_w2):
    b, l, d = x.shape
    lc = ctx.shape[1]
    depth = w_ada.shape[0]
    assert b % 2 == 0 and l % GRID_W == 0

    rows = 16
    c_all = jnp.zeros((rows, d), F32).at[:b].set(c).at[b].set(c_ctx)
    cos_l, sin_l = _rope_tables(l, True)
    cos_c, sin_c = _rope_tables(lc, False)

    xl, xc = x, ctx
    for li in range(depth):
        last = li == depth - 1
        lw = dict(hy_conv_w=hy_conv_w[li], hy_conv_b=hy_conv_b[li], hy_d=hy_d[li],
                  pool_w=pool_w[li].astype(BF), pool_scale=pool_scale[li],
                  w_branch=w_branch[li].astype(BF), w_out=w_out[li].astype(BF),
                  ln1_g=ln1_g[li], ln1_b=ln1_b[li], ln2_g=ln2_g[li], ln2_b=ln2_b[li],
                  ffn_w1=ffn_w1[li].astype(BF), ffn_w3=ffn_w3[li].astype(BF), ffn_w2=ffn_w2[li].astype(BF))
        w_in_p = _permute_w_in(w_in[li])

        mod = _ada(c_all, w_ada[li], b_ada[li]).reshape(rows, 6, d)
        mod_l = mod[:b]
        mod_c = jnp.broadcast_to(mod[b:b + 1], (b, 6, d))

        filt = (hf_w1[li], hf_b1[li], hf_freq[li], hf_w2[li], hf_b2[li], hf_w3[li])
        kf_l = _kernel_spectrum(l, *_hyena_filter(l, *filt))

        p_l = _in_proj(xl, mod_l, w_in_p, cos_l, sin_l, q_norm_g[li], k_norm_g[li])
        p_c = _in_proj(xc, mod_c, w_in_p, cos_c, sin_c, q_norm_g[li], k_norm_g[li])
        k_c, v_c = _k_cols(p_c), _v_cols(p_c)
        k_all = jnp.concatenate([k_c, _k_cols(p_l)], axis=1)
        v_all = jnp.concatenate([v_c, _v_cols(p_l)], axis=1)

        xl = _stream_block(xl, p_l, k_all, v_all, kf_l, mod_l, lw)
        if not last:
            kf_c = _kernel_spectrum(lc, *_hyena_filter(lc, *filt))
            xc = _stream_block(xc, p_c, k_c, v_c, kf_c, mod_c, lw)
    return xl
```

```python
import functools
import math

import numpy as np
import jax
import jax.numpy as jnp
from jax import lax
from jax.experimental import pallas as pl
from jax.experimental.pallas import tpu as pltpu

F32 = jnp.float32
BF = jnp.bfloat16
HIGHEST = lax.Precision.HIGHEST

D_MODEL = 1024
GRID_W = 64
N_HEADS = 8
N_KV_HEADS = 2
HEAD_DIM = 128
GQA_GROUP = N_HEADS // N_KV_HEADS
ROPE_THETA = 10000.0
Q_W = N_HEADS * HEAD_DIM
KV_W = N_KV_HEADS * HEAD_DIM
HY_WIDTH = D_MODEL
HY_EMB = 33
HY_BANDS = (HY_EMB - 1) // 2
HY_FILTER_HIDDEN = 64
HY_TARGET = 1e-2
HY_MAX_DECAY = math.log(HY_TARGET) / 0.3
HY_MIN_DECAY = math.log(HY_TARGET) / 1.5
HY_SHIFT = 0.05
POOL_WINDOWS = (2, 4, 8, 16)
POOL_GROUP = D_MODEL // len(POOL_WINDOWS)
D_FF = 2816
DEPTH = 2
DN_ALPHA = (2 * DEPTH) ** 0.25
EPS = 1e-6

C_Q = 0
C_K = C_Q + Q_W
C_V = C_K + KV_W
C_HY = C_V + KV_W
C_POOL = C_HY + 3 * HY_WIDTH
C_GATE = C_POOL + D_MODEL
IN_WIDTH = C_GATE + 3 * D_MODEL

P_GATE = 0
P_HY = 3 * D_MODEL
P_Q = P_HY + 3 * HY_WIDTH
P_POOL = P_Q + Q_W
P_K = P_POOL + D_MODEL
P_V = P_K + KV_W

Q_SCALE = HEAD_DIM ** -0.5 * math.log2(math.e)
F8 = jnp.float8_e4m3fn
QK_SHIFT = 4.0

ATT_TQ = 256
ATT_TK = 768
ATT_FLAGS = None
ATT_ROW_BLOCKS = 1
ATT_UNROLL = True
FFT_CT = 256
FFT_UNROLL = 8
KF_UNROLL = 4
ROW_SPLIT = 2

VMEM_LIMIT = 60 * 1024 * 1024
FFT_INNER = 128
LANES = 128
MXU_DIM = 256


def _cparams(n_axes):
    return pltpu.CompilerParams(dimension_semantics=("arbitrary",) * n_axes, vmem_limit_bytes=VMEM_LIMIT)


def _const_spec(shape, n_grid):
    nd = len(shape)
    return pl.BlockSpec(shape, lambda *g, _nd=nd: (0,) * _nd)


def _layer_norm(r, g, b):
    mu = jnp.mean(r, axis=-1, keepdims=True)
    d = r - mu
    var = jnp.mean(d * d, axis=-1, keepdims=True)
    return d * lax.rsqrt(var + EPS) * g + b


def _ada_body(c_ref, w_ref, b_ref, o_ref):
    c = c_ref[...]
    s = c * jax.nn.sigmoid(c)
    o_ref[...] = jnp.dot(s, w_ref[...], preferred_element_type=F32, precision=HIGHEST) + b_ref[...]


def _ada(c_all, w_ada, b_ada):
    rows, d = c_all.shape
    n = w_ada.shape[1]
    tn = 512
    return pl.pallas_call(
        _ada_body,
        out_shape=jax.ShapeDtypeStruct((rows, n), F32),
        grid=(n // tn,),
        in_specs=[pl.BlockSpec((rows, d), lambda j: (0, 0)),
                  pl.BlockSpec((d, tn), lambda j: (0, j)),
                  pl.BlockSpec((1, tn), lambda j: (0, j))],
        out_specs=pl.BlockSpec((rows, tn), lambda j: (0, j)),
        compiler_params=_cparams(1),
        name="ada_mod",
    )(c_all, w_ada, b_ada.reshape(1, n))


def _norm_rope(y, g, cos_t, sin_t, scale):
    ms = jnp.mean(y * y, axis=-1, keepdims=True)
    y = y * lax.rsqrt(ms + EPS) * g
    lane = lax.broadcasted_iota(jnp.int32, y.shape, 1)
    up = pltpu.roll(y, HEAD_DIM - 32, 1)
    dn = pltpu.roll(y, 32, 1)
    partner = jnp.where((lane % 64) < 32, up, dn)
    out = y * cos_t + partner * sin_t
    if scale != 1.0:
        out = out * scale
    return out


def _inproj_body(tn, x_ref, mod_ref, w_ref, cos_ref, sin_ref, gq_ref, gk_ref, o_ref):
    h = (x_ref[...] * (1.0 + mod_ref[1:2, :]) + mod_ref[0:1, :]).astype(BF)
    chunks = list(range(w_ref.shape[1] // tn))
    with_epilogue = [j for j in chunks if j * tn < P_Q + Q_W and (j + 1) * tn > P_Q or j * tn < P_K + KV_W and (j + 1) * tn > P_K]
    for j in with_epilogue + [j for j in chunks if j not in with_epilogue]:
        c0 = j * tn
        r = jnp.dot(h, w_ref[:, c0:c0 + tn], preferred_element_type=F32)
        heads = []
        for hh in range(tn // HEAD_DIM):
            col = c0 + hh * HEAD_DIM
            y = r[:, hh * HEAD_DIM:(hh + 1) * HEAD_DIM]
            if P_Q <= col < P_Q + Q_W:
                y = _norm_rope(y, gq_ref[...], cos_ref[...], sin_ref[...], Q_SCALE * QK_SHIFT)
            elif P_K <= col < P_K + KV_W:
                y = _norm_rope(y, gk_ref[...], cos_ref[...], sin_ref[...], 1.0 / QK_SHIFT)
            heads.append(y)
        o_ref[:, c0:c0 + tn] = jnp.concatenate(heads, axis=1).astype(BF)


def _in_proj(x, mod, w_in_bf, cos_t, sin_t, gq, gk):
    b, l, d = x.shape
    n = w_in_bf.shape[1]
    tm = min(l, 512)
    tn = 512
    vec = pl.BlockSpec((1, HEAD_DIM), lambda bi, i: (0, 0))
    tab = pl.BlockSpec((tm, HEAD_DIM), lambda bi, i: (i, 0))
    return pl.pallas_call(
        functools.partial(_inproj_body, tn),
        out_shape=jax.ShapeDtypeStruct((b, l, n), BF),
        grid=(b, l // tm),
        in_specs=[pl.BlockSpec((None, tm, d), lambda bi, i: (bi, i, 0)),
                  pl.BlockSpec((None, 6, d), lambda bi, i: (bi, 0, 0)),
                  pl.BlockSpec((d, n), lambda bi, i: (0, 0), pipeline_mode=pl.Buffered(1)),
                  tab, tab, vec, vec],
        out_specs=pl.BlockSpec((None, tm, n), lambda bi, i: (bi, i, 0)),
        compiler_params=_cparams(2),
        name="in_proj",
    )(x, mod, w_in_bf, cos_t, sin_t, gq.reshape(1, HEAD_DIM), gk.reshape(1, HEAD_DIM))


def _rope_tables(l, with_positions):
    quarter = HEAD_DIM // 4
    if not with_positions:
        return jnp.ones((l, HEAD_DIM), F32), jnp.zeros((l, HEAD_DIM), F32)
    t = jnp.arange(l)
    rows = (t // GRID_W).astype(F32)
    cols = (t % GRID_W).astype(F32)
    inv = jnp.power(ROPE_THETA, -jnp.arange(quarter, dtype=F32) / quarter)
    ar = rows[:, None] * inv[None, :]
    ac = cols[:, None] * inv[None, :]
    cos_t = jnp.concatenate([jnp.cos(ar), jnp.cos(ar), jnp.cos(ac), jnp.cos(ac)], axis=-1)
    sin_t = jnp.concatenate([-jnp.sin(ar), jnp.sin(ar), -jnp.sin(ac), jnp.sin(ac)], axis=-1)
    return cos_t, sin_t


def _flash_body(tk, q_ref, kt_ref, v_ref, o_ref, qs_scr, sa_scr, sb_scr, mpa_scr, mpb_scr, m_scr, accl_scr):
    tq = q_ref.shape[0]
    nk = kt_ref.shape[1] // tk
    for h in range(GQA_GROUP):
        qs_scr[h * tq:(h + 1) * tq, :] = q_ref[:, h * HEAD_DIM:(h + 1) * HEAD_DIM].astype(F8)
    m_scr[...] = jnp.full(m_scr.shape, -jnp.inf, F32)
    accl_scr[...] = jnp.zeros(accl_scr.shape, F32)

    def scores(i, s_scr, mp_scr):
        c0 = pl.multiple_of(i * tk, tk)
        s = jnp.dot(qs_scr[...], kt_ref[:, pl.ds(c0, tk)].astype(F8), preferred_element_type=F32)
        s_scr[...] = s
        mp = s[:, :HEAD_DIM]
        for t in range(1, tk // HEAD_DIM):
            mp = jnp.maximum(mp, s[:, t * HEAD_DIM:(t + 1) * HEAD_DIM])
        mp_scr[...] = mp

    def consume(i, s_scr, mp_scr):
        r0 = pl.multiple_of(i * tk, tk)
        v = v_ref[pl.ds(r0, tk), :]
        vext = jnp.concatenate([v, jnp.ones_like(v)], axis=1)
        rb = m_scr.shape[0] // ATT_ROW_BLOCKS
        for blk in range(ATT_ROW_BLOCKS):
            rows = slice(blk * rb, (blk + 1) * rb)
            m_prev = m_scr[rows, :]
            m_new = jnp.maximum(m_prev, jnp.max(mp_scr[rows, :], axis=-1, keepdims=True))
            alpha = jnp.exp2(m_prev - m_new)
            p = jnp.concatenate([jnp.exp2(s_scr[rows, t * HEAD_DIM:(t + 1) * HEAD_DIM] - m_new)
                                 for t in range(tk // HEAD_DIM)], axis=1).astype(BF)
            upd = jnp.dot(p, vext, preferred_element_type=F32)
            accl_scr[rows, :] = jnp.concatenate([alpha, alpha], axis=1) * accl_scr[rows, :] + upd
            m_scr[rows, :] = m_new

    scores(0, sa_scr, mpa_scr)

    def pair(j, carry):
        scores(2 * j + 1, sb_scr, mpb_scr)
        consume(2 * j, sa_scr, mpa_scr)
        scores(2 * j + 2, sa_scr, mpa_scr)
        consume(2 * j + 1, sb_scr, mpb_scr)
        return carry

    lax.fori_loop(0, (nk - 1) // 2, pair, 0, unroll=ATT_UNROLL)
    if nk % 2 == 1:
        consume(nk - 1, sa_scr, mpa_scr)
    else:
        scores(nk - 1, sb_scr, mpb_scr)
        consume(nk - 2, sa_scr, mpa_scr)
        consume(nk - 1, sb_scr, mpb_scr)
    for h in range(GQA_GROUP):
        rows = slice(h * tq, (h + 1) * tq)
        o_ref[:, h * HEAD_DIM:(h + 1) * HEAD_DIM] = (accl_scr[rows, :HEAD_DIM] / accl_scr[rows, HEAD_DIM:]).astype(BF)


def _attention(p, kt, v):
    b, lq, _ = p.shape
    lk = kt.shape[3]
    tq = min(lq, ATT_TQ)
    tk = ATT_TK if lk % ATT_TK == 0 else 256
    gw = GQA_GROUP * HEAD_DIM
    m = GQA_GROUP * tq
    return pl.pallas_call(
        functools.partial(_flash_body, tk),
        out_shape=jax.ShapeDtypeStruct((b, lq, Q_W), BF),
        grid=(b, N_KV_HEADS, lq // tq),
        in_specs=[pl.BlockSpec((None, tq, gw), lambda bi, g, i: (bi, i, P_Q // gw + g)),
                  pl.BlockSpec((None, None, HEAD_DIM, lk), lambda bi, g, i: (bi, g, 0, 0)),
                  pl.BlockSpec((None, lk, HEAD_DIM), lambda bi, g, i: (bi, 0, g))],
        out_specs=pl.BlockSpec((None, tq, gw), lambda bi, g, i: (bi, i, g)),
        scratch_shapes=[pltpu.VMEM((m, HEAD_DIM), F8),
                        pltpu.VMEM((m, tk), F32),
                        pltpu.VMEM((m, tk), F32),
                        pltpu.VMEM((m, HEAD_DIM), F32),
                        pltpu.VMEM((m, HEAD_DIM), F32),
                        pltpu.VMEM((m, HEAD_DIM), F32),
                        pltpu.VMEM((m, 2 * HEAD_DIM), F32)],
        compiler_params=pltpu.CompilerParams(dimension_semantics=("arbitrary",) * 3,
                                             vmem_limit_bytes=VMEM_LIMIT, flags=ATT_FLAGS),
        name="gqa_attention",
    )(p, kt, v)


def _pack(lo, hi):
    ul = lax.bitcast_convert_type(lo.astype(BF).astype(F32), jnp.uint32)
    uh = lax.bitcast_convert_type(hi.astype(BF).astype(F32), jnp.uint32)
    return lax.bitcast_convert_type(uh | (ul >> 16), jnp.int32)


def _unpack_lo(x):
    return lax.bitcast_convert_type(lax.bitcast_convert_type(x, jnp.uint32) << 16, F32)


def _unpack_hi(x):
    return lax.bitcast_convert_type(lax.bitcast_convert_type(x, jnp.uint32) & jnp.uint32(0xFFFF0000), F32)


def _hypre_body(rc, uv_ref, u0_ref, u1_ref, wv_ref, w0_ref, w1_ref, bv_ref, b0_ref, b1_ref, z_ref, x0_ref):
    l = uv_ref.shape[1]
    n_chunks = l // rc

    def conv(u_ref, half, r0, w_ref, b_ref):
        cur = u_ref[half, pl.ds(r0, rc), :].astype(F32)
        p0 = jnp.maximum(r0 - 16, 0)
        n0 = jnp.minimum(r0 + rc, l - 16)
        prev_row = u_ref[half, pl.ds(pl.multiple_of(p0, 16), 16), :].astype(F32)[15:16]
        next_row = u_ref[half, pl.ds(pl.multiple_of(n0, 16), 16), :].astype(F32)[0:1]
        prev_row = jnp.where(r0 > 0, prev_row, 0.0)
        next_row = jnp.where(r0 + rc < l, next_row, 0.0)
        row = lax.broadcasted_iota(jnp.int32, cur.shape, 0)
        x_prev = jnp.where(row == 0, prev_row, pltpu.roll(cur, 1, 0))
        x_next = jnp.where(row == rc - 1, next_row, pltpu.roll(cur, rc - 1, 0))
        return b_ref[...] + x_prev * w_ref[0:1, :] + cur * w_ref[1:2, :] + x_next * w_ref[2:3, :]

    blk, pitch = _seq_pitch(l)

    def chunk(i, carry):
        r0 = pl.multiple_of(i * rc, rc)
        zs = []
        for half in range(2):
            v = conv(uv_ref, half, r0, wv_ref, bv_ref)
            x1 = conv(u1_ref, half, r0, w1_ref, b1_ref)
            x0 = conv(u0_ref, half, r0, w0_ref, b0_ref)
            x0_ref[half, pl.ds(r0, rc), :] = x0.astype(BF)
            zs.append(v * x1)
        packed = _pack(zs[0], zs[1])
        for j in range(rc // blk):
            dst = pl.multiple_of((i * (rc // blk) + j) * pitch, 8)
            z_ref[pl.ds(dst, blk), :] = packed[j * blk:(j + 1) * blk]
            if pitch > blk:
                z_ref[pl.ds(dst + blk, pitch - blk), :] = jnp.zeros((pitch - blk, packed.shape[1]), jnp.int32)
        return carry

    lax.fori_loop(0, n_chunks, chunk, 0)


def _seq_pitch(l):
    n1, n2 = _split(l)
    return (n2, n2 + 8) if n1 > 1 else (l, l)


def _hyena_pre(p, conv_w, conv_b):
    b, l, n = p.shape
    hp = b // 2
    tc = LANES
    rc = min(l, 512)
    blk, pitch = _seq_pitch(l)
    rows_p = l // blk * pitch
    p4 = p.reshape(2, hp, l, n)
    c = HY_WIDTH
    nb = c // tc
    base = P_HY // tc

    def u_spec(g):
        return pl.BlockSpec((2, None, l, tc), lambda pi, j, _g=g: (0, pi, 0, base + _g * nb + j))

    def w_spec(g):
        return pl.BlockSpec((3, tc), lambda pi, j, _g=g: (0, _g * nb + j))

    def b_spec(g):
        return pl.BlockSpec((1, tc), lambda pi, j, _g=g: (0, _g * nb + j))

    cb = conv_b.reshape(1, 3 * c)
    z, x0 = pl.pallas_call(
        functools.partial(_hypre_body, rc),
        out_shape=(jax.ShapeDtypeStruct((hp, nb, rows_p, tc), jnp.int32), jax.ShapeDtypeStruct((2, hp, l, c), BF)),
        grid=(hp, nb),
        in_specs=[u_spec(0), u_spec(1), u_spec(2), w_spec(0), w_spec(1), w_spec(2), b_spec(0), b_spec(1), b_spec(2)],
        out_specs=(pl.BlockSpec((None, None, rows_p, tc), lambda pi, j: (pi, j, 0, 0)),
                   pl.BlockSpec((2, None, l, tc), lambda pi, j: (0, pi, 0, j))),
        compiler_params=_cparams(2),
        name="hyena_pre",
    )(p4, p4, p4, conv_w, conv_w, conv_w, cb, cb, cb)
    return z, x0.reshape(b, l, c)


def _filter_body(l, tr, f_ref, w1_ref, b1_ref, fr_ref, w2_ref, b2_ref, w3_ref, dl_ref, k_ref, ss_ref):
    i = pl.program_id(0)
    feats = f_ref[...]
    freq = fr_ref[...]
    h = jnp.sin(freq * (jnp.dot(feats, w1_ref[...], preferred_element_type=F32, precision=HIGHEST) + b1_ref[...]))
    h = jnp.sin(freq * (jnp.dot(h, w2_ref[...], preferred_element_type=F32, precision=HIGHEST) + b2_ref[...]))
    h = jnp.dot(h, w3_ref[...], preferred_element_type=F32, precision=HIGHEST)
    t01 = feats[:, 0:1]
    window = jnp.exp(-t01 * dl_ref[...]) + HY_SHIFT
    row = i * tr + lax.broadcasted_iota(jnp.int32, h.shape, 0)
    kern = jnp.where(row == l, 0.0, h * window)
    k_ref[...] = kern

    @pl.when(i == 0)
    def _():
        ss_ref[...] = jnp.zeros_like(ss_ref)

    ss_ref[...] += jnp.sum(kern * kern, axis=0, keepdims=True)


def _hyena_filter(l, w1, b1, freq, w2, b2, w3):
    n = 2 * l
    c = HY_WIDTH
    hid = HY_FILTER_HIDDEN
    tr = min(l, 512)
    j = jnp.arange(n)
    lag = jnp.where(j < l, j, n - j).astype(F32)
    t01 = lag / max(l - 1, 1)
    bands = jnp.linspace(1e-4, HY_BANDS - 1, HY_BANDS, dtype=F32)
    ang = (2.0 * math.pi / l) * lag[:, None] * bands[None, :]
    feats = jnp.concatenate([t01[:, None], jnp.cos(ang), -jnp.sin(ang)], axis=-1)
    feats = jnp.pad(feats, ((0, 0), (0, hid - HY_EMB)))
    w1p = jnp.pad(w1, ((0, hid - HY_EMB), (0, 0)))
    deltas = jnp.abs(jnp.linspace(HY_MIN_DECAY, HY_MAX_DECAY, c, dtype=F32)).reshape(1, c)
    nt = l // tr
    return pl.pallas_call(
        functools.partial(_filter_body, l, tr),
        out_shape=(jax.ShapeDtypeStruct((n, c), F32), jax.ShapeDtypeStruct((1, c), F32)),
        grid=(n // tr,),
        in_specs=[pl.BlockSpec((tr, hid), lambda i: (i, 0)),
                  pl.BlockSpec((hid, hid), lambda i: (0, 0)),
                  pl.BlockSpec((1, hid), lambda i: (0, 0)),
                  pl.BlockSpec((1, hid), lambda i: (0, 0)),
                  pl.BlockSpec((hid, hid), lambda i: (0, 0)),
                  pl.BlockSpec((1, hid), lambda i: (0, 0)),
                  pl.BlockSpec((hid, c), lambda i: (0, i // nt)),
                  pl.BlockSpec((1, c), lambda i: (0, 0))],
        out_specs=(pl.BlockSpec((tr, c), lambda i: (i, 0)),
                   pl.BlockSpec((1, c), lambda i: (0, 0))),
        compiler_params=_cparams(1),
        name="hyena_filter",
    )(feats, w1p, b1.reshape(1, hid), freq.reshape(1, hid), w2, b2.reshape(1, hid), w3, deltas)


def _split(l):
    n = 2 * l
    n1 = n // FFT_INNER if n > 4 * FFT_INNER else 1
    return n1, n // n1


def _interleave_cols(a, b):
    r, k = a.shape
    return np.stack([a, b], axis=-1).reshape(r, 2 * k)


@functools.lru_cache(maxsize=None)
def _fft_tables(l):
    n1, n2 = _split(l)
    n = 2 * l
    f64 = np.float64
    t = {}
    n2in = n2 if n1 > 1 else l
    k2 = np.arange(n2, dtype=f64)[:, None]
    t2 = np.arange(n2in, dtype=f64)[None, :]
    ang2 = -2.0 * np.pi * k2 * t2 / n2
    f2re, f2im = np.cos(ang2), np.sin(ang2)
    t['cre2'] = _interleave_cols(f2re, f2re)
    t['cim2'] = _interleave_cols(f2im, f2im)
    k1 = np.arange(n1, dtype=f64)[:, None]
    angw = -2.0 * np.pi * k1 * np.arange(n2in, dtype=f64)[None, :] / n
    wre, wim = np.cos(angw), np.sin(angw)
    t['w4'] = np.stack([_interleave_cols(wre, -wim), _interleave_cols(-wim, -wre),
                        _interleave_cols(wim, wre), _interleave_cols(wre, -wim)], axis=1)
    n2out = n2 if n1 > 1 else l
    tt = np.arange(n2out, dtype=f64)[:, None]
    kk = np.arange(n2, dtype=f64)[None, :]
    angc = -2.0 * np.pi * tt * kk / n2
    cr, ci = np.cos(angc), np.sin(angc)
    t['gc'] = np.block([[cr, ci], [-ci, cr]])
    k2f = np.arange(n2, dtype=f64)[:, None]
    t2f = np.arange(n2, dtype=f64)[None, :]
    angf = -2.0 * np.pi * k2f * t2f / n2
    t['f2re'], t['f2im'] = np.cos(angf), np.sin(angf)
    angwf = -2.0 * np.pi * k1 * np.arange(n2, dtype=f64)[None, :] / n
    t['w2'] = np.stack([np.cos(angwf), np.sin(angwf)], axis=1)
    if n1 > 1:
        h = n1 // 2
        kk1 = np.arange(n1, dtype=f64)[:, None]
        ang1 = -2.0 * np.pi * kk1 * np.arange(h, dtype=f64)[None, :] / n1
        f1re, f1im = np.cos(ang1), np.sin(ang1)
        t['m1'] = np.concatenate([_interleave_cols(f1re, -f1im), _interleave_cols(f1im, f1re)], axis=0)
        d1re, d1im = f1re.T, f1im.T
        t['d1re2'] = _interleave_cols(d1re, d1re)
        t['d1im2'] = _interleave_cols(d1im, d1im)
        wre_t, wim_t = wre.T, wim.T
        t['v4'] = np.stack([_interleave_cols(wre_t, wim_t), _interleave_cols(-wim_t, wre_t),
                            _interleave_cols(-wim_t, wre_t), _interleave_cols(-wre_t, -wim_t)], axis=1)
        ang1f = -2.0 * np.pi * kk1 * np.arange(n1, dtype=f64)[None, :] / n1
        t['m1k'] = np.concatenate([np.cos(ang1f), np.sin(ang1f)], axis=0)
    return {k: np.asarray(v, np.float32) for k, v in t.items()}


def _dot3(a, b):
    ah = a.astype(BF)
    al = (a - ah.astype(F32)).astype(BF)
    bh = b.astype(BF)
    bl = (b - bh.astype(F32)).astype(BF)
    return jnp.dot(jnp.concatenate([ah, ah, al], axis=1), jnp.concatenate([bh, bl, bh], axis=0),
                   preferred_element_type=F32)


def _kfft_body(l, *refs):
    n1, n2 = _split(l)
    n = 2 * l
    if n1 > 1:
        k_ref, ss_ref, m1k_ref, f2re_ref, f2im_ref, w2_ref, o_ref, are_scr, aim_scr = refs
    else:
        k_ref, ss_ref, f2re_ref, f2im_ref, o_ref = refs
    scale = lax.rsqrt(ss_ref[...] + EPS) * (1.0 / n)

    if n1 > 1:
        pitch = are_scr.shape[0] // n1

        def stage1(t2, carry):
            xin = k_ref[pl.ds(t2, n1, stride=n2), :]
            o = _dot3(m1k_ref[...], xin)
            are_scr[pl.ds(t2, n1, stride=pitch), :] = o[:n1]
            aim_scr[pl.ds(t2, n1, stride=pitch), :] = o[n1:]
            return carry

        lax.fori_loop(0, n2, stage1, 0, unroll=KF_UNROLL)

        def slab(k1, carry):
            r0 = pl.multiple_of(k1 * pitch, 8)
            s = jnp.concatenate([are_scr[pl.ds(r0, n2), :], aim_scr[pl.ds(r0, n2), :]], axis=0)
            w2 = w2_ref[k1]
            wre, wim = w2[0:1], w2[1:2]
            hre = f2re_ref[...] * wre - f2im_ref[...] * wim
            him = f2re_ref[...] * wim + f2im_ref[...] * wre
            hblk = jnp.concatenate([jnp.concatenate([hre, -him], axis=1),
                                    jnp.concatenate([him, hre], axis=1)], axis=0)
            x = _dot3(hblk, s)
            o_ref[k1] = _pack(x[:n2] * scale, x[n2:] * scale)
            return carry

        lax.fori_loop(0, n1, slab, 0, unroll=KF_UNROLL)
    else:
        f = jnp.concatenate([f2re_ref[...], f2im_ref[...]], axis=0)
        x = _dot3(f, k_ref[...])
        o_ref[0] = _pack(x[:n2] * scale, x[n2:] * scale)


def _kernel_spectrum(l, kern, sumsq):
    n1, n2 = _split(l)
    n = 2 * l
    c = kern.shape[1]
    ct = 128
    tb = _fft_tables(l)
    names = (['m1k'] if n1 > 1 else []) + ['f2re', 'f2im'] + (['w2'] if n1 > 1 else [])
    consts = [jnp.asarray(tb[k]) for k in names]
    scratch = [pltpu.VMEM((n1 * (n2 + 8), ct), F32)] * 2 if n1 > 1 else []
    return pl.pallas_call(
        functools.partial(_kfft_body, l),
        out_shape=jax.ShapeDtypeStruct((n1, n2, c), jnp.int32),
        grid=(c // ct,),
        in_specs=[pl.BlockSpec((n, ct), lambda j: (0, j), pipeline_mode=pl.Buffered(1)),
                  pl.BlockSpec((1, ct), lambda j: (0, j))] + [_const_spec(a.shape, 1) for a in consts],
        out_specs=pl.BlockSpec((n1, n2, ct), lambda j: (0, 0, j)),
        scratch_shapes=scratch,
        compiler_params=_cparams(1),
        name="hyena_filter_fft",
    )(kern, sumsq, *consts)


def _fft_body(l, *refs):
    n1, n2 = _split(l)
    if n1 > 1:
        (z_ref, kf_ref, cre2_ref, cim2_ref, w4_ref, gc_ref, m1_ref, d1re2_ref, d1im2_ref, v4_ref,
         out_ref, a_scr) = refs
        h = n1 // 2
        ctn = z_ref.shape[0]
        a_pitch = a_scr.shape[1] // n1
        io_pitch = _seq_pitch(l)[1]

        def load(ref, rows):
            return jnp.concatenate([ref[c, rows, :] for c in range(ctn)], axis=1)

        def store(ref, rows, val):
            for c in range(ctn):
                ref[c, rows, :] = val[:, c * LANES:(c + 1) * LANES]

        def stage1(t2, carry):
            xin = load(z_ref, pl.ds(t2, h, stride=io_pitch))
            xb = pltpu.bitcast(xin, BF)
            o = jnp.dot(m1_ref[...], xb, preferred_element_type=F32)
            store(a_scr, pl.ds(t2, n1, stride=a_pitch), _pack(o[:n1], o[n1:]))
            return carry

        lax.fori_loop(0, n2, stage1, 0, unroll=FFT_UNROLL)

        def slab(k1, carry):
            r0 = pl.multiple_of(k1 * a_pitch, 8)
            s = pltpu.bitcast(load(a_scr, pl.ds(r0, n2)), BF)
            w4 = w4_ref[k1]
            cre2 = cre2_ref[...]
            cim2 = cim2_ref[...]
            top = cre2 * w4[0:1] + cim2 * w4[1:2]
            bot = cre2 * w4[2:3] + cim2 * w4[3:4]
            hblk = jnp.concatenate([top, bot], axis=0).astype(BF)
            x = jnp.dot(hblk, s, preferred_element_type=F32)
            kf = kf_ref[k1]
            kre, kim = _unpack_lo(kf), _unpack_hi(kf)
            xre, xim = x[:n2], x[n2:]
            y = jnp.concatenate([xre * kre - xim * kim, xre * kim + xim * kre], axis=0).astype(BF)
            b = jnp.dot(gc_ref[...], y, preferred_element_type=F32)
            store(a_scr, pl.ds(r0, n2), _pack(b[:n2], b[n2:]))
            return carry

        lax.fori_loop(0, n1, slab, 0, unroll=FFT_UNROLL)

        def stage3(t2, carry):
            bin_ = pltpu.bitcast(load(a_scr, pl.ds(t2, n1, stride=a_pitch)), BF)
            v4 = v4_ref[t2]
            d1re2 = d1re2_ref[...]
            d1im2 = d1im2_ref[...]
            top = d1re2 * v4[0:1] + d1im2 * v4[1:2]
            bot = d1re2 * v4[2:3] + d1im2 * v4[3:4]
            m3 = jnp.concatenate([top, bot], axis=0).astype(BF)
            o = jnp.dot(m3, bin_, preferred_element_type=F32)
            store(out_ref, pl.ds(t2, h, stride=io_pitch), _pack(o[:h], o[h:]))
            return carry

        lax.fori_loop(0, n2, stage3, 0, unroll=FFT_UNROLL)
        for r in range(n2, io_pitch):
            store(out_ref, pl.ds(r, h, stride=io_pitch), jnp.zeros((h, ctn * LANES), jnp.int32))
    else:
        z_ref, kf_ref, cre2_ref, cim2_ref, w4_ref, gc_ref, out_ref = refs
        ctn = z_ref.shape[0]
        s = pltpu.bitcast(jnp.concatenate([z_ref[c] for c in range(ctn)], axis=1), BF)
        w4 = w4_ref[0]
        top = cre2_ref[...] * w4[0:1] + cim2_ref[...] * w4[1:2]
        bot = cre2_ref[...] * w4[2:3] + cim2_ref[...] * w4[3:4]
        hblk = jnp.concatenate([top, bot], axis=0).astype(BF)
        x = jnp.dot(hblk, s, preferred_element_type=F32)
        kf = kf_ref[0]
        kre, kim = _unpack_lo(kf), _unpack_hi(kf)
        xre, xim = x[:n2], x[n2:]
        y = jnp.concatenate([xre * kre - xim * kim, xre * kim + xim * kre], axis=0).astype(BF)
        b = jnp.dot(gc_ref[...], y, preferred_element_type=F32)
        packed = _pack(b[:l], b[l:])
        for c in range(ctn):
            out_ref[c] = packed[:, c * LANES:(c + 1) * LANES]


def _fft_conv(zp, kf):
    hp, nct, rows_p, _ = zp.shape
    n1, n2 = kf.shape[0], kf.shape[1]
    l = n1 * n2 // 2
    assert rows_p == l // _seq_pitch(l)[0] * _seq_pitch(l)[1]
    ct = FFT_CT
    ctn = ct // LANES
    tb = _fft_tables(l)
    names = ['cre2', 'cim2', 'w4', 'gc'] + (['m1', 'd1re2', 'd1im2', 'v4'] if n1 > 1 else [])
    consts = [jnp.asarray(tb[k]).astype(BF) if k in ('gc', 'm1') else jnp.asarray(tb[k]) for k in names]
    scratch = [pltpu.VMEM((ctn, n1 * (n2 + 8), LANES), jnp.int32)] if n1 > 1 else []
    one = dict(pipeline_mode=pl.Buffered(1))
    return pl.pallas_call(
        functools.partial(_fft_body, l),
        out_shape=jax.ShapeDtypeStruct((hp, nct, rows_p, LANES), jnp.int32),
        grid=(nct // ctn, hp),
        in_specs=[pl.BlockSpec((None, ctn, rows_p, LANES), lambda j, p: (p, j, 0, 0), **one),
                  pl.BlockSpec((n1, n2, ct), lambda j, p: (0, 0, j), **one)] + [_const_spec(a.shape, 2) for a in consts],
        out_specs=pl.BlockSpec((None, ctn, rows_p, LANES), lambda j, p: (p, j, 0, 0), **one),
        scratch_shapes=scratch,
        compiler_params=_cparams(2),
        name="hyena_fft_conv",
    )(zp, kf, *consts)


POOL_HALO = 64


@functools.lru_cache(maxsize=None)
def _pool_bands(rows):
    off = np.arange(rows + 2 * POOL_HALO)[None, :] - POOL_HALO - np.arange(rows)[:, None]
    bands = []
    for w in POOL_WINDOWS:
        before = w // 2
        after = w - 1 - before
        bands.append(((off >= -before) & (off <= after)).astype(np.float32))
    return np.stack(bands)


def _pool_rows(ext, cur, t0, l, band_ref, pw_ref, ps_ref):
    rows = cur.shape[0]
    outs = []
    for g, w in enumerate(POOL_WINDOWS):
        before = w // 2
        after = w - 1 - before
        cols = slice(g * POOL_GROUP, (g + 1) * POOL_GROUP)
        sums = jnp.dot(band_ref[g], ext[:, cols], preferred_element_type=F32)
        t = t0 + lax.broadcasted_iota(jnp.int32, (rows, POOL_GROUP), 0)
        cnt = jnp.minimum(t + after + 1, l) - jnp.maximum(t - before, 0)
        m = sums / cnt.astype(F32) - cur[:, cols].astype(F32)
        outs.append(jnp.dot(m.astype(BF), pw_ref[g], preferred_element_type=F32) * ps_ref[:, cols])
    return jnp.concatenate(outs, axis=1).astype(BF)


def _mix_body(hp, blk, pitch, l, x_ref, gate_ref, y_ref, z_ref, x0_ref, hd_ref, at_ref, u_ref, up_ref, un_ref,
              band_ref, pw_ref, ps_ref, wb_ref, wo_ref, mod_ref, g_ref, b_ref, o_ref):
    hi = pl.program_id(0) >= hp
    i = pl.program_id(1)
    nct = y_ref.shape[0]
    d = D_MODEL
    tm = x_ref.shape[0]
    sub = tm // ROW_SPLIT
    prev = jnp.where(i > 0, up_ref[...], jnp.zeros(up_ref.shape, BF))
    nxt = jnp.where((i + 1) * tm < l, un_ref[...], jnp.zeros(un_ref.shape, BF))
    ext_all = jnp.concatenate([prev, u_ref[...], nxt], axis=0)

    def signal_rows(ref, start, size):
        if pitch == blk:
            return jnp.concatenate([ref[c, start:start + size, :] for c in range(nct)], axis=1)
        assert start % blk == 0 and size % blk == 0
        return jnp.concatenate(
            [jnp.concatenate([ref[c, j * pitch:j * pitch + blk, :] for c in range(nct)], axis=1)
             for j in range(start // blk, (start + size) // blk)], axis=0)

    for s in range(ROW_SPLIT):
        rows = slice(s * sub, (s + 1) * sub)
        yw = signal_rows(y_ref, s * sub, sub)
        zw = signal_rows(z_ref, s * sub, sub)
        y = jnp.where(hi, _unpack_hi(yw), _unpack_lo(yw))
        z = jnp.where(hi, _unpack_hi(zw), _unpack_lo(zw))
        hy = ((y + z * hd_ref[...]) * x0_ref[rows, :].astype(F32)).astype(BF)
        merged = jax.nn.sigmoid(gate_ref[rows, 0:d].astype(F32)) * jnp.dot(hy, wb_ref[0], preferred_element_type=F32)
        merged += jax.nn.sigmoid(gate_ref[rows, d:2 * d].astype(F32)) * jnp.dot(at_ref[rows, :], wb_ref[1],
                                                                                preferred_element_type=F32)
        pooled = _pool_rows(ext_all[s * sub:(s + 1) * sub + 2 * POOL_HALO], u_ref[rows, :], i * tm + s * sub, l,
                            band_ref, pw_ref, ps_ref)
        merged += jax.nn.sigmoid(gate_ref[rows, 2 * d:].astype(F32)) * jnp.dot(pooled, wb_ref[2],
                                                                               preferred_element_type=F32)
        out = jnp.dot(merged.astype(BF), wo_ref[...], preferred_element_type=F32)
        r = DN_ALPHA * x_ref[rows, :] + mod_ref[2:3, :] * out
        o_ref[rows, :] = _layer_norm(r, g_ref[...], b_ref[...])


def _mix(x, p, y_pair, z_pair, x0c, hy_d, attn, pool_w_bf, pool_scale, wb_bf, wo_bf, mod, ln_g, ln_b):
    b, l, d = x.shape
    hp = b // 2
    tm = min(l, 512)
    row = lambda bi, i: (bi, i, 0)
    pair = lambda bi, i: (bi % hp, 0, i, 0)
    nct = d // LANES
    blk, pitch = _seq_pitch(l)
    tm_p = tm // blk * pitch
    ng = len(POOL_WINDOWS)
    hb = tm // POOL_HALO
    n_hb = l // POOL_HALO
    pool_col = P_POOL // d
    bands = jnp.asarray(_pool_bands(tm // ROW_SPLIT), BF)
    return pl.pallas_call(
        functools.partial(_mix_body, hp, blk, pitch, l),
        out_shape=jax.ShapeDtypeStruct((b, l, d), F32),
        grid=(b, l // tm),
        in_specs=[pl.BlockSpec((None, tm, d), row),
                  pl.BlockSpec((None, tm, 3 * d), lambda bi, i: (bi, i, P_GATE // (3 * d))),
                  pl.BlockSpec((None, nct, tm_p, LANES), pair),
                  pl.BlockSpec((None, nct, tm_p, LANES), pair),
                  pl.BlockSpec((None, tm, d), row),
                  pl.BlockSpec((1, d), lambda bi, i: (0, 0)),
                  pl.BlockSpec((None, tm, d), row),
                  pl.BlockSpec((None, tm, d), lambda bi, i: (bi, i, pool_col)),
                  pl.BlockSpec((None, POOL_HALO, d), lambda bi, i: (bi, jnp.maximum(i * hb - 1, 0), pool_col)),
                  pl.BlockSpec((None, POOL_HALO, d), lambda bi, i: (bi, jnp.minimum((i + 1) * hb, n_hb - 1), pool_col)),
                  pl.BlockSpec(bands.shape, lambda bi, i: (0, 0, 0)),
                  pl.BlockSpec((ng, POOL_GROUP, POOL_GROUP), lambda bi, i: (0, 0, 0)),
                  pl.BlockSpec((1, d), lambda bi, i: (0, 0)),
                  pl.BlockSpec((3, d, d), lambda bi, i: (0, 0, 0), pipeline_mode=pl.Buffered(1)),
                  pl.BlockSpec((d, d), lambda bi, i: (0, 0), pipeline_mode=pl.Buffered(1)),
                  pl.BlockSpec((None, 6, d), lambda bi, i: (bi, 0, 0)),
                  pl.BlockSpec((1, d), lambda bi, i: (0, 0)),
                  pl.BlockSpec((1, d), lambda bi, i: (0, 0))],
        out_specs=pl.BlockSpec((None, tm, d), row),
        compiler_params=_cparams(2),
        name="branch_mix",
    )(x, p, y_pair, z_pair, x0c, hy_d.reshape(1, d), attn, p, p, p, bands, pool_w_bf, pool_scale.reshape(1, d),
      wb_bf, wo_bf, mod, ln_g.reshape(1, d), ln_b.reshape(1, d))


def _ffn_body(n_chunks, x_ref, mod_ref, w1_ref, w3_ref, w2_ref, g_ref, b_ref, o_ref):
    tiles = D_FF // MXU_DIM
    edges = [MXU_DIM * ((tiles * c) // n_chunks) for c in range(n_chunks + 1)]
    sub = x_ref.shape[0] // ROW_SPLIT
    for s in range(ROW_SPLIT):
        rows = slice(s * sub, (s + 1) * sub)
        x = x_ref[rows, :]
        h = (x * (1.0 + mod_ref[4:5, :]) + mod_ref[3:4, :]).astype(BF)
        acc = jnp.zeros(x.shape, F32)
        for c in range(n_chunks):
            sl = slice(edges[c], edges[c + 1])
            a = jnp.dot(h, w1_ref[:, sl], preferred_element_type=F32)
            bb = jnp.dot(h, w3_ref[:, sl], preferred_element_type=F32)
            gg = (a * jax.nn.sigmoid(a) * bb).astype(BF)
            acc += jnp.dot(gg, w2_ref[sl, :], preferred_element_type=F32)
        r = DN_ALPHA * x + mod_ref[5:6, :] * acc
        o_ref[rows, :] = _layer_norm(r, g_ref[...], b_ref[...])


def _ffn(x, mod, w1_bf, w3_bf, w2_bf, ln_g, ln_b):
    b, l, d = x.shape
    tm = min(l, 512)
    row = lambda bi, i: (bi, i, 0)
    return pl.pallas_call(
        functools.partial(_ffn_body, 2),
        out_shape=jax.ShapeDtypeStruct((b, l, d), F32),
        grid=(b, l // tm),
        in_specs=[pl.BlockSpec((None, tm, d), row),
                  pl.BlockSpec((None, 6, d), lambda bi, i: (bi, 0, 0)),
                  pl.BlockSpec((d, D_FF), lambda bi, i: (0, 0), pipeline_mode=pl.Buffered(1)),
                  pl.BlockSpec((d, D_FF), lambda bi, i: (0, 0), pipeline_mode=pl.Buffered(1)),
                  pl.BlockSpec((D_FF, d), lambda bi, i: (0, 0), pipeline_mode=pl.Buffered(1)),
                  pl.BlockSpec((1, d), lambda bi, i: (0, 0)),
                  pl.BlockSpec((1, d), lambda bi, i: (0, 0))],
        out_specs=pl.BlockSpec((None, tm, d), row),
        compiler_params=_cparams(2),
        name="swiglu_ffn",
    )(x, mod, w1_bf, w3_bf, w2_bf, ln_g.reshape(1, d), ln_b.reshape(1, d))


def _permute_w_in(w):
    return jnp.concatenate([w[:, C_GATE:], w[:, C_HY:C_POOL], w[:, C_Q:C_K], w[:, C_POOL:C_GATE],
                            w[:, C_K:C_V], w[:, C_V:C_HY]], axis=1).astype(BF)


def _v_cols(p):
    return p[..., P_V:P_V + KV_W]


def _k_cols(p):
    return p[..., P_K:P_K + KV_W]


def _keys_transposed(k):
    b, lk, _ = k.shape
    return k.reshape(b, lk, N_KV_HEADS, HEAD_DIM).transpose(0, 2, 3, 1)


def _stream_block(x, p, k_all, v_all, kf, mod, lw):
    attn = _attention(p, _keys_transposed(k_all), v_all)
    z_pair, x0c = _hyena_pre(p, lw['hy_conv_w'], lw['hy_conv_b'])
    y_pair = _fft_conv(z_pair, kf)
    x = _mix(x, p, y_pair, z_pair, x0c, lw['hy_d'], attn, lw['pool_w'], lw['pool_scale'], lw['w_branch'],
             lw['w_out'], mod, lw['ln1_g'], lw['ln1_b'])
    return _ffn(x, mod, lw['ffn_w1'], lw['ffn_w3'], lw['ffn_w2'], lw['ln2_g'], lw['ln2_b'])


def kernel(x, c, ctx, c_ctx, w_ada, b_ada, w_in, q_norm_g, k_norm_g, hy_conv_w, hy_conv_b, hf_w1, hf_b1, hf_freq,
           hf_w2, hf_b2, hf_w3, hy_d, pool_w, pool_scale, w_branch, w_out, ln1_g, ln1_b, ln2_g, ln2_b,
           ffn_w1, ffn_w3, ffn_w2):
    b, l, d = x.shape
    lc = ctx.shape[1]
    depth = w_ada.shape[0]
    assert b % 2 == 0 and l % GRID_W == 0

    rows = 16
    c_all = jnp.zeros((rows, d), F32).at[:b].set(c).at[b].set(c_ctx)
    cos_l, sin_l = _rope_tables(l, True)
    cos_c, sin_c = _rope_tables(lc, False)

    xl, xc = x, ctx
    for li in range(depth):
        last = li == depth - 1
        lw = dict(hy_conv_w=hy_conv_w[li], hy_conv_b=hy_conv_b[li], hy_d=hy_d[li],
                  pool_w=pool_w[li].astype(BF), pool_scale=pool_scale[li],
                  w_branch=w_branch[li].astype(BF), w_out=w_out[li].astype(BF),
                  ln1_g=ln1_g[li], ln1_b=ln1_b[li], ln2_g=ln2_g[li], ln2_b=ln2_b[li],
                  ffn_w1=ffn_w1[li].astype(BF), ffn_w3=ffn_w3[li].astype(BF), ffn_w2=ffn_w2[li].astype(BF))
        w_in_p = _permute_w_in(w_in[li])

        mod = _ada(c_all, w_ada[li], b_ada[li]).reshape(rows, 6, d)
        mod_l = mod[:b]
        mod_c = jnp.broadcast_to(mod[b:b + 1], (b, 6, d))

        filt = (hf_w1[li], hf_b1[li], hf_freq[li], hf_w2[li], hf_b2[li], hf_w3[li])
        kf_l = _kernel_spectrum(l, *_hyena_filter(l, *filt))

        p_l = _in_proj(xl, mod_l, w_in_p, cos_l, sin_l, q_norm_g[li], k_norm_g[li])
        p_c = _in_proj(xc, mod_c, w_in_p, cos_c, sin_c, q_norm_g[li], k_norm_g[li])
        k_c, v_c = _k_cols(p_c), _v_cols(p_c)
        k_all = jnp.concatenate([k_c, _k_cols(p_l)], axis=1)
        v_all = jnp.concatenate([v_c, _v_cols(p_l)], axis=1)

        xl = _stream_block(xl, p_l, k_all, v_all, kf_l, mod_l, lw)
        if not last:
            kf_c = _kernel_spectrum(lc, *_hyena_filter(lc, *filt))
            xc = _stream_block(xc, p_c, k_c, v_c, kf_c, mod_c, lw)
    return xl
```

```python
import functools
import math

import numpy as np
import jax
import jax.numpy as jnp
from jax import lax
from jax.experimental import pallas as pl
from jax.experimental.pallas import tpu as pltpu

F32 = jnp.float32
BF = jnp.bfloat16
HIGHEST = lax.Precision.HIGHEST

D_MODEL = 1024
GRID_W = 64
N_HEADS = 8
N_KV_HEADS = 2
HEAD_DIM = 128
GQA_GROUP = N_HEADS // N_KV_HEADS
ROPE_THETA = 10000.0
Q_W = N_HEADS * HEAD_DIM
KV_W = N_KV_HEADS * HEAD_DIM
HY_WIDTH = D_MODEL
HY_EMB = 33
HY_BANDS = (HY_EMB - 1) // 2
HY_FILTER_HIDDEN = 64
HY_TARGET = 1e-2
HY_MAX_DECAY = math.log(HY_TARGET) / 0.3
HY_MIN_DECAY = math.log(HY_TARGET) / 1.5
HY_SHIFT = 0.05
POOL_WINDOWS = (2, 4, 8, 16)
POOL_GROUP = D_MODEL // len(POOL_WINDOWS)
D_FF = 2816
DEPTH = 2
DN_ALPHA = (2 * DEPTH) ** 0.25
EPS = 1e-6

C_Q = 0
C_K = C_Q + Q_W
C_V = C_K + KV_W
C_HY = C_V + KV_W
C_POOL = C_HY + 3 * HY_WIDTH
C_GATE = C_POOL + D_MODEL
IN_WIDTH = C_GATE + 3 * D_MODEL

P_GATE = 0
P_HY = 3 * D_MODEL
P_Q = P_HY + 3 * HY_WIDTH
P_POOL = P_Q + Q_W
P_K = P_POOL + D_MODEL
P_V = P_K + KV_W

Q_SCALE = HEAD_DIM ** -0.5 * math.log2(math.e)
F8 = jnp.float8_e4m3fn
QK_SHIFT = 4.0

ATT_TQ = 256
ATT_TK = 768
FFT_CT = 256
FFT_UNROLL = 8
KF_UNROLL = 4
ROW_SPLIT = 2

VMEM_LIMIT = 60 * 1024 * 1024
FFT_INNER = 128
LANES = 128
MXU_DIM = 256


def _cparams(n_axes):
    return pltpu.CompilerParams(dimension_semantics=("arbitrary",) * n_axes, vmem_limit_bytes=VMEM_LIMIT)


def _const_spec(shape, n_grid):
    nd = len(shape)
    return pl.BlockSpec(shape, lambda *g, _nd=nd: (0,) * _nd)


def _layer_norm(r, g, b):
    mu = jnp.mean(r, axis=-1, keepdims=True)
    d = r - mu
    var = jnp.mean(d * d, axis=-1, keepdims=True)
    return d * lax.rsqrt(var + EPS) * g + b


def _ada_body(c_ref, w_ref, b_ref, o_ref):
    c = c_ref[...]
    s = c * jax.nn.sigmoid(c)
    o_ref[...] = jnp.dot(s, w_ref[...], preferred_element_type=F32, precision=HIGHEST) + b_ref[...]


def _ada(c_all, w_ada, b_ada):
    rows, d = c_all.shape
    n = w_ada.shape[1]
    tn = 512
    return pl.pallas_call(
        _ada_body,
        out_shape=jax.ShapeDtypeStruct((rows, n), F32),
        grid=(n // tn,),
        in_specs=[pl.BlockSpec((rows, d), lambda j: (0, 0)),
                  pl.BlockSpec((d, tn), lambda j: (0, j)),
                  pl.BlockSpec((1, tn), lambda j: (0, j))],
        out_specs=pl.BlockSpec((rows, tn), lambda j: (0, j)),
        compiler_params=_cparams(1),
        name="ada_mod",
    )(c_all, w_ada, b_ada.reshape(1, n))


def _norm_rope(y, g, cos_t, sin_t, scale):
    ms = jnp.mean(y * y, axis=-1, keepdims=True)
    y = y * lax.rsqrt(ms + EPS) * g
    lane = lax.broadcasted_iota(jnp.int32, y.shape, 1)
    up = pltpu.roll(y, HEAD_DIM - 32, 1)
    dn = pltpu.roll(y, 32, 1)
    partner = jnp.where((lane % 64) < 32, up, dn)
    out = y * cos_t + partner * sin_t
    if scale != 1.0:
        out = out * scale
    return out


def _inproj_body(tn, x_ref, mod_ref, w_ref, cos_ref, sin_ref, gq_ref, gk_ref, o_ref):
    h = (x_ref[...] * (1.0 + mod_ref[1:2, :]) + mod_ref[0:1, :]).astype(BF)
    chunks = list(range(w_ref.shape[1] // tn))
    with_epilogue = [j for j in chunks if j * tn < P_Q + Q_W and (j + 1) * tn > P_Q or j * tn < P_K + KV_W and (j + 1) * tn > P_K]
    for j in with_epilogue + [j for j in chunks if j not in with_epilogue]:
        c0 = j * tn
        r = jnp.dot(h, w_ref[:, c0:c0 + tn], preferred_element_type=F32)
        heads = []
        for hh in range(tn // HEAD_DIM):
            col = c0 + hh * HEAD_DIM
            y = r[:, hh * HEAD_DIM:(hh + 1) * HEAD_DIM]
            if P_Q <= col < P_Q + Q_W:
                y = _norm_rope(y, gq_ref[...], cos_ref[...], sin_ref[...], Q_SCALE * QK_SHIFT)
            elif P_K <= col < P_K + KV_W:
                y = _norm_rope(y, gk_ref[...], cos_ref[...], sin_ref[...], 1.0 / QK_SHIFT)
            heads.append(y)
        o_ref[:, c0:c0 + tn] = jnp.concatenate(heads, axis=1).astype(BF)


def _in_proj(x, mod, w_in_bf, cos_t, sin_t, gq, gk):
    b, l, d = x.shape
    n = w_in_bf.shape[1]
    tm = min(l, 512)
    tn = 512
    vec = pl.BlockSpec((1, HEAD_DIM), lambda bi, i: (0, 0))
    tab = pl.BlockSpec((tm, HEAD_DIM), lambda bi, i: (i, 0))
    return pl.pallas_call(
        functools.partial(_inproj_body, tn),
        out_shape=jax.ShapeDtypeStruct((b, l, n), BF),
        grid=(b, l // tm),
        in_specs=[pl.BlockSpec((None, tm, d), lambda bi, i: (bi, i, 0)),
                  pl.BlockSpec((None, 6, d), lambda bi, i: (bi, 0, 0)),
                  pl.BlockSpec((d, n), lambda bi, i: (0, 0), pipeline_mode=pl.Buffered(1)),
                  tab, tab, vec, vec],
        out_specs=pl.BlockSpec((None, tm, n), lambda bi, i: (bi, i, 0)),
        compiler_params=_cparams(2),
        name="in_proj",
    )(x, mod, w_in_bf, cos_t, sin_t, gq.reshape(1, HEAD_DIM), gk.reshape(1, HEAD_DIM))


def _rope_tables(l, with_positions):
    quarter = HEAD_DIM // 4
    if not with_positions:
        return jnp.ones((l, HEAD_DIM), F32), jnp.zeros((l, HEAD_DIM), F32)
    t = jnp.arange(l)
    rows = (t // GRID_W).astype(F32)
    cols = (t % GRID_W).astype(F32)
    inv = jnp.power(ROPE_THETA, -jnp.arange(quarter, dtype=F32) / quarter)
    ar = rows[:, None] * inv[None, :]
    ac = cols[:, None] * inv[None, :]
    cos_t = jnp.concatenate([jnp.cos(ar), jnp.cos(ar), jnp.cos(ac), jnp.cos(ac)], axis=-1)
    sin_t = jnp.concatenate([-jnp.sin(ar), jnp.sin(ar), -jnp.sin(ac), jnp.sin(ac)], axis=-1)
    return cos_t, sin_t


def _flash_body(tk, q_ref, kt_ref, v_ref, o_ref, qs_scr, sa_scr, sb_scr, mpa_scr, mpb_scr, m_scr, accl_scr):
    tq = q_ref.shape[0]
    nk = kt_ref.shape[1] // tk
    for h in range(GQA_GROUP):
        qs_scr[h * tq:(h + 1) * tq, :] = q_ref[:, h * HEAD_DIM:(h + 1) * HEAD_DIM].astype(F8)
    m_scr[...] = jnp.full(m_scr.shape, -jnp.inf, F32)
    accl_scr[...] = jnp.zeros(accl_scr.shape, F32)

    def scores(i, s_scr, mp_scr):
        c0 = pl.multiple_of(i * tk, tk)
        s = jnp.dot(qs_scr[...], kt_ref[:, pl.ds(c0, tk)].astype(F8), preferred_element_type=F32)
        s_scr[...] = s
        mp = s[:, :HEAD_DIM]
        for t in range(1, tk // HEAD_DIM):
            mp = jnp.maximum(mp, s[:, t * HEAD_DIM:(t + 1) * HEAD_DIM])
        mp_scr[...] = mp

    def consume(i, s_scr, mp_scr):
        r0 = pl.multiple_of(i * tk, tk)
        v = v_ref[pl.ds(r0, tk), :]
        vext = jnp.concatenate([v, jnp.ones_like(v)], axis=1)
        m_prev = m_scr[...]
        m_new = jnp.maximum(m_prev, jnp.max(mp_scr[...], axis=-1, keepdims=True))
        alpha = jnp.exp2(m_prev - m_new)
        p = jnp.concatenate([jnp.exp2(s_scr[:, t * HEAD_DIM:(t + 1) * HEAD_DIM] - m_new)
                             for t in range(tk // HEAD_DIM)], axis=1).astype(BF)
        upd = jnp.dot(p, vext, preferred_element_type=F32)
        accl_scr[...] = jnp.concatenate([alpha, alpha], axis=1) * accl_scr[...] + upd
        m_scr[...] = m_new

    scores(0, sa_scr, mpa_scr)

    def pair(j, carry):
        scores(2 * j + 1, sb_scr, mpb_scr)
        consume(2 * j, sa_scr, mpa_scr)
        scores(2 * j + 2, sa_scr, mpa_scr)
        consume(2 * j + 1, sb_scr, mpb_scr)
        return carry

    lax.fori_loop(0, (nk - 1) // 2, pair, 0, unroll=True)
    if nk % 2 == 1:
        consume(nk - 1, sa_scr, mpa_scr)
    else:
        scores(nk - 1, sb_scr, mpb_scr)
        consume(nk - 2, sa_scr, mpa_scr)
        consume(nk - 1, sb_scr, mpb_scr)
    for h in range(GQA_GROUP):
        rows = slice(h * tq, (h + 1) * tq)
        o_ref[:, h * HEAD_DIM:(h + 1) * HEAD_DIM] = (accl_scr[rows, :HEAD_DIM] / accl_scr[rows, HEAD_DIM:]).astype(BF)


def _attention(p, kt, v):
    b, lq, _ = p.shape
    lk = kt.shape[3]
    tq = min(lq, ATT_TQ)
    tk = ATT_TK if lk % ATT_TK == 0 else 256
    gw = GQA_GROUP * HEAD_DIM
    m = GQA_GROUP * tq
    return pl.pallas_call(
        functools.partial(_flash_body, tk),
        out_shape=jax.ShapeDtypeStruct((b, lq, Q_W), BF),
        grid=(b, N_KV_HEADS, lq // tq),
        in_specs=[pl.BlockSpec((None, tq, gw), lambda bi, g, i: (bi, i, P_Q // gw + g)),
                  pl.BlockSpec((None, None, HEAD_DIM, lk), lambda bi, g, i: (bi, g, 0, 0)),
                  pl.BlockSpec((None, lk, HEAD_DIM), lambda bi, g, i: (bi, 0, g))],
        out_specs=pl.BlockSpec((None, tq, gw), lambda bi, g, i: (bi, i, g)),
        scratch_shapes=[pltpu.VMEM((m, HEAD_DIM), F8),
                        pltpu.VMEM((m, tk), F32),
                        pltpu.VMEM((m, tk), F32),
                        pltpu.VMEM((m, HEAD_DIM), F32),
                        pltpu.VMEM((m, HEAD_DIM), F32),
                        pltpu.VMEM((m, HEAD_DIM), F32),
                        pltpu.VMEM((m, 2 * HEAD_DIM), F32)],
        compiler_params=_cparams(3),
        name="gqa_attention",
    )(p, kt, v)


def _pack(lo, hi):
    ul = lax.bitcast_convert_type(lo.astype(BF).astype(F32), jnp.uint32)
    uh = lax.bitcast_convert_type(hi.astype(BF).astype(F32), jnp.uint32)
    return lax.bitcast_convert_type(uh | (ul >> 16), jnp.int32)


def _unpack_lo(x):
    return lax.bitcast_convert_type(lax.bitcast_convert_type(x, jnp.uint32) << 16, F32)


def _unpack_hi(x):
    return lax.bitcast_convert_type(lax.bitcast_convert_type(x, jnp.uint32) & jnp.uint32(0xFFFF0000), F32)


def _hypre_body(rc, uv_ref, u0_ref, u1_ref, wv_ref, w0_ref, w1_ref, bv_ref, b0_ref, b1_ref, z_ref, x0_ref):
    l = uv_ref.shape[1]
    n_chunks = l // rc

    def conv(u_ref, half, r0, w_ref, b_ref):
        cur = u_ref[half, pl.ds(r0, rc), :].astype(F32)
        p0 = jnp.maximum(r0 - 16, 0)
        n0 = jnp.minimum(r0 + rc, l - 16)
        prev_row = u_ref[half, pl.ds(pl.multiple_of(p0, 16), 16), :].astype(F32)[15:16]
        next_row = u_ref[half, pl.ds(pl.multiple_of(n0, 16), 16), :].astype(F32)[0:1]
        prev_row = jnp.where(r0 > 0, prev_row, 0.0)
        next_row = jnp.where(r0 + rc < l, next_row, 0.0)
        row = lax.broadcasted_iota(jnp.int32, cur.shape, 0)
        x_prev = jnp.where(row == 0, prev_row, pltpu.roll(cur, 1, 0))
        x_next = jnp.where(row == rc - 1, next_row, pltpu.roll(cur, rc - 1, 0))
        return b_ref[...] + x_prev * w_ref[0:1, :] + cur * w_ref[1:2, :] + x_next * w_ref[2:3, :]

    blk, pitch = _seq_pitch(l)

    def chunk(i, carry):
        r0 = pl.multiple_of(i * rc, rc)
        zs = []
        for half in range(2):
            v = conv(uv_ref, half, r0, wv_ref, bv_ref)
            x1 = conv(u1_ref, half, r0, w1_ref, b1_ref)
            x0 = conv(u0_ref, half, r0, w0_ref, b0_ref)
            x0_ref[half, pl.ds(r0, rc), :] = x0.astype(BF)
            zs.append(v * x1)
        packed = _pack(zs[0], zs[1])
        for j in range(rc // blk):
            dst = pl.multiple_of((i * (rc // blk) + j) * pitch, 8)
            z_ref[pl.ds(dst, blk), :] = packed[j * blk:(j + 1) * blk]
            if pitch > blk:
                z_ref[pl.ds(dst + blk, pitch - blk), :] = jnp.zeros((pitch - blk, packed.shape[1]), jnp.int32)
        return carry

    lax.fori_loop(0, n_chunks, chunk, 0)


def _seq_pitch(l):
    n1, n2 = _split(l)
    return (n2, n2 + 8) if n1 > 1 else (l, l)


def _hyena_pre(p, conv_w, conv_b):
    b, l, n = p.shape
    hp = b // 2
    tc = LANES
    rc = min(l, 512)
    blk, pitch = _seq_pitch(l)
    rows_p = l // blk * pitch
    p4 = p.reshape(2, hp, l, n)
    c = HY_WIDTH
    nb = c // tc
    base = P_HY // tc

    def u_spec(g):
        return pl.BlockSpec((2, None, l, tc), lambda pi, j, _g=g: (0, pi, 0, base + _g * nb + j))

    def w_spec(g):
        return pl.BlockSpec((3, tc), lambda pi, j, _g=g: (0, _g * nb + j))

    def b_spec(g):
        return pl.BlockSpec((1, tc), lambda pi, j, _g=g: (0, _g * nb + j))

    cb = conv_b.reshape(1, 3 * c)
    z, x0 = pl.pallas_call(
        functools.partial(_hypre_body, rc),
        out_shape=(jax.ShapeDtypeStruct((hp, nb, rows_p, tc), jnp.int32), jax.ShapeDtypeStruct((2, hp, l, c), BF)),
        grid=(hp, nb),
        in_specs=[u_spec(0), u_spec(1), u_spec(2), w_spec(0), w_spec(1), w_spec(2), b_spec(0), b_spec(1), b_spec(2)],
        out_specs=(pl.BlockSpec((None, None, rows_p, tc), lambda pi, j: (pi, j, 0, 0)),
                   pl.BlockSpec((2, None, l, tc), lambda pi, j: (0, pi, 0, j))),
        compiler_params=_cparams(2),
        name="hyena_pre",
    )(p4, p4, p4, conv_w, conv_w, conv_w, cb, cb, cb)
    return z, x0.reshape(b, l, c)


def _filter_body(l, tr, f_ref, w1_ref, b1_ref, fr_ref, w2_ref, b2_ref, w3_ref, dl_ref, k_ref, ss_ref):
    i = pl.program_id(0)
    feats = f_ref[...]
    freq = fr_ref[...]
    h = jnp.sin(freq * (jnp.dot(feats, w1_ref[...], preferred_element_type=F32, precision=HIGHEST) + b1_ref[...]))
    h = jnp.sin(freq * (jnp.dot(h, w2_ref[...], preferred_element_type=F32, precision=HIGHEST) + b2_ref[...]))
    h = jnp.dot(h, w3_ref[...], preferred_element_type=F32, precision=HIGHEST)
    t01 = feats[:, 0:1]
    window = jnp.exp(-t01 * dl_ref[...]) + HY_SHIFT
    row = i * tr + lax.broadcasted_iota(jnp.int32, h.shape, 0)
    kern = jnp.where(row == l, 0.0, h * window)
    k_ref[...] = kern

    @pl.when(i == 0)
    def _():
        ss_ref[...] = jnp.zeros_like(ss_ref)

    ss_ref[...] += jnp.sum(kern * kern, axis=0, keepdims=True)


def _hyena_filter(l, w1, b1, freq, w2, b2, w3):
    n = 2 * l
    c = HY_WIDTH
    hid = HY_FILTER_HIDDEN
    tr = min(l, 512)
    j = jnp.arange(n)
    lag = jnp.where(j < l, j, n - j).astype(F32)
    t01 = lag / max(l - 1, 1)
    bands = jnp.linspace(1e-4, HY_BANDS - 1, HY_BANDS, dtype=F32)
    ang = (2.0 * math.pi / l) * lag[:, None] * bands[None, :]
    feats = jnp.concatenate([t01[:, None], jnp.cos(ang), -jnp.sin(ang)], axis=-1)
    feats = jnp.pad(feats, ((0, 0), (0, hid - HY_EMB)))
    w1p = jnp.pad(w1, ((0, hid - HY_EMB), (0, 0)))
    deltas = jnp.abs(jnp.linspace(HY_MIN_DECAY, HY_MAX_DECAY, c, dtype=F32)).reshape(1, c)
    nt = l // tr
    return pl.pallas_call(
        functools.partial(_filter_body, l, tr),
        out_shape=(jax.ShapeDtypeStruct((n, c), F32), jax.ShapeDtypeStruct((1, c), F32)),
        grid=(n // tr,),
        in_specs=[pl.BlockSpec((tr, hid), lambda i: (i, 0)),
                  pl.BlockSpec((hid, hid), lambda i: (0, 0)),
                  pl.BlockSpec((1, hid), lambda i: (0, 0)),
                  pl.BlockSpec((1, hid), lambda i: (0, 0)),
                  pl.BlockSpec((hid, hid), lambda i: (0, 0)),
                  pl.BlockSpec((1, hid), lambda i: (0, 0)),
                  pl.BlockSpec((hid, c), lambda i: (0, i // nt)),
                  pl.BlockSpec((1, c), lambda i: (0, 0))],
        out_specs=(pl.BlockSpec((tr, c), lambda i: (i, 0)),
                   pl.BlockSpec((1, c), lambda i: (0, 0))),
        compiler_params=_cparams(1),
        name="hyena_filter",
    )(feats, w1p, b1.reshape(1, hid), freq.reshape(1, hid), w2, b2.reshape(1, hid), w3, deltas)


def _split(l):
    n = 2 * l
    n1 = n // FFT_INNER if n > 4 * FFT_INNER else 1
    return n1, n // n1


def _interleave_cols(a, b):
    r, k = a.shape
    return np.stack([a, b], axis=-1).reshape(r, 2 * k)


@functools.lru_cache(maxsize=None)
def _fft_tables(l):
    n1, n2 = _split(l)
    n = 2 * l
    f64 = np.float64
    t = {}
    n2in = n2 if n1 > 1 else l
    k2 = np.arange(n2, dtype=f64)[:, None]
    t2 = np.arange(n2in, dtype=f64)[None, :]
    ang2 = -2.0 * np.pi * k2 * t2 / n2
    f2re, f2im = np.cos(ang2), np.sin(ang2)
    t['cre2'] = _interleave_cols(f2re, f2re)
    t['cim2'] = _interleave_cols(f2im, f2im)
    k1 = np.arange(n1, dtype=f64)[:, None]
    angw = -2.0 * np.pi * k1 * np.arange(n2in, dtype=f64)[None, :] / n
    wre, wim = np.cos(angw), np.sin(angw)
    t['w4'] = np.stack([_interleave_cols(wre, -wim), _interleave_cols(-wim, -wre),
                        _interleave_cols(wim, wre), _interleave_cols(wre, -wim)], axis=1)
    n2out = n2 if n1 > 1 else l
    tt = np.arange(n2out, dtype=f64)[:, None]
    kk = np.arange(n2, dtype=f64)[None, :]
    angc = -2.0 * np.pi * tt * kk / n2
    cr, ci = np.cos(angc), np.sin(angc)
    t['gc'] = np.block([[cr, ci], [-ci, cr]])
    k2f = np.arange(n2, dtype=f64)[:, None]
    t2f = np.arange(n2, dtype=f64)[None, :]
    angf = -2.0 * np.pi * k2f * t2f / n2
    t['f2re'], t['f2im'] = np.cos(angf), np.sin(angf)
    angwf = -2.0 * np.pi * k1 * np.arange(n2, dtype=f64)[None, :] / n
    t['w2'] = np.stack([np.cos(angwf), np.sin(angwf)], axis=1)
    if n1 > 1:
        h = n1 // 2
        kk1 = np.arange(n1, dtype=f64)[:, None]
        ang1 = -2.0 * np.pi * kk1 * np.arange(h, dtype=f64)[None, :] / n1
        f1re, f1im = np.cos(ang1), np.sin(ang1)
        t['m1'] = np.concatenate([_interleave_cols(f1re, -f1im), _interleave_cols(f1im, f1re)], axis=0)
        d1re, d1im = f1re.T, f1im.T
        t['d1re2'] = _interleave_cols(d1re, d1re)
        t['d1im2'] = _interleave_cols(d1im, d1im)
        wre_t, wim_t = wre.T, wim.T
        t['v4'] = np.stack([_interleave_cols(wre_t, wim_t), _interleave_cols(-wim_t, wre_t),
                            _interleave_cols(-wim_t, wre_t), _interleave_cols(-wre_t, -wim_t)], axis=1)
        ang1f = -2.0 * np.pi * kk1 * np.arange(n1, dtype=f64)[None, :] / n1
        t['m1k'] = np.concatenate([np.cos(ang1f), np.sin(ang1f)], axis=0)
    return {k: np.asarray(v, np.float32) for k, v in t.items()}


def _dot3(a, b):
    ah = a.astype(BF)
    al = (a - ah.astype(F32)).astype(BF)
    bh = b.astype(BF)
    bl = (b - bh.astype(F32)).astype(BF)
    return jnp.dot(jnp.concatenate([ah, ah, al], axis=1), jnp.concatenate([bh, bl, bh], axis=0),
                   preferred_element_type=F32)


def _kfft_body(l, *refs):
    n1, n2 = _split(l)
    n = 2 * l
    if n1 > 1:
        k_ref, ss_ref, m1k_ref, f2re_ref, f2im_ref, w2_ref, o_ref, are_scr, aim_scr = refs
    else:
        k_ref, ss_ref, f2re_ref, f2im_ref, o_ref = refs
    scale = lax.rsqrt(ss_ref[...] + EPS) * (1.0 / n)

    if n1 > 1:
        pitch = are_scr.shape[0] // n1

        def stage1(t2, carry):
            xin = k_ref[pl.ds(t2, n1, stride=n2), :]
            o = _dot3(m1k_ref[...], xin)
            are_scr[pl.ds(t2, n1, stride=pitch), :] = o[:n1]
            aim_scr[pl.ds(t2, n1, stride=pitch), :] = o[n1:]
            return carry

        lax.fori_loop(0, n2, stage1, 0, unroll=KF_UNROLL)

        def slab(k1, carry):
            r0 = pl.multiple_of(k1 * pitch, 8)
            s = jnp.concatenate([are_scr[pl.ds(r0, n2), :], aim_scr[pl.ds(r0, n2), :]], axis=0)
            w2 = w2_ref[k1]
            wre, wim = w2[0:1], w2[1:2]
            hre = f2re_ref[...] * wre - f2im_ref[...] * wim
            him = f2re_ref[...] * wim + f2im_ref[...] * wre
            hblk = jnp.concatenate([jnp.concatenate([hre, -him], axis=1),
                                    jnp.concatenate([him, hre], axis=1)], axis=0)
            x = _dot3(hblk, s)
            o_ref[k1] = _pack(x[:n2] * scale, x[n2:] * scale)
            return carry

        lax.fori_loop(0, n1, slab, 0, unroll=KF_UNROLL)
    else:
        f = jnp.concatenate([f2re_ref[...], f2im_ref[...]], axis=0)
        x = _dot3(f, k_ref[...])
        o_ref[0] = _pack(x[:n2] * scale, x[n2:] * scale)


def _kernel_spectrum(l, kern, sumsq):
    n1, n2 = _split(l)
    n = 2 * l
    c = kern.shape[1]
    ct = 128
    tb = _fft_tables(l)
    names = (['m1k'] if n1 > 1 else []) + ['f2re', 'f2im'] + (['w2'] if n1 > 1 else [])
    consts = [jnp.asarray(tb[k]) for k in names]
    scratch = [pltpu.VMEM((n1 * (n2 + 8), ct), F32)] * 2 if n1 > 1 else []
    return pl.pallas_call(
        functools.partial(_kfft_body, l),
        out_shape=jax.ShapeDtypeStruct((n1, n2, c), jnp.int32),
        grid=(c // ct,),
        in_specs=[pl.BlockSpec((n, ct), lambda j: (0, j), pipeline_mode=pl.Buffered(1)),
                  pl.BlockSpec((1, ct), lambda j: (0, j))] + [_const_spec(a.shape, 1) for a in consts],
        out_specs=pl.BlockSpec((n1, n2, ct), lambda j: (0, 0, j)),
        scratch_shapes=scratch,
        compiler_params=_cparams(1),
        name="hyena_filter_fft",
    )(kern, sumsq, *consts)


def _fft_body(l, *refs):
    n1, n2 = _split(l)
    if n1 > 1:
        (z_ref, kf_ref, cre2_ref, cim2_ref, w4_ref, gc_ref, m1_ref, d1re2_ref, d1im2_ref, v4_ref,
         out_ref, a_scr) = refs
        h = n1 // 2
        ctn = z_ref.shape[0]
        a_pitch = a_scr.shape[1] // n1
        io_pitch = _seq_pitch(l)[1]

        def load(ref, rows):
            return jnp.concatenate([ref[c, rows, :] for c in range(ctn)], axis=1)

        def store(ref, rows, val):
            for c in range(ctn):
                ref[c, rows, :] = val[:, c * LANES:(c + 1) * LANES]

        def stage1(t2, carry):
            xin = load(z_ref, pl.ds(t2, h, stride=io_pitch))
            xb = pltpu.bitcast(xin, BF)
            o = jnp.dot(m1_ref[...], xb, preferred_element_type=F32)
            store(a_scr, pl.ds(t2, n1, stride=a_pitch), _pack(o[:n1], o[n1:]))
            return carry

        lax.fori_loop(0, n2, stage1, 0, unroll=FFT_UNROLL)

        def slab(k1, carry):
            r0 = pl.multiple_of(k1 * a_pitch, 8)
            s = pltpu.bitcast(load(a_scr, pl.ds(r0, n2)), BF)
            w4 = w4_ref[k1]
            cre2 = cre2_ref[...]
            cim2 = cim2_ref[...]
            top = cre2 * w4[0:1] + cim2 * w4[1:2]
            bot = cre2 * w4[2:3] + cim2 * w4[3:4]
            hblk = jnp.concatenate([top, bot], axis=0).astype(BF)
            x = jnp.dot(hblk, s, preferred_element_type=F32)
            kf = kf_ref[k1]
            kre, kim = _unpack_lo(kf), _unpack_hi(kf)
            xre, xim = x[:n2], x[n2:]
            y = jnp.concatenate([xre * kre - xim * kim, xre * kim + xim * kre], axis=0).astype(BF)
            b = jnp.dot(gc_ref[...], y, preferred_element_type=F32)
            store(a_scr, pl.ds(r0, n2), _pack(b[:n2], b[n2:]))
            return carry

        lax.fori_loop(0, n1, slab, 0, unroll=FFT_UNROLL)

        def stage3(t2, carry):
            bin_ = pltpu.bitcast(load(a_scr, pl.ds(t2, n1, stride=a_pitch)), BF)
            v4 = v4_ref[t2]
            d1re2 = d1re2_ref[...]
            d1im2 = d1im2_ref[...]
            top = d1re2 * v4[0:1] + d1im2 * v4[1:2]
            bot = d1re2 * v4[2:3] + d1im2 * v4[3:4]
            m3 = jnp.concatenate([top, bot], axis=0).astype(BF)
            o = jnp.dot(m3, bin_, preferred_element_type=F32)
            store(out_ref, pl.ds(t2, h, stride=io_pitch), _pack(o[:h], o[h:]))
            return carry

        lax.fori_loop(0, n2, stage3, 0, unroll=FFT_UNROLL)
        for r in range(n2, io_pitch):
            store(out_ref, pl.ds(r, h, stride=io_pitch), jnp.zeros((h, ctn * LANES), jnp.int32))
    else:
        z_ref, kf_ref, cre2_ref, cim2_ref, w4_ref, gc_ref, out_ref = refs
        ctn = z_ref.shape[0]
        s = pltpu.bitcast(jnp.concatenate([z_ref[c] for c in range(ctn)], axis=1), BF)
        w4 = w4_ref[0]
        top = cre2_ref[...] * w4[0:1] + cim2_ref[...] * w4[1:2]
        bot = cre2_ref[...] * w4[2:3] + cim2_ref[...] * w4[3:4]
        hblk = jnp.concatenate([top, bot], axis=0).astype(BF)
        x = jnp.dot(hblk, s, preferred_element_type=F32)
        kf = kf_ref[0]
        kre, kim = _unpack_lo(kf), _unpack_hi(kf)
        xre, xim = x[:n2], x[n2:]
        y = jnp.concatenate([xre * kre - xim * kim, xre * kim + xim * kre], axis=0).astype(BF)
        b = jnp.dot(gc_ref[...], y, preferred_element_type=F32)
        packed = _pack(b[:l], b[l:])
        for c in range(ctn):
            out_ref[c] = packed[:, c * LANES:(c + 1) * LANES]


def _fft_conv(zp, kf):
    hp, nct, rows_p, _ = zp.shape
    n1, n2 = kf.shape[0], kf.shape[1]
    l = n1 * n2 // 2
    assert rows_p == l // _seq_pitch(l)[0] * _seq_pitch(l)[1]
    ct = FFT_CT
    ctn = ct // LANES
    tb = _fft_tables(l)
    names = ['cre2', 'cim2', 'w4', 'gc'] + (['m1', 'd1re2', 'd1im2', 'v4'] if n1 > 1 else [])
    consts = [jnp.asarray(tb[k]).astype(BF) if k in ('gc', 'm1') else jnp.asarray(tb[k]) for k in names]
    scratch = [pltpu.VMEM((ctn, n1 * (n2 + 8), LANES), jnp.int32)] if n1 > 1 else []
    one = dict(pipeline_mode=pl.Buffered(1))
    return pl.pallas_call(
        functools.partial(_fft_body, l),
        out_shape=jax.ShapeDtypeStruct((hp, nct, rows_p, LANES), jnp.int32),
        grid=(nct // ctn, hp),
        in_specs=[pl.BlockSpec((None, ctn, rows_p, LANES), lambda j, p: (p, j, 0, 0), **one),
                  pl.BlockSpec((n1, n2, ct), lambda j, p: (0, 0, j), **one)] + [_const_spec(a.shape, 2) for a in consts],
        out_specs=pl.BlockSpec((None, ctn, rows_p, LANES), lambda j, p: (p, j, 0, 0), **one),
        scratch_shapes=scratch,
        compiler_params=_cparams(2),
        name="hyena_fft_conv",
    )(zp, kf, *consts)


POOL_HALO = 64


@functools.lru_cache(maxsize=None)
def _pool_bands(rows):
    off = np.arange(rows + 2 * POOL_HALO)[None, :] - POOL_HALO - np.arange(rows)[:, None]
    bands = []
    for w in POOL_WINDOWS:
        before = w // 2
        after = w - 1 - before
        bands.append(((off >= -before) & (off <= after)).astype(np.float32))
    return np.stack(bands)


def _pool_rows(ext, cur, t0, l, band_ref, pw_ref, ps_ref):
    rows = cur.shape[0]
    outs = []
    for g, w in enumerate(POOL_WINDOWS):
        before = w // 2
        after = w - 1 - before
        cols = slice(g * POOL_GROUP, (g + 1) * POOL_GROUP)
        sums = jnp.dot(band_ref[g], ext[:, cols], preferred_element_type=F32)
        t = t0 + lax.broadcasted_iota(jnp.int32, (rows, POOL_GROUP), 0)
        cnt = jnp.minimum(t + after + 1, l) - jnp.maximum(t - before, 0)
        m = sums / cnt.astype(F32) - cur[:, cols].astype(F32)
        outs.append(jnp.dot(m.astype(BF), pw_ref[g], preferred_element_type=F32) * ps_ref[:, cols])
    return jnp.concatenate(outs, axis=1).astype(BF)


def _mix_body(hp, blk, pitch, l, x_ref, gate_ref, y_ref, z_ref, x0_ref, hd_ref, at_ref, u_ref, up_ref, un_ref,
              band_ref, pw_ref, ps_ref, wb_ref, wo_ref, mod_ref, g_ref, b_ref, o_ref):
    hi = pl.program_id(0) >= hp
    i = pl.program_id(1)
    nct = y_ref.shape[0]
    d = D_MODEL
    tm = x_ref.shape[0]
    sub = tm // ROW_SPLIT
    prev = jnp.where(i > 0, up_ref[...], jnp.zeros(up_ref.shape, BF))
    nxt = jnp.where((i + 1) * tm < l, un_ref[...], jnp.zeros(un_ref.shape, BF))
    ext_all = jnp.concatenate([prev, u_ref[...], nxt], axis=0)

    def signal_rows(ref, start, size):
        if pitch == blk:
            return jnp.concatenate([ref[c, start:start + size, :] for c in range(nct)], axis=1)
        assert start % blk == 0 and size % blk == 0
        return jnp.concatenate(
            [jnp.concatenate([ref[c, j * pitch:j * pitch + blk, :] for c in range(nct)], axis=1)
             for j in range(start // blk, (start + size) // blk)], axis=0)

    for s in range(ROW_SPLIT):
        rows = slice(s * sub, (s + 1) * sub)
        yw = signal_rows(y_ref, s * sub, sub)
        zw = signal_rows(z_ref, s * sub, sub)
        y = jnp.where(hi, _unpack_hi(yw), _unpack_lo(yw))
        z = jnp.where(hi, _unpack_hi(zw), _unpack_lo(zw))
        hy = ((y + z * hd_ref[...]) * x0_ref[rows, :].astype(F32)).astype(BF)
        merged = jax.nn.sigmoid(gate_ref[rows, 0:d].astype(F32)) * jnp.dot(hy, wb_ref[0], preferred_element_type=F32)
        merged += jax.nn.sigmoid(gate_ref[rows, d:2 * d].astype(F32)) * jnp.dot(at_ref[rows, :], wb_ref[1],
                                                                                preferred_element_type=F32)
        pooled = _pool_rows(ext_all[s * sub:(s + 1) * sub + 2 * POOL_HALO], u_ref[rows, :], i * tm + s * sub, l,
                            band_ref, pw_ref, ps_ref)
        merged += jax.nn.sigmoid(gate_ref[rows, 2 * d:].astype(F32)) * jnp.dot(pooled, wb_ref[2],
                                                                               preferred_element_type=F32)
        out = jnp.dot(merged.astype(BF), wo_ref[...], preferred_element_type=F32)
        r = DN_ALPHA * x_ref[rows, :] + mod_ref[2:3, :] * out
        o_ref[rows, :] = _layer_norm(r, g_ref[...], b_ref[...])


def _mix(x, p, y_pair, z_pair, x0c, hy_d, attn, pool_w_bf, pool_scale, wb_bf, wo_bf, mod, ln_g, ln_b):
    b, l, d = x.shape
    hp = b // 2
    tm = min(l, 512)
    row = lambda bi, i: (bi, i, 0)
    pair = lambda bi, i: (bi % hp, 0, i, 0)
    nct = d // LANES
    blk, pitch = _seq_pitch(l)
    tm_p = tm // blk * pitch
    ng = len(POOL_WINDOWS)
    hb = tm // POOL_HALO
    n_hb = l // POOL_HALO
    pool_col = P_POOL // d
    bands = jnp.asarray(_pool_bands(tm // ROW_SPLIT), BF)
    return pl.pallas_call(
        functools.partial(_mix_body, hp, blk, pitch, l),
        out_shape=jax.ShapeDtypeStruct((b, l, d), F32),
        grid=(b, l // tm),
        in_specs=[pl.BlockSpec((None, tm, d), row),
                  pl.BlockSpec((None, tm, 3 * d), lambda bi, i: (bi, i, P_GATE // (3 * d))),
                  pl.BlockSpec((None, nct, tm_p, LANES), pair),
                  pl.BlockSpec((None, nct, tm_p, LANES), pair),
                  pl.BlockSpec((None, tm, d), row),
                  pl.BlockSpec((1, d), lambda bi, i: (0, 0)),
                  pl.BlockSpec((None, tm, d), row),
                  pl.BlockSpec((None, tm, d), lambda bi, i: (bi, i, pool_col)),
                  pl.BlockSpec((None, POOL_HALO, d), lambda bi, i: (bi, jnp.maximum(i * hb - 1, 0), pool_col)),
                  pl.BlockSpec((None, POOL_HALO, d), lambda bi, i: (bi, jnp.minimum((i + 1) * hb, n_hb - 1), pool_col)),
                  pl.BlockSpec(bands.shape, lambda bi, i: (0, 0, 0)),
                  pl.BlockSpec((ng, POOL_GROUP, POOL_GROUP), lambda bi, i: (0, 0, 0)),
                  pl.BlockSpec((1, d), lambda bi, i: (0, 0)),
                  pl.BlockSpec((3, d, d), lambda bi, i: (0, 0, 0), pipeline_mode=pl.Buffered(1)),
                  pl.BlockSpec((d, d), lambda bi, i: (0, 0), pipeline_mode=pl.Buffered(1)),
                  pl.BlockSpec((None, 6, d), lambda bi, i: (bi, 0, 0)),
                  pl.BlockSpec((1, d), lambda bi, i: (0, 0)),
                  pl.BlockSpec((1, d), lambda bi, i: (0, 0))],
        out_specs=pl.BlockSpec((None, tm, d), row),
        compiler_params=_cparams(2),
        name="branch_mix",
    )(x, p, y_pair, z_pair, x0c, hy_d.reshape(1, d), attn, p, p, p, bands, pool_w_bf, pool_scale.reshape(1, d),
      wb_bf, wo_bf, mod, ln_g.reshape(1, d), ln_b.reshape(1, d))


def _ffn_body(n_chunks, x_ref, mod_ref, w1_ref, w3_ref, w2_ref, g_ref, b_ref, o_ref):
    tiles = D_FF // MXU_DIM
    edges = [MXU_DIM * ((tiles * c) // n_chunks) for c in range(n_chunks + 1)]
    sub = x_ref.shape[0] // ROW_SPLIT
    for s in range(ROW_SPLIT):
        rows = slice(s * sub, (s + 1) * sub)
        x = x_ref[rows, :]
        h = (x * (1.0 + mod_ref[4:5, :]) + mod_ref[3:4, :]).astype(BF)
        acc = jnp.zeros(x.shape, F32)
        for c in range(n_chunks):
            sl = slice(edges[c], edges[c + 1])
            a = jnp.dot(h, w1_ref[:, sl], preferred_element_type=F32)
            bb = jnp.dot(h, w3_ref[:, sl], preferred_element_type=F32)
            gg = (a * jax.nn.sigmoid(a) * bb).astype(BF)
            acc += jnp.dot(gg, w2_ref[sl, :], preferred_element_type=F32)
        r = DN_ALPHA * x + mod_ref[5:6, :] * acc
        o_ref[rows, :] = _layer_norm(r, g_ref[...], b_ref[...])


def _ffn(x, mod, w1_bf, w3_bf, w2_bf, ln_g, ln_b):
    b, l, d = x.shape
    tm = min(l, 512)
    row = lambda bi, i: (bi, i, 0)
    return pl.pallas_call(
        functools.partial(_ffn_body, 2),
        out_shape=jax.ShapeDtypeStruct((b, l, d), F32),
        grid=(b, l // tm),
        in_specs=[pl.BlockSpec((None, tm, d), row),
                  pl.BlockSpec((None, 6, d), lambda bi, i: (bi, 0, 0)),
                  pl.BlockSpec((d, D_FF), lambda bi, i: (0, 0), pipeline_mode=pl.Buffered(1)),
                  pl.BlockSpec((d, D_FF), lambda bi, i: (0, 0), pipeline_mode=pl.Buffered(1)),
                  pl.BlockSpec((D_FF, d), lambda bi, i: (0, 0), pipeline_mode=pl.Buffered(1)),
                  pl.BlockSpec((1, d), lambda bi, i: (0, 0)),
                  pl.BlockSpec((1, d), lambda bi, i: (0, 0))],
        out_specs=pl.BlockSpec((None, tm, d), row),
        compiler_params=_cparams(2),
        name="swiglu_ffn",
    )(x, mod, w1_bf, w3_bf, w2_bf, ln_g.reshape(1, d), ln_b.reshape(1, d))


def _permute_w_in(w):
    return jnp.concatenate([w[:, C_GATE:], w[:, C_HY:C_POOL], w[:, C_Q:C_K], w[:, C_POOL:C_GATE],
                            w[:, C_K:C_V], w[:, C_V:C_HY]], axis=1).astype(BF)


def _v_cols(p):
    return p[..., P_V:P_V + KV_W]


def _k_cols(p):
    return p[..., P_K:P_K + KV_W]


def _keys_transposed(k):
    b, lk, _ = k.shape
    return k.reshape(b, lk, N_KV_HEADS, HEAD_DIM).transpose(0, 2, 3, 1)


def _stream_block(x, p, k_all, v_all, kf, mod, lw):
    attn = _attention(p, _keys_transposed(k_all), v_all)
    z_pair, x0c = _hyena_pre(p, lw['hy_conv_w'], lw['hy_conv_b'])
    y_pair = _fft_conv(z_pair, kf)
    x = _mix(x, p, y_pair, z_pair, x0c, lw['hy_d'], attn, lw['pool_w'], lw['pool_scale'], lw['w_branch'],
             lw['w_out'], mod, lw['ln1_g'], lw['ln1_b'])
    return _ffn(x, mod, lw['ffn_w1'], lw['ffn_w3'], lw['ffn_w2'], lw['ln2_g'], lw['ln2_b'])


def kernel(x, c, ctx, c_ctx, w_ada, b_ada, w_in, q_norm_g, k_norm_g, hy_conv_w, hy_conv_b, hf_w1, hf_b1, hf_freq,
           hf_w2, hf_b2, hf_w3, hy_d, pool_w, pool_scale, w_branch, w_out, ln1_g, ln1_b, ln2_g, ln2_b,
           ffn_w1, ffn_w3, ffn_w2):
    b, l, d = x.shape
    lc = ctx.shape[1]
    depth = w_ada.shape[0]
    assert b % 2 == 0 and l % GRID_W == 0

    rows = 16
    c_all = jnp.zeros((rows, d), F32).at[:b].set(c).at[b].set(c_ctx)
    cos_l, sin_l = _rope_tables(l, True)
    cos_c, sin_c = _rope_tables(lc, False)

    xl, xc = x, ctx
    for li in range(depth):
        last = li == depth - 1
        lw = dict(hy_conv_w=hy_conv_w[li], hy_conv_b=hy_conv_b[li], hy_d=hy_d[li],
                  pool_w=pool_w[li].astype(BF), pool_scale=pool_scale[li],
                  w_branch=w_branch[li].astype(BF), w_out=w_out[li].astype(BF),
                  ln1_g=ln1_g[li], ln1_b=ln1_b[li], ln2_g=ln2_g[li], ln2_b=ln2_b[li],
                  ffn_w1=ffn_w1[li].astype(BF), ffn_w3=ffn_w3[li].astype(BF), ffn_w2=ffn_w2[li].astype(BF))
        w_in_p = _permute_w_in(w_in[li])

        mod = _ada(c_all, w_ada[li], b_ada[li]).reshape(rows, 6, d)
        mod_l = mod[:b]
        mod_c = jnp.broadcast_to(mod[b:b + 1], (b, 6, d))

        filt = (hf_w1[li], hf_b1[li], hf_freq[li], hf_w2[li], hf_b2[li], hf_w3[li])
        kf_l = _kernel_spectrum(l, *_hyena_filter(l, *filt))

        p_l = _in_proj(xl, mod_l, w_in_p, cos_l, sin_l, q_norm_g[li], k_norm_g[li])
        p_c = _in_proj(xc, mod_c, w_in_p, cos_c, sin_c, q_norm_g[li], k_norm_g[li])
        k_c, v_c = _k_cols(p_c), _v_cols(p_c)
        k_all = jnp.concatenate([k_c, _k_cols(p_l)], axis=1)
        v_all = jnp.concatenate([v_c, _v_cols(p_l)], axis=1)

        xl = _stream_block(xl, p_l, k_all, v_all, kf_l, mod_l, lw)
        if not last:
            kf_c = _kernel_spectrum(lc, *_hyena_filter(lc, *filt))
            xc = _stream_block(xc, p_c, k_c, v_c, kf_c, mod_c, lw)
    return xl
```

```python
import functools
import math

import numpy as np
import jax
import jax.numpy as jnp
from jax import lax
from jax.experimental import pallas as pl
from jax.experimental.pallas import tpu as pltpu

F32 = jnp.float32
BF = jnp.bfloat16
HIGHEST = lax.Precision.HIGHEST

D_MODEL = 1024
GRID_W = 64
N_HEADS = 8
N_KV_HEADS = 2
HEAD_DIM = 128
GQA_GROUP = N_HEADS // N_KV_HEADS
ROPE_THETA = 10000.0
Q_W = N_HEADS * HEAD_DIM
KV_W = N_KV_HEADS * HEAD_DIM
HY_WIDTH = D_MODEL
HY_EMB = 33
HY_BANDS = (HY_EMB - 1) // 2
HY_FILTER_HIDDEN = 64
HY_TARGET = 1e-2
HY_MAX_DECAY = math.log(HY_TARGET) / 0.3
HY_MIN_DECAY = math.log(HY_TARGET) / 1.5
HY_SHIFT = 0.05
POOL_WINDOWS = (2, 4, 8, 16)
POOL_GROUP = D_MODEL // len(POOL_WINDOWS)
D_FF = 2816
DEPTH = 2
DN_ALPHA = (2 * DEPTH) ** 0.25
EPS = 1e-6

C_Q = 0
C_K = C_Q + Q_W
C_V = C_K + KV_W
C_HY = C_V + KV_W
C_POOL = C_HY + 3 * HY_WIDTH
C_GATE = C_POOL + D_MODEL
IN_WIDTH = C_GATE + 3 * D_MODEL

P_GATE = 0
P_HY = 3 * D_MODEL
P_Q = P_HY + 3 * HY_WIDTH
P_POOL = P_Q + Q_W
P_K = P_POOL + D_MODEL
P_V = P_K + KV_W

Q_SCALE = HEAD_DIM ** -0.5 * math.log2(math.e)
F8 = jnp.float8_e4m3fn
QK_SHIFT = 4.0

ATT_TQ = 256
ATT_TK = 768
FFT_CT = 256
FFT_UNROLL = 8
KF_UNROLL = 8
ROW_SPLIT = 2

VMEM_LIMIT = 60 * 1024 * 1024
FFT_INNER = 128
LANES = 128
MXU_DIM = 256


def _cparams(n_axes):
    return pltpu.CompilerParams(dimension_semantics=("arbitrary",) * n_axes, vmem_limit_bytes=VMEM_LIMIT)


def _const_spec(shape, n_grid):
    nd = len(shape)
    return pl.BlockSpec(shape, lambda *g, _nd=nd: (0,) * _nd)


def _layer_norm(r, g, b):
    mu = jnp.mean(r, axis=-1, keepdims=True)
    d = r - mu
    var = jnp.mean(d * d, axis=-1, keepdims=True)
    return d * lax.rsqrt(var + EPS) * g + b


def _ada_body(c_ref, w_ref, b_ref, o_ref):
    c = c_ref[...]
    s = c * jax.nn.sigmoid(c)
    o_ref[...] = jnp.dot(s, w_ref[...], preferred_element_type=F32, precision=HIGHEST) + b_ref[...]


def _ada(c_all, w_ada, b_ada):
    rows, d = c_all.shape
    n = w_ada.shape[1]
    tn = 512
    return pl.pallas_call(
        _ada_body,
        out_shape=jax.ShapeDtypeStruct((rows, n), F32),
        grid=(n // tn,),
        in_specs=[pl.BlockSpec((rows, d), lambda j: (0, 0)),
                  pl.BlockSpec((d, tn), lambda j: (0, j)),
                  pl.BlockSpec((1, tn), lambda j: (0, j))],
        out_specs=pl.BlockSpec((rows, tn), lambda j: (0, j)),
        compiler_params=_cparams(1),
        name="ada_mod",
    )(c_all, w_ada, b_ada.reshape(1, n))


def _norm_rope(y, g, cos_t, sin_t, scale):
    ms = jnp.mean(y * y, axis=-1, keepdims=True)
    y = y * lax.rsqrt(ms + EPS) * g
    lane = lax.broadcasted_iota(jnp.int32, y.shape, 1)
    up = pltpu.roll(y, HEAD_DIM - 32, 1)
    dn = pltpu.roll(y, 32, 1)
    partner = jnp.where((lane % 64) < 32, up, dn)
    out = y * cos_t + partner * sin_t
    if scale != 1.0:
        out = out * scale
    return out


def _inproj_body(tn, x_ref, mod_ref, w_ref, cos_ref, sin_ref, gq_ref, gk_ref, o_ref):
    h = (x_ref[...] * (1.0 + mod_ref[1:2, :]) + mod_ref[0:1, :]).astype(BF)
    chunks = list(range(w_ref.shape[1] // tn))
    with_epilogue = [j for j in chunks if j * tn < P_Q + Q_W and (j + 1) * tn > P_Q or j * tn < P_K + KV_W and (j + 1) * tn > P_K]
    for j in with_epilogue + [j for j in chunks if j not in with_epilogue]:
        c0 = j * tn
        r = jnp.dot(h, w_ref[:, c0:c0 + tn], preferred_element_type=F32)
        heads = []
        for hh in range(tn // HEAD_DIM):
            col = c0 + hh * HEAD_DIM
            y = r[:, hh * HEAD_DIM:(hh + 1) * HEAD_DIM]
            if P_Q <= col < P_Q + Q_W:
                y = _norm_rope(y, gq_ref[...], cos_ref[...], sin_ref[...], Q_SCALE * QK_SHIFT)
            elif P_K <= col < P_K + KV_W:
                y = _norm_rope(y, gk_ref[...], cos_ref[...], sin_ref[...], 1.0 / QK_SHIFT)
            heads.append(y)
        o_ref[:, c0:c0 + tn] = jnp.concatenate(heads, axis=1).astype(BF)


def _in_proj(x, mod, w_in_bf, cos_t, sin_t, gq, gk):
    b, l, d = x.shape
    n = w_in_bf.shape[1]
    tm = min(l, 512)
    tn = 512
    vec = pl.BlockSpec((1, HEAD_DIM), lambda bi, i: (0, 0))
    tab = pl.BlockSpec((tm, HEAD_DIM), lambda bi, i: (i, 0))
    return pl.pallas_call(
        functools.partial(_inproj_body, tn),
        out_shape=jax.ShapeDtypeStruct((b, l, n), BF),
        grid=(b, l // tm),
        in_specs=[pl.BlockSpec((None, tm, d), lambda bi, i: (bi, i, 0)),
                  pl.BlockSpec((None, 6, d), lambda bi, i: (bi, 0, 0)),
                  pl.BlockSpec((d, n), lambda bi, i: (0, 0), pipeline_mode=pl.Buffered(1)),
                  tab, tab, vec, vec],
        out_specs=pl.BlockSpec((None, tm, n), lambda bi, i: (bi, i, 0)),
        compiler_params=_cparams(2),
        name="in_proj",
    )(x, mod, w_in_bf, cos_t, sin_t, gq.reshape(1, HEAD_DIM), gk.reshape(1, HEAD_DIM))


def _rope_tables(l, with_positions):
    quarter = HEAD_DIM // 4
    if not with_positions:
        return jnp.ones((l, HEAD_DIM), F32), jnp.zeros((l, HEAD_DIM), F32)
    t = jnp.arange(l)
    rows = (t // GRID_W).astype(F32)
    cols = (t % GRID_W).astype(F32)
    inv = jnp.power(ROPE_THETA, -jnp.arange(quarter, dtype=F32) / quarter)
    ar = rows[:, None] * inv[None, :]
    ac = cols[:, None] * inv[None, :]
    cos_t = jnp.concatenate([jnp.cos(ar), jnp.cos(ar), jnp.cos(ac), jnp.cos(ac)], axis=-1)
    sin_t = jnp.concatenate([-jnp.sin(ar), jnp.sin(ar), -jnp.sin(ac), jnp.sin(ac)], axis=-1)
    return cos_t, sin_t


def _flash_body(tk, q_ref, kt_ref, v_ref, o_ref, qs_scr, sa_scr, sb_scr, mpa_scr, mpb_scr, m_scr, accl_scr):
    tq = q_ref.shape[0]
    nk = kt_ref.shape[1] // tk
    for h in range(GQA_GROUP):
        qs_scr[h * tq:(h + 1) * tq, :] = q_ref[:, h * HEAD_DIM:(h + 1) * HEAD_DIM].astype(F8)
    m_scr[...] = jnp.full(m_scr.shape, -jnp.inf, F32)
    accl_scr[...] = jnp.zeros(accl_scr.shape, F32)

    def scores(i, s_scr, mp_scr):
        c0 = pl.multiple_of(i * tk, tk)
        s = jnp.dot(qs_scr[...], kt_ref[:, pl.ds(c0, tk)].astype(F8), preferred_element_type=F32)
        s_scr[...] = s
        mp = s[:, :HEAD_DIM]
        for t in range(1, tk // HEAD_DIM):
            mp = jnp.maximum(mp, s[:, t * HEAD_DIM:(t + 1) * HEAD_DIM])
        mp_scr[...] = mp

    def consume(i, s_scr, mp_scr):
        r0 = pl.multiple_of(i * tk, tk)
        v = v_ref[pl.ds(r0, tk), :]
        vext = jnp.concatenate([v, jnp.ones_like(v)], axis=1)
        m_prev = m_scr[...]
        m_new = jnp.maximum(m_prev, jnp.max(mp_scr[...], axis=-1, keepdims=True))
        alpha = jnp.exp2(m_prev - m_new)
        p = jnp.concatenate([jnp.exp2(s_scr[:, t * HEAD_DIM:(t + 1) * HEAD_DIM] - m_new)
                             for t in range(tk // HEAD_DIM)], axis=1).astype(BF)
        upd = jnp.dot(p, vext, preferred_element_type=F32)
        accl_scr[...] = jnp.concatenate([alpha, alpha], axis=1) * accl_scr[...] + upd
        m_scr[...] = m_new

    scores(0, sa_scr, mpa_scr)

    def pair(j, carry):
        scores(2 * j + 1, sb_scr, mpb_scr)
        consume(2 * j, sa_scr, mpa_scr)
        scores(2 * j + 2, sa_scr, mpa_scr)
        consume(2 * j + 1, sb_scr, mpb_scr)
        return carry

    lax.fori_loop(0, (nk - 1) // 2, pair, 0, unroll=True)
    if nk % 2 == 1:
        consume(nk - 1, sa_scr, mpa_scr)
    else:
        scores(nk - 1, sb_scr, mpb_scr)
        consume(nk - 2, sa_scr, mpa_scr)
        consume(nk - 1, sb_scr, mpb_scr)
    for h in range(GQA_GROUP):
        rows = slice(h * tq, (h + 1) * tq)
        o_ref[:, h * HEAD_DIM:(h + 1) * HEAD_DIM] = (accl_scr[rows, :HEAD_DIM] / accl_scr[rows, HEAD_DIM:]).astype(BF)


def _attention(p, kt, v):
    b, lq, _ = p.shape
    lk = kt.shape[3]
    tq = min(lq, ATT_TQ)
    tk = ATT_TK if lk % ATT_TK == 0 else 256
    gw = GQA_GROUP * HEAD_DIM
    m = GQA_GROUP * tq
    return pl.pallas_call(
        functools.partial(_flash_body, tk),
        out_shape=jax.ShapeDtypeStruct((b, lq, Q_W), BF),
        grid=(b, N_KV_HEADS, lq // tq),
        in_specs=[pl.BlockSpec((None, tq, gw), lambda bi, g, i: (bi, i, P_Q // gw + g)),
                  pl.BlockSpec((None, None, HEAD_DIM, lk), lambda bi, g, i: (bi, g, 0, 0)),
                  pl.BlockSpec((None, lk, HEAD_DIM), lambda bi, g, i: (bi, 0, g))],
        out_specs=pl.BlockSpec((None, tq, gw), lambda bi, g, i: (bi, i, g)),
        scratch_shapes=[pltpu.VMEM((m, HEAD_DIM), F8),
                        pltpu.VMEM((m, tk), F32),
                        pltpu.VMEM((m, tk), F32),
                        pltpu.VMEM((m, HEAD_DIM), F32),
                        pltpu.VMEM((m, HEAD_DIM), F32),
                        pltpu.VMEM((m, HEAD_DIM), F32),
                        pltpu.VMEM((m, 2 * HEAD_DIM), F32)],
        compiler_params=_cparams(3),
        name="gqa_attention",
    )(p, kt, v)


def _pack(lo, hi):
    ul = lax.bitcast_convert_type(lo.astype(BF).astype(F32), jnp.uint32)
    uh = lax.bitcast_convert_type(hi.astype(BF).astype(F32), jnp.uint32)
    return lax.bitcast_convert_type(uh | (ul >> 16), jnp.int32)


def _unpack_lo(x):
    return lax.bitcast_convert_type(lax.bitcast_convert_type(x, jnp.uint32) << 16, F32)


def _unpack_hi(x):
    return lax.bitcast_convert_type(lax.bitcast_convert_type(x, jnp.uint32) & jnp.uint32(0xFFFF0000), F32)


def _hypre_body(rc, uv_ref, u0_ref, u1_ref, wv_ref, w0_ref, w1_ref, bv_ref, b0_ref, b1_ref, z_ref, x0_ref):
    l = uv_ref.shape[1]
    n_chunks = l // rc

    def conv(u_ref, half, r0, w_ref, b_ref):
        cur = u_ref[half, pl.ds(r0, rc), :].astype(F32)
        p0 = jnp.maximum(r0 - 16, 0)
        n0 = jnp.minimum(r0 + rc, l - 16)
        prev_row = u_ref[half, pl.ds(pl.multiple_of(p0, 16), 16), :].astype(F32)[15:16]
        next_row = u_ref[half, pl.ds(pl.multiple_of(n0, 16), 16), :].astype(F32)[0:1]
        prev_row = jnp.where(r0 > 0, prev_row, 0.0)
        next_row = jnp.where(r0 + rc < l, next_row, 0.0)
        row = lax.broadcasted_iota(jnp.int32, cur.shape, 0)
        x_prev = jnp.where(row == 0, prev_row, pltpu.roll(cur, 1, 0))
        x_next = jnp.where(row == rc - 1, next_row, pltpu.roll(cur, rc - 1, 0))
        return b_ref[...] + x_prev * w_ref[0:1, :] + cur * w_ref[1:2, :] + x_next * w_ref[2:3, :]

    blk, pitch = _seq_pitch(l)

    def chunk(i, carry):
        r0 = pl.multiple_of(i * rc, rc)
        zs = []
        for half in range(2):
            v = conv(uv_ref, half, r0, wv_ref, bv_ref)
            x1 = conv(u1_ref, half, r0, w1_ref, b1_ref)
            x0 = conv(u0_ref, half, r0, w0_ref, b0_ref)
            x0_ref[half, pl.ds(r0, rc), :] = x0.astype(BF)
            zs.append(v * x1)
        packed = _pack(zs[0], zs[1])
        for j in range(rc // blk):
            dst = pl.multiple_of((i * (rc // blk) + j) * pitch, 8)
            z_ref[pl.ds(dst, blk), :] = packed[j * blk:(j + 1) * blk]
            if pitch > blk:
                z_ref[pl.ds(dst + blk, pitch - blk), :] = jnp.zeros((pitch - blk, packed.shape[1]), jnp.int32)
        return carry

    lax.fori_loop(0, n_chunks, chunk, 0)


def _seq_pitch(l):
    n1, n2 = _split(l)
    return (n2, n2 + 8) if n1 > 1 else (l, l)


def _hyena_pre(p, conv_w, conv_b):
    b, l, n = p.shape
    hp = b // 2
    tc = LANES
    rc = min(l, 512)
    blk, pitch = _seq_pitch(l)
    rows_p = l // blk * pitch
    p4 = p.reshape(2, hp, l, n)
    c = HY_WIDTH
    nb = c // tc
    base = P_HY // tc

    def u_spec(g):
        return pl.BlockSpec((2, None, l, tc), lambda pi, j, _g=g: (0, pi, 0, base + _g * nb + j))

    def w_spec(g):
        return pl.BlockSpec((3, tc), lambda pi, j, _g=g: (0, _g * nb + j))

    def b_spec(g):
        return pl.BlockSpec((1, tc), lambda pi, j, _g=g: (0, _g * nb + j))

    cb = conv_b.reshape(1, 3 * c)
    z, x0 = pl.pallas_call(
        functools.partial(_hypre_body, rc),
        out_shape=(jax.ShapeDtypeStruct((hp, nb, rows_p, tc), jnp.int32), jax.ShapeDtypeStruct((2, hp, l, c), BF)),
        grid=(hp, nb),
        in_specs=[u_spec(0), u_spec(1), u_spec(2), w_spec(0), w_spec(1), w_spec(2), b_spec(0), b_spec(1), b_spec(2)],
        out_specs=(pl.BlockSpec((None, None, rows_p, tc), lambda pi, j: (pi, j, 0, 0)),
                   pl.BlockSpec((2, None, l, tc), lambda pi, j: (0, pi, 0, j))),
        compiler_params=_cparams(2),
        name="hyena_pre",
    )(p4, p4, p4, conv_w, conv_w, conv_w, cb, cb, cb)
    return z, x0.reshape(b, l, c)


def _filter_body(l, tr, f_ref, w1_ref, b1_ref, fr_ref, w2_ref, b2_ref, w3_ref, dl_ref, k_ref, ss_ref):
    i = pl.program_id(0)
    feats = f_ref[...]
    freq = fr_ref[...]
    h = jnp.sin(freq * (jnp.dot(feats, w1_ref[...], preferred_element_type=F32, precision=HIGHEST) + b1_ref[...]))
    h = jnp.sin(freq * (jnp.dot(h, w2_ref[...], preferred_element_type=F32, precision=HIGHEST) + b2_ref[...]))
    h = jnp.dot(h, w3_ref[...], preferred_element_type=F32, precision=HIGHEST)
    t01 = feats[:, 0:1]
    window = jnp.exp(-t01 * dl_ref[...]) + HY_SHIFT
    row = i * tr + lax.broadcasted_iota(jnp.int32, h.shape, 0)
    kern = jnp.where(row == l, 0.0, h * window)
    k_ref[...] = kern

    @pl.when(i == 0)
    def _():
        ss_ref[...] = jnp.zeros_like(ss_ref)

    ss_ref[...] += jnp.sum(kern * kern, axis=0, keepdims=True)


def _hyena_filter(l, w1, b1, freq, w2, b2, w3):
    n = 2 * l
    c = HY_WIDTH
    hid = HY_FILTER_HIDDEN
    tr = min(l, 512)
    j = jnp.arange(n)
    lag = jnp.where(j < l, j, n - j).astype(F32)
    t01 = lag / max(l - 1, 1)
    bands = jnp.linspace(1e-4, HY_BANDS - 1, HY_BANDS, dtype=F32)
    ang = (2.0 * math.pi / l) * lag[:, None] * bands[None, :]
    feats = jnp.concatenate([t01[:, None], jnp.cos(ang), -jnp.sin(ang)], axis=-1)
    feats = jnp.pad(feats, ((0, 0), (0, hid - HY_EMB)))
    w1p = jnp.pad(w1, ((0, hid - HY_EMB), (0, 0)))
    deltas = jnp.abs(jnp.linspace(HY_MIN_DECAY, HY_MAX_DECAY, c, dtype=F32)).reshape(1, c)
    nt = l // tr
    return pl.pallas_call(
        functools.partial(_filter_body, l, tr),
        out_shape=(jax.ShapeDtypeStruct((n, c), F32), jax.ShapeDtypeStruct((1, c), F32)),
        grid=(n // tr,),
        in_specs=[pl.BlockSpec((tr, hid), lambda i: (i, 0)),
                  pl.BlockSpec((hid, hid), lambda i: (0, 0)),
                  pl.BlockSpec((1, hid), lambda i: (0, 0)),
                  pl.BlockSpec((1, hid), lambda i: (0, 0)),
                  pl.BlockSpec((hid, hid), lambda i: (0, 0)),
                  pl.BlockSpec((1, hid), lambda i: (0, 0)),
                  pl.BlockSpec((hid, c), lambda i: (0, i // nt)),
                  pl.BlockSpec((1, c), lambda i: (0, 0))],
        out_specs=(pl.BlockSpec((tr, c), lambda i: (i, 0)),
                   pl.BlockSpec((1, c), lambda i: (0, 0))),
        compiler_params=_cparams(1),
        name="hyena_filter",
    )(feats, w1p, b1.reshape(1, hid), freq.reshape(1, hid), w2, b2.reshape(1, hid), w3, deltas)


def _split(l):
    n = 2 * l
    n1 = n // FFT_INNER if n > 4 * FFT_INNER else 1
    return n1, n // n1


def _interleave_cols(a, b):
    r, k = a.shape
    return np.stack([a, b], axis=-1).reshape(r, 2 * k)


@functools.lru_cache(maxsize=None)
def _fft_tables(l):
    n1, n2 = _split(l)
    n = 2 * l
    f64 = np.float64
    t = {}
    n2in = n2 if n1 > 1 else l
    k2 = np.arange(n2, dtype=f64)[:, None]
    t2 = np.arange(n2in, dtype=f64)[None, :]
    ang2 = -2.0 * np.pi * k2 * t2 / n2
    f2re, f2im = np.cos(ang2), np.sin(ang2)
    t['cre2'] = _interleave_cols(f2re, f2re)
    t['cim2'] = _interleave_cols(f2im, f2im)
    k1 = np.arange(n1, dtype=f64)[:, None]
    angw = -2.0 * np.pi * k1 * np.arange(n2in, dtype=f64)[None, :] / n
    wre, wim = np.cos(angw), np.sin(angw)
    t['w4'] = np.stack([_interleave_cols(wre, -wim), _interleave_cols(-wim, -wre),
                        _interleave_cols(wim, wre), _interleave_cols(wre, -wim)], axis=1)
    n2out = n2 if n1 > 1 else l
    tt = np.arange(n2out, dtype=f64)[:, None]
    kk = np.arange(n2, dtype=f64)[None, :]
    angc = -2.0 * np.pi * tt * kk / n2
    cr, ci = np.cos(angc), np.sin(angc)
    t['gc'] = np.block([[cr, ci], [-ci, cr]])
    k2f = np.arange(n2, dtype=f64)[:, None]
    t2f = np.arange(n2, dtype=f64)[None, :]
    angf = -2.0 * np.pi * k2f * t2f / n2
    t['f2re'], t['f2im'] = np.cos(angf), np.sin(angf)
    angwf = -2.0 * np.pi * k1 * np.arange(n2, dtype=f64)[None, :] / n
    t['w2'] = np.stack([np.cos(angwf), np.sin(angwf)], axis=1)
    if n1 > 1:
        h = n1 // 2
        kk1 = np.arange(n1, dtype=f64)[:, None]
        ang1 = -2.0 * np.pi * kk1 * np.arange(h, dtype=f64)[None, :] / n1
        f1re, f1im = np.cos(ang1), np.sin(ang1)
        t['m1'] = np.concatenate([_interleave_cols(f1re, -f1im), _interleave_cols(f1im, f1re)], axis=0)
        d1re, d1im = f1re.T, f1im.T
        t['d1re2'] = _interleave_cols(d1re, d1re)
        t['d1im2'] = _interleave_cols(d1im, d1im)
        wre_t, wim_t = wre.T, wim.T
        t['v4'] = np.stack([_interleave_cols(wre_t, wim_t), _interleave_cols(-wim_t, wre_t),
                            _interleave_cols(-wim_t, wre_t), _interleave_cols(-wre_t, -wim_t)], axis=1)
        ang1f = -2.0 * np.pi * kk1 * np.arange(n1, dtype=f64)[None, :] / n1
        t['m1k'] = np.concatenate([np.cos(ang1f), np.sin(ang1f)], axis=0)
    return {k: np.asarray(v, np.float32) for k, v in t.items()}


def _dot3(a, b):
    ah = a.astype(BF)
    al = (a - ah.astype(F32)).astype(BF)
    bh = b.astype(BF)
    bl = (b - bh.astype(F32)).astype(BF)
    return jnp.dot(jnp.concatenate([ah, ah, al], axis=1), jnp.concatenate([bh, bl, bh], axis=0),
                   preferred_element_type=F32)


def _kfft_body(l, *refs):
    n1, n2 = _split(l)
    n = 2 * l
    if n1 > 1:
        k_ref, ss_ref, m1k_ref, f2re_ref, f2im_ref, w2_ref, o_ref, are_scr, aim_scr = refs
    else:
        k_ref, ss_ref, f2re_ref, f2im_ref, o_ref = refs
    scale = lax.rsqrt(ss_ref[...] + EPS) * (1.0 / n)

    if n1 > 1:
        pitch = are_scr.shape[0] // n1

        def stage1(t2, carry):
            xin = k_ref[pl.ds(t2, n1, stride=n2), :]
            o = _dot3(m1k_ref[...], xin)
            are_scr[pl.ds(t2, n1, stride=pitch), :] = o[:n1]
            aim_scr[pl.ds(t2, n1, stride=pitch), :] = o[n1:]
            return carry

        lax.fori_loop(0, n2, stage1, 0, unroll=KF_UNROLL)

        def slab(k1, carry):
            r0 = pl.multiple_of(k1 * pitch, 8)
            s = jnp.concatenate([are_scr[pl.ds(r0, n2), :], aim_scr[pl.ds(r0, n2), :]], axis=0)
            w2 = w2_ref[k1]
            wre, wim = w2[0:1], w2[1:2]
            hre = f2re_ref[...] * wre - f2im_ref[...] * wim
            him = f2re_ref[...] * wim + f2im_ref[...] * wre
            hblk = jnp.concatenate([jnp.concatenate([hre, -him], axis=1),
                                    jnp.concatenate([him, hre], axis=1)], axis=0)
            x = _dot3(hblk, s)
            o_ref[k1] = _pack(x[:n2] * scale, x[n2:] * scale)
            return carry

        lax.fori_loop(0, n1, slab, 0, unroll=KF_UNROLL)
    else:
        f = jnp.concatenate([f2re_ref[...], f2im_ref[...]], axis=0)
        x = _dot3(f, k_ref[...])
        o_ref[0] = _pack(x[:n2] * scale, x[n2:] * scale)


def _kernel_spectrum(l, kern, sumsq):
    n1, n2 = _split(l)
    n = 2 * l
    c = kern.shape[1]
    ct = 128
    tb = _fft_tables(l)
    names = (['m1k'] if n1 > 1 else []) + ['f2re', 'f2im'] + (['w2'] if n1 > 1 else [])
    consts = [jnp.asarray(tb[k]) for k in names]
    scratch = [pltpu.VMEM((n1 * (n2 + 8), ct), F32)] * 2 if n1 > 1 else []
    return pl.pallas_call(
        functools.partial(_kfft_body, l),
        out_shape=jax.ShapeDtypeStruct((n1, n2, c), jnp.int32),
        grid=(c // ct,),
        in_specs=[pl.BlockSpec((n, ct), lambda j: (0, j), pipeline_mode=pl.Buffered(1)),
                  pl.BlockSpec((1, ct), lambda j: (0, j))] + [_const_spec(a.shape, 1) for a in consts],
        out_specs=pl.BlockSpec((n1, n2, ct), lambda j: (0, 0, j)),
        scratch_shapes=scratch,
        compiler_params=_cparams(1),
        name="hyena_filter_fft",
    )(kern, sumsq, *consts)


def _fft_body(l, *refs):
    n1, n2 = _split(l)
    if n1 > 1:
        (z_ref, kf_ref, cre2_ref, cim2_ref, w4_ref, gc_ref, m1_ref, d1re2_ref, d1im2_ref, v4_ref,
         out_ref, a_scr) = refs
        h = n1 // 2
        ctn = z_ref.shape[0]
        a_pitch = a_scr.shape[1] // n1
        io_pitch = _seq_pitch(l)[1]

        def load(ref, rows):
            return jnp.concatenate([ref[c, rows, :] for c in range(ctn)], axis=1)

        def store(ref, rows, val):
            for c in range(ctn):
                ref[c, rows, :] = val[:, c * LANES:(c + 1) * LANES]

        def stage1(t2, carry):
            xin = load(z_ref, pl.ds(t2, h, stride=io_pitch))
            xb = pltpu.bitcast(xin, BF)
            o = jnp.dot(m1_ref[...], xb, preferred_element_type=F32)
            store(a_scr, pl.ds(t2, n1, stride=a_pitch), _pack(o[:n1], o[n1:]))
            return carry

        lax.fori_loop(0, n2, stage1, 0, unroll=FFT_UNROLL)

        def slab(k1, carry):
            r0 = pl.multiple_of(k1 * a_pitch, 8)
            s = pltpu.bitcast(load(a_scr, pl.ds(r0, n2)), BF)
            w4 = w4_ref[k1]
            cre2 = cre2_ref[...]
            cim2 = cim2_ref[...]
            top = cre2 * w4[0:1] + cim2 * w4[1:2]
            bot = cre2 * w4[2:3] + cim2 * w4[3:4]
            hblk = jnp.concatenate([top, bot], axis=0).astype(BF)
            x = jnp.dot(hblk, s, preferred_element_type=F32)
            kf = kf_ref[k1]
            kre, kim = _unpack_lo(kf), _unpack_hi(kf)
            xre, xim = x[:n2], x[n2:]
            y = jnp.concatenate([xre * kre - xim * kim, xre * kim + xim * kre], axis=0).astype(BF)
            b = jnp.dot(gc_ref[...], y, preferred_element_type=F32)
            store(a_scr, pl.ds(r0, n2), _pack(b[:n2], b[n2:]))
            return carry

        lax.fori_loop(0, n1, slab, 0, unroll=FFT_UNROLL)

        def stage3(t2, carry):
            bin_ = pltpu.bitcast(load(a_scr, pl.ds(t2, n1, stride=a_pitch)), BF)
            v4 = v4_ref[t2]
            d1re2 = d1re2_ref[...]
            d1im2 = d1im2_ref[...]
            top = d1re2 * v4[0:1] + d1im2 * v4[1:2]
            bot = d1re2 * v4[2:3] + d1im2 * v4[3:4]
            m3 = jnp.concatenate([top, bot], axis=0).astype(BF)
            o = jnp.dot(m3, bin_, preferred_element_type=F32)
            store(out_ref, pl.ds(t2, h, stride=io_pitch), _pack(o[:h], o[h:]))
            return carry

        lax.fori_loop(0, n2, stage3, 0, unroll=FFT_UNROLL)
        for r in range(n2, io_pitch):
            store(out_ref, pl.ds(r, h, stride=io_pitch), jnp.zeros((h, ctn * LANES), jnp.int32))
    else:
        z_ref, kf_ref, cre2_ref, cim2_ref, w4_ref, gc_ref, out_ref = refs
        ctn = z_ref.shape[0]
        s = pltpu.bitcast(jnp.concatenate([z_ref[c] for c in range(ctn)], axis=1), BF)
        w4 = w4_ref[0]
        top = cre2_ref[...] * w4[0:1] + cim2_ref[...] * w4[1:2]
        bot = cre2_ref[...] * w4[2:3] + cim2_ref[...] * w4[3:4]
        hblk = jnp.concatenate([top, bot], axis=0).astype(BF)
        x = jnp.dot(hblk, s, preferred_element_type=F32)
        kf = kf_ref[0]
        kre, kim = _unpack_lo(kf), _unpack_hi(kf)
        xre, xim = x[:n2], x[n2:]
        y = jnp.concatenate([xre * kre - xim * kim, xre * kim + xim * kre], axis=0).astype(BF)
        b = jnp.dot(gc_ref[...], y, preferred_element_type=F32)
        packed = _pack(b[:l], b[l:])
        for c in range(ctn):
            out_ref[c] = packed[:, c * LANES:(c + 1) * LANES]


def _fft_conv(zp, kf):
    hp, nct, rows_p, _ = zp.shape
    n1, n2 = kf.shape[0], kf.shape[1]
    l = n1 * n2 // 2
    assert rows_p == l // _seq_pitch(l)[0] * _seq_pitch(l)[1]
    ct = FFT_CT
    ctn = ct // LANES
    tb = _fft_tables(l)
    names = ['cre2', 'cim2', 'w4', 'gc'] + (['m1', 'd1re2', 'd1im2', 'v4'] if n1 > 1 else [])
    consts = [jnp.asarray(tb[k]).astype(BF) if k in ('gc', 'm1') else jnp.asarray(tb[k]) for k in names]
    scratch = [pltpu.VMEM((ctn, n1 * (n2 + 8), LANES), jnp.int32)] if n1 > 1 else []
    one = dict(pipeline_mode=pl.Buffered(1))
    return pl.pallas_call(
        functools.partial(_fft_body, l),
        out_shape=jax.ShapeDtypeStruct((hp, nct, rows_p, LANES), jnp.int32),
        grid=(nct // ctn, hp),
        in_specs=[pl.BlockSpec((None, ctn, rows_p, LANES), lambda j, p: (p, j, 0, 0), **one),
                  pl.BlockSpec((n1, n2, ct), lambda j, p: (0, 0, j), **one)] + [_const_spec(a.shape, 2) for a in consts],
        out_specs=pl.BlockSpec((None, ctn, rows_p, LANES), lambda j, p: (p, j, 0, 0), **one),
        scratch_shapes=scratch,
        compiler_params=_cparams(2),
        name="hyena_fft_conv",
    )(zp, kf, *consts)


POOL_HALO = 64


@functools.lru_cache(maxsize=None)
def _pool_bands(rows):
    off = np.arange(rows + 2 * POOL_HALO)[None, :] - POOL_HALO - np.arange(rows)[:, None]
    bands = []
    for w in POOL_WINDOWS:
        before = w // 2
        after = w - 1 - before
        bands.append(((off >= -before) & (off <= after)).astype(np.float32))
    return np.stack(bands)


def _pool_rows(ext, cur, t0, l, band_ref, pw_ref, ps_ref):
    rows = cur.shape[0]
    outs = []
    for g, w in enumerate(POOL_WINDOWS):
        before = w // 2
        after = w - 1 - before
        cols = slice(g * POOL_GROUP, (g + 1) * POOL_GROUP)
        sums = jnp.dot(band_ref[g], ext[:, cols], preferred_element_type=F32)
        t = t0 + lax.broadcasted_iota(jnp.int32, (rows, POOL_GROUP), 0)
        cnt = jnp.minimum(t + after + 1, l) - jnp.maximum(t - before, 0)
        m = sums / cnt.astype(F32) - cur[:, cols].astype(F32)
        outs.append(jnp.dot(m.astype(BF), pw_ref[g], preferred_element_type=F32) * ps_ref[:, cols])
    return jnp.concatenate(outs, axis=1).astype(BF)


def _mix_body(hp, blk, pitch, l, x_ref, gate_ref, y_ref, z_ref, x0_ref, hd_ref, at_ref, u_ref, up_ref, un_ref,
              band_ref, pw_ref, ps_ref, wb_ref, wo_ref, mod_ref, g_ref, b_ref, o_ref):
    hi = pl.program_id(0) >= hp
    i = pl.program_id(1)
    nct = y_ref.shape[0]
    d = D_MODEL
    tm = x_ref.shape[0]
    sub = tm // ROW_SPLIT
    prev = jnp.where(i > 0, up_ref[...], jnp.zeros(up_ref.shape, BF))
    nxt = jnp.where((i + 1) * tm < l, un_ref[...], jnp.zeros(un_ref.shape, BF))
    ext_all = jnp.concatenate([prev, u_ref[...], nxt], axis=0)

    def signal_rows(ref, start, size):
        if pitch == blk:
            return jnp.concatenate([ref[c, start:start + size, :] for c in range(nct)], axis=1)
        assert start % blk == 0 and size % blk == 0
        return jnp.concatenate(
            [jnp.concatenate([ref[c, j * pitch:j * pitch + blk, :] for c in range(nct)], axis=1)
             for j in range(start // blk, (start + size) // blk)], axis=0)

    for s in range(ROW_SPLIT):
        rows = slice(s * sub, (s + 1) * sub)
        yw = signal_rows(y_ref, s * sub, sub)
        zw = signal_rows(z_ref, s * sub, sub)
        y = jnp.where(hi, _unpack_hi(yw), _unpack_lo(yw))
        z = jnp.where(hi, _unpack_hi(zw), _unpack_lo(zw))
        hy = ((y + z * hd_ref[...]) * x0_ref[rows, :].astype(F32)).astype(BF)
        merged = jax.nn.sigmoid(gate_ref[rows, 0:d].astype(F32)) * jnp.dot(hy, wb_ref[0], preferred_element_type=F32)
        merged += jax.nn.sigmoid(gate_ref[rows, d:2 * d].astype(F32)) * jnp.dot(at_ref[rows, :], wb_ref[1],
                                                                                preferred_element_type=F32)
        pooled = _pool_rows(ext_all[s * sub:(s + 1) * sub + 2 * POOL_HALO], u_ref[rows, :], i * tm + s * sub, l,
                            band_ref, pw_ref, ps_ref)
        merged += jax.nn.sigmoid(gate_ref[rows, 2 * d:].astype(F32)) * jnp.dot(pooled, wb_ref[2],
                                                                               preferred_element_type=F32)
        out = jnp.dot(merged.astype(BF), wo_ref[...], preferred_element_type=F32)
        r = DN_ALPHA * x_ref[rows, :] + mod_ref[2:3, :] * out
        o_ref[rows, :] = _layer_norm(r, g_ref[...], b_ref[...])


def _mix(x, p, y_pair, z_pair, x0c, hy_d, attn, pool_w_bf, pool_scale, wb_bf, wo_bf, mod, ln_g, ln_b):
    b, l, d = x.shape
    hp = b // 2
    tm = min(l, 512)
    row = lambda bi, i: (bi, i, 0)
    pair = lambda bi, i: (bi % hp, 0, i, 0)
    nct = d // LANES
    blk, pitch = _seq_pitch(l)
    tm_p = tm // blk * pitch
    ng = len(POOL_WINDOWS)
    hb = tm // POOL_HALO
    n_hb = l // POOL_HALO
    pool_col = P_POOL // d
    bands = jnp.asarray(_pool_bands(tm // ROW_SPLIT), BF)
    return pl.pallas_call(
        functools.partial(_mix_body, hp, blk, pitch, l),
        out_shape=jax.ShapeDtypeStruct((b, l, d), F32),
        grid=(b, l // tm),
        in_specs=[pl.BlockSpec((None, tm, d), row),
                  pl.BlockSpec((None, tm, 3 * d), lambda bi, i: (bi, i, P_GATE // (3 * d))),
                  pl.BlockSpec((None, nct, tm_p, LANES), pair),
                  pl.BlockSpec((None, nct, tm_p, LANES), pair),
                  pl.BlockSpec((None, tm, d), row),
                  pl.BlockSpec((1, d), lambda bi, i: (0, 0)),
                  pl.BlockSpec((None, tm, d), row),
                  pl.BlockSpec((None, tm, d), lambda bi, i: (bi, i, pool_col)),
                  pl.BlockSpec((None, POOL_HALO, d), lambda bi, i: (bi, jnp.maximum(i * hb - 1, 0), pool_col)),
                  pl.BlockSpec((None, POOL_HALO, d), lambda bi, i: (bi, jnp.minimum((i + 1) * hb, n_hb - 1), pool_col)),
                  pl.BlockSpec(bands.shape, lambda bi, i: (0, 0, 0)),
                  pl.BlockSpec((ng, POOL_GROUP, POOL_GROUP), lambda bi, i: (0, 0, 0)),
                  pl.BlockSpec((1, d), lambda bi, i: (0, 0)),
                  pl.BlockSpec((3, d, d), lambda bi, i: (0, 0, 0), pipeline_mode=pl.Buffered(1)),
                  pl.BlockSpec((d, d), lambda bi, i: (0, 0), pipeline_mode=pl.Buffered(1)),
                  pl.BlockSpec((None, 6, d), lambda bi, i: (bi, 0, 0)),
                  pl.BlockSpec((1, d), lambda bi, i: (0, 0)),
                  pl.BlockSpec((1, d), lambda bi, i: (0, 0))],
        out_specs=pl.BlockSpec((None, tm, d), row),
        compiler_params=_cparams(2),
        name="branch_mix",
    )(x, p, y_pair, z_pair, x0c, hy_d.reshape(1, d), attn, p, p, p, bands, pool_w_bf, pool_scale.reshape(1, d),
      wb_bf, wo_bf, mod, ln_g.reshape(1, d), ln_b.reshape(1, d))


def _ffn_body(n_chunks, x_ref, mod_ref, w1_ref, w3_ref, w2_ref, g_ref, b_ref, o_ref):
    tiles = D_FF // MXU_DIM
    edges = [MXU_DIM * ((tiles * c) // n_chunks) for c in range(n_chunks + 1)]
    sub = x_ref.shape[0] // ROW_SPLIT
    for s in range(ROW_SPLIT):
        rows = slice(s * sub, (s + 1) * sub)
        x = x_ref[rows, :]
        h = (x * (1.0 + mod_ref[4:5, :]) + mod_ref[3:4, :]).astype(BF)
        acc = jnp.zeros(x.shape, F32)
        for c in range(n_chunks):
            sl = slice(edges[c], edges[c + 1])
            a = jnp.dot(h, w1_ref[:, sl], preferred_element_type=F32)
            bb = jnp.dot(h, w3_ref[:, sl], preferred_element_type=F32)
            gg = (a * jax.nn.sigmoid(a) * bb).astype(BF)
            acc += jnp.dot(gg, w2_ref[sl, :], preferred_element_type=F32)
        r = DN_ALPHA * x + mod_ref[5:6, :] * acc
        o_ref[rows, :] = _layer_norm(r, g_ref[...], b_ref[...])


def _ffn(x, mod, w1_bf, w3_bf, w2_bf, ln_g, ln_b):
    b, l, d = x.shape
    tm = min(l, 512)
    row = lambda bi, i: (bi, i, 0)
    return pl.pallas_call(
        functools.partial(_ffn_body, 2),
        out_shape=jax.ShapeDtypeStruct((b, l, d), F32),
        grid=(b, l // tm),
        in_specs=[pl.BlockSpec((None, tm, d), row),
                  pl.BlockSpec((None, 6, d), lambda bi, i: (bi, 0, 0)),
                  pl.BlockSpec((d, D_FF), lambda bi, i: (0, 0), pipeline_mode=pl.Buffered(1)),
                  pl.BlockSpec((d, D_FF), lambda bi, i: (0, 0), pipeline_mode=pl.Buffered(1)),
                  pl.BlockSpec((D_FF, d), lambda bi, i: (0, 0), pipeline_mode=pl.Buffered(1)),
                  pl.BlockSpec((1, d), lambda bi, i: (0, 0)),
                  pl.BlockSpec((1, d), lambda bi, i: (0, 0))],
        out_specs=pl.BlockSpec((None, tm, d), row),
        compiler_params=_cparams(2),
        name="swiglu_ffn",
    )(x, mod, w1_bf, w3_bf, w2_bf, ln_g.reshape(1, d), ln_b.reshape(1, d))


def _permute_w_in(w):
    return jnp.concatenate([w[:, C_GATE:], w[:, C_HY:C_POOL], w[:, C_Q:C_K], w[:, C_POOL:C_GATE],
                            w[:, C_K:C_V], w[:, C_V:C_HY]], axis=1).astype(BF)


def _v_cols(p):
    return p[..., P_V:P_V + KV_W]


def _k_cols(p):
    return p[..., P_K:P_K + KV_W]


def _keys_transposed(k):
    b, lk, _ = k.shape
    return k.reshape(b, lk, N_KV_HEADS, HEAD_DIM).transpose(0, 2, 3, 1)


def _stream_block(x, p, k_all, v_all, kf, mod, lw):
    attn = _attention(p, _keys_transposed(k_all), v_all)
    z_pair, x0c = _hyena_pre(p, lw['hy_conv_w'], lw['hy_conv_b'])
    y_pair = _fft_conv(z_pair, kf)
    x = _mix(x, p, y_pair, z_pair, x0c, lw['hy_d'], attn, lw['pool_w'], lw['pool_scale'], lw['w_branch'],
             lw['w_out'], mod, lw['ln1_g'], lw['ln1_b'])
    return _ffn(x, mod, lw['ffn_w1'], lw['ffn_w3'], lw['ffn_w2'], lw['ln2_g'], lw['ln2_b'])


def kernel(x, c, ctx, c_ctx, w_ada, b_ada, w_in, q_norm_g, k_norm_g, hy_conv_w, hy_conv_b, hf_w1, hf_b1, hf_freq,
           hf_w2, hf_b2, hf_w3, hy_d, pool_w, pool_scale, w_branch, w_out, ln1_g, ln1_b, ln2_g, ln2_b,
           ffn_w1, ffn_w3, ffn_w2):
    b, l, d = x.shape
    lc = ctx.shape[1]
    depth = w_ada.shape[0]
    assert b % 2 == 0 and l % GRID_W == 0

    rows = 16
    c_all = jnp.zeros((rows, d), F32).at[:b].set(c).at[b].set(c_ctx)
    cos_l, sin_l = _rope_tables(l, True)
    cos_c, sin_c = _rope_tables(lc, False)

    xl, xc = x, ctx
    for li in range(depth):
        last = li == depth - 1
        lw = dict(hy_conv_w=hy_conv_w[li], hy_conv_b=hy_conv_b[li], hy_d=hy_d[li],
                  pool_w=pool_w[li].astype(BF), pool_scale=pool_scale[li],
                  w_branch=w_branch[li].astype(BF), w_out=w_out[li].astype(BF),
                  ln1_g=ln1_g[li], ln1_b=ln1_b[li], ln2_g=ln2_g[li], ln2_b=ln2_b[li],
                  ffn_w1=ffn_w1[li].astype(BF), ffn_w3=ffn_w3[li].astype(BF), ffn_w2=ffn_w2[li].astype(BF))
        w_in_p = _permute_w_in(w_in[li])

        mod = _ada(c_all, w_ada[li], b_ada[li]).reshape(rows, 6, d)
        mod_l = mod[:b]
        mod_c = jnp.broadcast_to(mod[b:b + 1], (b, 6, d))

        filt = (hf_w1[li], hf_b1[li], hf_freq[li], hf_w2[li], hf_b2[li], hf_w3[li])
        kf_l = _kernel_spectrum(l, *_hyena_filter(l, *filt))

        p_l = _in_proj(xl, mod_l, w_in_p, cos_l, sin_l, q_norm_g[li], k_norm_g[li])
        p_c = _in_proj(xc, mod_c, w_in_p, cos_c, sin_c, q_norm_g[li], k_norm_g[li])
        k_c, v_c = _k_cols(p_c), _v_cols(p_c)
        k_all = jnp.concatenate([k_c, _k_cols(p_l)], axis=1)
        v_all = jnp.concatenate([v_c, _v_cols(p_l)], axis=1)

        xl = _stream_block(xl, p_l, k_all, v_all, kf_l, mod_l, lw)
        if not last:
            kf_c = _kernel_spectrum(lc, *_hyena_filter(lc, *filt))
            xc = _stream_block(xc, p_c, k_c, v_c, kf_c, mod_c, lw)
    return xl
```

```python
import functools
import math

import numpy as np
import jax
import jax.numpy as jnp
from jax import lax
from jax.experimental import pallas as pl
from jax.experimental.pallas import tpu as pltpu

F32 = jnp.float32
BF = jnp.bfloat16
HIGHEST = lax.Precision.HIGHEST

D_MODEL = 1024
GRID_W = 64
N_HEADS = 8
N_KV_HEADS = 2
HEAD_DIM = 128
GQA_GROUP = N_HEADS // N_KV_HEADS
ROPE_THETA = 10000.0
Q_W = N_HEADS * HEAD_DIM
KV_W = N_KV_HEADS * HEAD_DIM
HY_WIDTH = D_MODEL
HY_EMB = 33
HY_BANDS = (HY_EMB - 1) // 2
HY_FILTER_HIDDEN = 64
HY_TARGET = 1e-2
HY_MAX_DECAY = math.log(HY_TARGET) / 0.3
HY_MIN_DECAY = math.log(HY_TARGET) / 1.5
HY_SHIFT = 0.05
POOL_WINDOWS = (2, 4, 8, 16)
POOL_GROUP = D_MODEL // len(POOL_WINDOWS)
D_FF = 2816
DEPTH = 2
DN_ALPHA = (2 * DEPTH) ** 0.25
EPS = 1e-6

C_Q = 0
C_K = C_Q + Q_W
C_V = C_K + KV_W
C_HY = C_V + KV_W
C_POOL = C_HY + 3 * HY_WIDTH
C_GATE = C_POOL + D_MODEL
IN_WIDTH = C_GATE + 3 * D_MODEL

P_GATE = 0
P_HY = 3 * D_MODEL
P_Q = P_HY + 3 * HY_WIDTH
P_POOL = P_Q + Q_W
P_K = P_POOL + D_MODEL
P_V = P_K + KV_W

Q_SCALE = HEAD_DIM ** -0.5 * math.log2(math.e)
F8 = jnp.float8_e4m3fn
QK_SHIFT = 4.0

ATT_TQ = 256
ATT_TK = 768
FFT_CT = 256
FFT_UNROLL = 16
KF_UNROLL = 8
ROW_SPLIT = 2

VMEM_LIMIT = 60 * 1024 * 1024
FFT_INNER = 128
LANES = 128
MXU_DIM = 256


def _cparams(n_axes):
    return pltpu.CompilerParams(dimension_semantics=("arbitrary",) * n_axes, vmem_limit_bytes=VMEM_LIMIT)


def _const_spec(shape, n_grid):
    nd = len(shape)
    return pl.BlockSpec(shape, lambda *g, _nd=nd: (0,) * _nd)


def _layer_norm(r, g, b):
    mu = jnp.mean(r, axis=-1, keepdims=True)
    d = r - mu
    var = jnp.mean(d * d, axis=-1, keepdims=True)
    return d * lax.rsqrt(var + EPS) * g + b


def _ada_body(c_ref, w_ref, b_ref, o_ref):
    c = c_ref[...]
    s = c * jax.nn.sigmoid(c)
    o_ref[...] = jnp.dot(s, w_ref[...], preferred_element_type=F32, precision=HIGHEST) + b_ref[...]


def _ada(c_all, w_ada, b_ada):
    rows, d = c_all.shape
    n = w_ada.shape[1]
    tn = 512
    return pl.pallas_call(
        _ada_body,
        out_shape=jax.ShapeDtypeStruct((rows, n), F32),
        grid=(n // tn,),
        in_specs=[pl.BlockSpec((rows, d), lambda j: (0, 0)),
                  pl.BlockSpec((d, tn), lambda j: (0, j)),
                  pl.BlockSpec((1, tn), lambda j: (0, j))],
        out_specs=pl.BlockSpec((rows, tn), lambda j: (0, j)),
        compiler_params=_cparams(1),
        name="ada_mod",
    )(c_all, w_ada, b_ada.reshape(1, n))


def _norm_rope(y, g, cos_t, sin_t, scale):
    ms = jnp.mean(y * y, axis=-1, keepdims=True)
    y = y * lax.rsqrt(ms + EPS) * g
    lane = lax.broadcasted_iota(jnp.int32, y.shape, 1)
    up = pltpu.roll(y, HEAD_DIM - 32, 1)
    dn = pltpu.roll(y, 32, 1)
    partner = jnp.where((lane % 64) < 32, up, dn)
    out = y * cos_t + partner * sin_t
    if scale != 1.0:
        out = out * scale
    return out


def _inproj_body(tn, x_ref, mod_ref, w_ref, cos_ref, sin_ref, gq_ref, gk_ref, o_ref):
    h = (x_ref[...] * (1.0 + mod_ref[1:2, :]) + mod_ref[0:1, :]).astype(BF)
    chunks = list(range(w_ref.shape[1] // tn))
    with_epilogue = [j for j in chunks if j * tn < P_Q + Q_W and (j + 1) * tn > P_Q or j * tn < P_K + KV_W and (j + 1) * tn > P_K]
    for j in with_epilogue + [j for j in chunks if j not in with_epilogue]:
        c0 = j * tn
        r = jnp.dot(h, w_ref[:, c0:c0 + tn], preferred_element_type=F32)
        heads = []
        for hh in range(tn // HEAD_DIM):
            col = c0 + hh * HEAD_DIM
            y = r[:, hh * HEAD_DIM:(hh + 1) * HEAD_DIM]
            if P_Q <= col < P_Q + Q_W:
                y = _norm_rope(y, gq_ref[...], cos_ref[...], sin_ref[...], Q_SCALE * QK_SHIFT)
            elif P_K <= col < P_K + KV_W:
                y = _norm_rope(y, gk_ref[...], cos_ref[...], sin_ref[...], 1.0 / QK_SHIFT)
            heads.append(y)
        o_ref[:, c0:c0 + tn] = jnp.concatenate(heads, axis=1).astype(BF)


def _in_proj(x, mod, w_in_bf, cos_t, sin_t, gq, gk):
    b, l, d = x.shape
    n = w_in_bf.shape[1]
    tm = min(l, 512)
    tn = 512
    vec = pl.BlockSpec((1, HEAD_DIM), lambda bi, i: (0, 0))
    tab = pl.BlockSpec((tm, HEAD_DIM), lambda bi, i: (i, 0))
    return pl.pallas_call(
        functools.partial(_inproj_body, tn),
        out_shape=jax.ShapeDtypeStruct((b, l, n), BF),
        grid=(b, l // tm),
        in_specs=[pl.BlockSpec((None, tm, d), lambda bi, i: (bi, i, 0)),
                  pl.BlockSpec((None, 6, d), lambda bi, i: (bi, 0, 0)),
                  pl.BlockSpec((d, n), lambda bi, i: (0, 0), pipeline_mode=pl.Buffered(1)),
                  tab, tab, vec, vec],
        out_specs=pl.BlockSpec((None, tm, n), lambda bi, i: (bi, i, 0)),
        compiler_params=_cparams(2),
        name="in_proj",
    )(x, mod, w_in_bf, cos_t, sin_t, gq.reshape(1, HEAD_DIM), gk.reshape(1, HEAD_DIM))


def _rope_tables(l, with_positions):
    quarter = HEAD_DIM // 4
    if not with_positions:
        return jnp.ones((l, HEAD_DIM), F32), jnp.zeros((l, HEAD_DIM), F32)
    t = jnp.arange(l)
    rows = (t // GRID_W).astype(F32)
    cols = (t % GRID_W).astype(F32)
    inv = jnp.power(ROPE_THETA, -jnp.arange(quarter, dtype=F32) / quarter)
    ar = rows[:, None] * inv[None, :]
    ac = cols[:, None] * inv[None, :]
    cos_t = jnp.concatenate([jnp.cos(ar), jnp.cos(ar), jnp.cos(ac), jnp.cos(ac)], axis=-1)
    sin_t = jnp.concatenate([-jnp.sin(ar), jnp.sin(ar), -jnp.sin(ac), jnp.sin(ac)], axis=-1)
    return cos_t, sin_t


def _flash_body(tk, q_ref, kt_ref, v_ref, o_ref, qs_scr, sa_scr, sb_scr, mpa_scr, mpb_scr, m_scr, accl_scr):
    tq = q_ref.shape[0]
    nk = kt_ref.shape[1] // tk
    for h in range(GQA_GROUP):
        qs_scr[h * tq:(h + 1) * tq, :] = q_ref[:, h * HEAD_DIM:(h + 1) * HEAD_DIM].astype(F8)
    m_scr[...] = jnp.full(m_scr.shape, -jnp.inf, F32)
    accl_scr[...] = jnp.zeros(accl_scr.shape, F32)

    def scores(i, s_scr, mp_scr):
        c0 = pl.multiple_of(i * tk, tk)
        s = jnp.dot(qs_scr[...], kt_ref[:, pl.ds(c0, tk)].astype(F8), preferred_element_type=F32)
        s_scr[...] = s
        mp = s[:, :HEAD_DIM]
        for t in range(1, tk // HEAD_DIM):
            mp = jnp.maximum(mp, s[:, t * HEAD_DIM:(t + 1) * HEAD_DIM])
        mp_scr[...] = mp

    def consume(i, s_scr, mp_scr):
        r0 = pl.multiple_of(i * tk, tk)
        v = v_ref[pl.ds(r0, tk), :]
        vext = jnp.concatenate([v, jnp.ones_like(v)], axis=1)
        m_prev = m_scr[...]
        m_new = jnp.maximum(m_prev, jnp.max(mp_scr[...], axis=-1, keepdims=True))
        alpha = jnp.exp2(m_prev - m_new)
        p = jnp.concatenate([jnp.exp2(s_scr[:, t * HEAD_DIM:(t + 1) * HEAD_DIM] - m_new)
                             for t in range(tk // HEAD_DIM)], axis=1).astype(BF)
        upd = jnp.dot(p, vext, preferred_element_type=F32)
        accl_scr[...] = jnp.concatenate([alpha, alpha], axis=1) * accl_scr[...] + upd
        m_scr[...] = m_new

    scores(0, sa_scr, mpa_scr)

    def pair(j, carry):
        scores(2 * j + 1, sb_scr, mpb_scr)
        consume(2 * j, sa_scr, mpa_scr)
        scores(2 * j + 2, sa_scr, mpa_scr)
        consume(2 * j + 1, sb_scr, mpb_scr)
        return carry

    lax.fori_loop(0, (nk - 1) // 2, pair, 0, unroll=True)
    if nk % 2 == 1:
        consume(nk - 1, sa_scr, mpa_scr)
    else:
        scores(nk - 1, sb_scr, mpb_scr)
        consume(nk - 2, sa_scr, mpa_scr)
        consume(nk - 1, sb_scr, mpb_scr)
    for h in range(GQA_GROUP):
        rows = slice(h * tq, (h + 1) * tq)
        o_ref[:, h * HEAD_DIM:(h + 1) * HEAD_DIM] = (accl_scr[rows, :HEAD_DIM] / accl_scr[rows, HEAD_DIM:]).astype(BF)


def _attention(p, kt, v):
    b, lq, _ = p.shape
    lk = kt.shape[3]
    tq = min(lq, ATT_TQ)
    tk = ATT_TK if lk % ATT_TK == 0 else 256
    gw = GQA_GROUP * HEAD_DIM
    m = GQA_GROUP * tq
    return pl.pallas_call(
        functools.partial(_flash_body, tk),
        out_shape=jax.ShapeDtypeStruct((b, lq, Q_W), BF),
        grid=(b, N_KV_HEADS, lq // tq),
        in_specs=[pl.BlockSpec((None, tq, gw), lambda bi, g, i: (bi, i, P_Q // gw + g)),
                  pl.BlockSpec((None, None, HEAD_DIM, lk), lambda bi, g, i: (bi, g, 0, 0)),
                  pl.BlockSpec((None, lk, HEAD_DIM), lambda bi, g, i: (bi, 0, g))],
        out_specs=pl.BlockSpec((None, tq, gw), lambda bi, g, i: (bi, i, g)),
        scratch_shapes=[pltpu.VMEM((m, HEAD_DIM), F8),
                        pltpu.VMEM((m, tk), F32),
                        pltpu.VMEM((m, tk), F32),
                        pltpu.VMEM((m, HEAD_DIM), F32),
                        pltpu.VMEM((m, HEAD_DIM), F32),
                        pltpu.VMEM((m, HEAD_DIM), F32),
                        pltpu.VMEM((m, 2 * HEAD_DIM), F32)],
        compiler_params=_cparams(3),
        name="gqa_attention",
    )(p, kt, v)


def _pack(lo, hi):
    ul = lax.bitcast_convert_type(lo.astype(BF).astype(F32), jnp.uint32)
    uh = lax.bitcast_convert_type(hi.astype(BF).astype(F32), jnp.uint32)
    return lax.bitcast_convert_type(uh | (ul >> 16), jnp.int32)


def _unpack_lo(x):
    return lax.bitcast_convert_type(lax.bitcast_convert_type(x, jnp.uint32) << 16, F32)


def _unpack_hi(x):
    return lax.bitcast_convert_type(lax.bitcast_convert_type(x, jnp.uint32) & jnp.uint32(0xFFFF0000), F32)


def _hypre_body(rc, uv_ref, u0_ref, u1_ref, wv_ref, w0_ref, w1_ref, bv_ref, b0_ref, b1_ref, z_ref, x0_ref):
    l = uv_ref.shape[1]
    n_chunks = l // rc

    def conv(u_ref, half, r0, w_ref, b_ref):
        cur = u_ref[half, pl.ds(r0, rc), :].astype(F32)
        p0 = jnp.maximum(r0 - 16, 0)
        n0 = jnp.minimum(r0 + rc, l - 16)
        prev_row = u_ref[half, pl.ds(pl.multiple_of(p0, 16), 16), :].astype(F32)[15:16]
        next_row = u_ref[half, pl.ds(pl.multiple_of(n0, 16), 16), :].astype(F32)[0:1]
        prev_row = jnp.where(r0 > 0, prev_row, 0.0)
        next_row = jnp.where(r0 + rc < l, next_row, 0.0)
        row = lax.broadcasted_iota(jnp.int32, cur.shape, 0)
        x_prev = jnp.where(row == 0, prev_row, pltpu.roll(cur, 1, 0))
        x_next = jnp.where(row == rc - 1, next_row, pltpu.roll(cur, rc - 1, 0))
        return b_ref[...] + x_prev * w_ref[0:1, :] + cur * w_ref[1:2, :] + x_next * w_ref[2:3, :]

    blk, pitch = _seq_pitch(l)

    def chunk(i, carry):
        r0 = pl.multiple_of(i * rc, rc)
        zs = []
        for half in range(2):
            v = conv(uv_ref, half, r0, wv_ref, bv_ref)
            x1 = conv(u1_ref, half, r0, w1_ref, b1_ref)
            x0 = conv(u0_ref, half, r0, w0_ref, b0_ref)
            x0_ref[half, pl.ds(r0, rc), :] = x0.astype(BF)
            zs.append(v * x1)
        packed = _pack(zs[0], zs[1])
        for j in range(rc // blk):
            dst = pl.multiple_of((i * (rc // blk) + j) * pitch, 8)
            z_ref[pl.ds(dst, blk), :] = packed[j * blk:(j + 1) * blk]
            if pitch > blk:
                z_ref[pl.ds(dst + blk, pitch - blk), :] = jnp.zeros((pitch - blk, packed.shape[1]), jnp.int32)
        return carry

    lax.fori_loop(0, n_chunks, chunk, 0)


def _seq_pitch(l):
    n1, n2 = _split(l)
    return (n2, n2 + 8) if n1 > 1 else (l, l)


def _hyena_pre(p, conv_w, conv_b):
    b, l, n = p.shape
    hp = b // 2
    tc = LANES
    rc = min(l, 512)
    blk, pitch = _seq_pitch(l)
    rows_p = l // blk * pitch
    p4 = p.reshape(2, hp, l, n)
    c = HY_WIDTH
    nb = c // tc
    base = P_HY // tc

    def u_spec(g):
        return pl.BlockSpec((2, None, l, tc), lambda pi, j, _g=g: (0, pi, 0, base + _g * nb + j))

    def w_spec(g):
        return pl.BlockSpec((3, tc), lambda pi, j, _g=g: (0, _g * nb + j))

    def b_spec(g):
        return pl.BlockSpec((1, tc), lambda pi, j, _g=g: (0, _g * nb + j))

    cb = conv_b.reshape(1, 3 * c)
    z, x0 = pl.pallas_call(
        functools.partial(_hypre_body, rc),
        out_shape=(jax.ShapeDtypeStruct((hp, nb, rows_p, tc), jnp.int32), jax.ShapeDtypeStruct((2, hp, l, c), BF)),
        grid=(hp, nb),
        in_specs=[u_spec(0), u_spec(1), u_spec(2), w_spec(0), w_spec(1), w_spec(2), b_spec(0), b_spec(1), b_spec(2)],
        out_specs=(pl.BlockSpec((None, None, rows_p, tc), lambda pi, j: (pi, j, 0, 0)),
                   pl.BlockSpec((2, None, l, tc), lambda pi, j: (0, pi, 0, j))),
        compiler_params=_cparams(2),
        name="hyena_pre",
    )(p4, p4, p4, conv_w, conv_w, conv_w, cb, cb, cb)
    return z, x0.reshape(b, l, c)


def _filter_body(l, tr, f_ref, w1_ref, b1_ref, fr_ref, w2_ref, b2_ref, w3_ref, dl_ref, k_ref, ss_ref):
    i = pl.program_id(0)
    feats = f_ref[...]
    freq = fr_ref[...]
    h = jnp.sin(freq * (jnp.dot(feats, w1_ref[...], preferred_element_type=F32, precision=HIGHEST) + b1_ref[...]))
    h = jnp.sin(freq * (jnp.dot(h, w2_ref[...], preferred_element_type=F32, precision=HIGHEST) + b2_ref[...]))
    h = jnp.dot(h, w3_ref[...], preferred_element_type=F32, precision=HIGHEST)
    t01 = feats[:, 0:1]
    window = jnp.exp(-t01 * dl_ref[...]) + HY_SHIFT
    row = i * tr + lax.broadcasted_iota(jnp.int32, h.shape, 0)
    kern = jnp.where(row == l, 0.0, h * window)
    k_ref[...] = kern

    @pl.when(i == 0)
    def _():
        ss_ref[...] = jnp.zeros_like(ss_ref)

    ss_ref[...] += jnp.sum(kern * kern, axis=0, keepdims=True)


def _hyena_filter(l, w1, b1, freq, w2, b2, w3):
    n = 2 * l
    c = HY_WIDTH
    hid = HY_FILTER_HIDDEN
    tr = min(l, 512)
    j = jnp.arange(n)
    lag = jnp.where(j < l, j, n - j).astype(F32)
    t01 = lag / max(l - 1, 1)
    bands = jnp.linspace(1e-4, HY_BANDS - 1, HY_BANDS, dtype=F32)
    ang = (2.0 * math.pi / l) * lag[:, None] * bands[None, :]
    feats = jnp.concatenate([t01[:, None], jnp.cos(ang), -jnp.sin(ang)], axis=-1)
    feats = jnp.pad(feats, ((0, 0), (0, hid - HY_EMB)))
    w1p = jnp.pad(w1, ((0, hid - HY_EMB), (0, 0)))
    deltas = jnp.abs(jnp.linspace(HY_MIN_DECAY, HY_MAX_DECAY, c, dtype=F32)).reshape(1, c)
    nt = l // tr
    return pl.pallas_call(
        functools.partial(_filter_body, l, tr),
        out_shape=(jax.ShapeDtypeStruct((n, c), F32), jax.ShapeDtypeStruct((1, c), F32)),
        grid=(n // tr,),
        in_specs=[pl.BlockSpec((tr, hid), lambda i: (i, 0)),
                  pl.BlockSpec((hid, hid), lambda i: (0, 0)),
                  pl.BlockSpec((1, hid), lambda i: (0, 0)),
                  pl.BlockSpec((1, hid), lambda i: (0, 0)),
                  pl.BlockSpec((hid, hid), lambda i: (0, 0)),
                  pl.BlockSpec((1, hid), lambda i: (0, 0)),
                  pl.BlockSpec((hid, c), lambda i: (0, i // nt)),
                  pl.BlockSpec((1, c), lambda i: (0, 0))],
        out_specs=(pl.BlockSpec((tr, c), lambda i: (i, 0)),
                   pl.BlockSpec((1, c), lambda i: (0, 0))),
        compiler_params=_cparams(1),
        name="hyena_filter",
    )(feats, w1p, b1.reshape(1, hid), freq.reshape(1, hid), w2, b2.reshape(1, hid), w3, deltas)


def _split(l):
    n = 2 * l
    n1 = n // FFT_INNER if n > 4 * FFT_INNER else 1
    return n1, n // n1


def _interleave_cols(a, b):
    r, k = a.shape
    return np.stack([a, b], axis=-1).reshape(r, 2 * k)


@functools.lru_cache(maxsize=None)
def _fft_tables(l):
    n1, n2 = _split(l)
    n = 2 * l
    f64 = np.float64
    t = {}
    n2in = n2 if n1 > 1 else l
    k2 = np.arange(n2, dtype=f64)[:, None]
    t2 = np.arange(n2in, dtype=f64)[None, :]
    ang2 = -2.0 * np.pi * k2 * t2 / n2
    f2re, f2im = np.cos(ang2), np.sin(ang2)
    t['cre2'] = _interleave_cols(f2re, f2re)
    t['cim2'] = _interleave_cols(f2im, f2im)
    k1 = np.arange(n1, dtype=f64)[:, None]
    angw = -2.0 * np.pi * k1 * np.arange(n2in, dtype=f64)[None, :] / n
    wre, wim = np.cos(angw), np.sin(angw)
    t['w4'] = np.stack([_interleave_cols(wre, -wim), _interleave_cols(-wim, -wre),
                        _interleave_cols(wim, wre), _interleave_cols(wre, -wim)], axis=1)
    n2out = n2 if n1 > 1 else l
    tt = np.arange(n2out, dtype=f64)[:, None]
    kk = np.arange(n2, dtype=f64)[None, :]
    angc = -2.0 * np.pi * tt * kk / n2
    cr, ci = np.cos(angc), np.sin(angc)
    t['gc'] = np.block([[cr, ci], [-ci, cr]])
    k2f = np.arange(n2, dtype=f64)[:, None]
    t2f = np.arange(n2, dtype=f64)[None, :]
    angf = -2.0 * np.pi * k2f * t2f / n2
    t['f2re'], t['f2im'] = np.cos(angf), np.sin(angf)
    angwf = -2.0 * np.pi * k1 * np.arange(n2, dtype=f64)[None, :] / n
    t['w2'] = np.stack([np.cos(angwf), np.sin(angwf)], axis=1)
    if n1 > 1:
        h = n1 // 2
        kk1 = np.arange(n1, dtype=f64)[:, None]
        ang1 = -2.0 * np.pi * kk1 * np.arange(h, dtype=f64)[None, :] / n1
        f1re, f1im = np.cos(ang1), np.sin(ang1)
        t['m1'] = np.concatenate([_interleave_cols(f1re, -f1im), _interleave_cols(f1im, f1re)], axis=0)
        d1re, d1im = f1re.T, f1im.T
        t['d1re2'] = _interleave_cols(d1re, d1re)
        t['d1im2'] = _interleave_cols(d1im, d1im)
        wre_t, wim_t = wre.T, wim.T
        t['v4'] = np.stack([_interleave_cols(wre_t, wim_t), _interleave_cols(-wim_t, wre_t),
                            _interleave_cols(-wim_t, wre_t), _interleave_cols(-wre_t, -wim_t)], axis=1)
        ang1f = -2.0 * np.pi * kk1 * np.arange(n1, dtype=f64)[None, :] / n1
        t['m1k'] = np.concatenate([np.cos(ang1f), np.sin(ang1f)], axis=0)
    return {k: np.asarray(v, np.float32) for k, v in t.items()}


def _dot3(a, b):
    ah = a.astype(BF)
    al = (a - ah.astype(F32)).astype(BF)
    bh = b.astype(BF)
    bl = (b - bh.astype(F32)).astype(BF)
    return jnp.dot(jnp.concatenate([ah, ah, al], axis=1), jnp.concatenate([bh, bl, bh], axis=0),
                   preferred_element_type=F32)


def _kfft_body(l, *refs):
    n1, n2 = _split(l)
    n = 2 * l
    if n1 > 1:
        k_ref, ss_ref, m1k_ref, f2re_ref, f2im_ref, w2_ref, o_ref, are_scr, aim_scr = refs
    else:
        k_ref, ss_ref, f2re_ref, f2im_ref, o_ref = refs
    scale = lax.rsqrt(ss_ref[...] + EPS) * (1.0 / n)

    if n1 > 1:
        pitch = are_scr.shape[0] // n1

        def stage1(t2, carry):
            xin = k_ref[pl.ds(t2, n1, stride=n2), :]
            o = _dot3(m1k_ref[...], xin)
            are_scr[pl.ds(t2, n1, stride=pitch), :] = o[:n1]
            aim_scr[pl.ds(t2, n1, stride=pitch), :] = o[n1:]
            return carry

        lax.fori_loop(0, n2, stage1, 0, unroll=KF_UNROLL)

        def slab(k1, carry):
            r0 = pl.multiple_of(k1 * pitch, 8)
            s = jnp.concatenate([are_scr[pl.ds(r0, n2), :], aim_scr[pl.ds(r0, n2), :]], axis=0)
            w2 = w2_ref[k1]
            wre, wim = w2[0:1], w2[1:2]
            hre = f2re_ref[...] * wre - f2im_ref[...] * wim
            him = f2re_ref[...] * wim + f2im_ref[...] * wre
            hblk = jnp.concatenate([jnp.concatenate([hre, -him], axis=1),
                                    jnp.concatenate([him, hre], axis=1)], axis=0)
            x = _dot3(hblk, s)
            o_ref[k1] = _pack(x[:n2] * scale, x[n2:] * scale)
            return carry

        lax.fori_loop(0, n1, slab, 0, unroll=KF_UNROLL)
    else:
        f = jnp.concatenate([f2re_ref[...], f2im_ref[...]], axis=0)
        x = _dot3(f, k_ref[...])
        o_ref[0] = _pack(x[:n2] * scale, x[n2:] * scale)


def _kernel_spectrum(l, kern, sumsq):
    n1, n2 = _split(l)
    n = 2 * l
    c = kern.shape[1]
    ct = 128
    tb = _fft_tables(l)
    names = (['m1k'] if n1 > 1 else []) + ['f2re', 'f2im'] + (['w2'] if n1 > 1 else [])
    consts = [jnp.asarray(tb[k]) for k in names]
    scratch = [pltpu.VMEM((n1 * (n2 + 8), ct), F32)] * 2 if n1 > 1 else []
    return pl.pallas_call(
        functools.partial(_kfft_body, l),
        out_shape=jax.ShapeDtypeStruct((n1, n2, c), jnp.int32),
        grid=(c // ct,),
        in_specs=[pl.BlockSpec((n, ct), lambda j: (0, j), pipeline_mode=pl.Buffered(1)),
                  pl.BlockSpec((1, ct), lambda j: (0, j))] + [_const_spec(a.shape, 1) for a in consts],
        out_specs=pl.BlockSpec((n1, n2, ct), lambda j: (0, 0, j)),
        scratch_shapes=scratch,
        compiler_params=_cparams(1),
        name="hyena_filter_fft",
    )(kern, sumsq, *consts)


def _fft_body(l, *refs):
    n1, n2 = _split(l)
    if n1 > 1:
        (z_ref, kf_ref, cre2_ref, cim2_ref, w4_ref, gc_ref, m1_ref, d1re2_ref, d1im2_ref, v4_ref,
         out_ref, a_scr) = refs
        h = n1 // 2
        ctn = z_ref.shape[0]
        a_pitch = a_scr.shape[1] // n1
        io_pitch = _seq_pitch(l)[1]

        def load(ref, rows):
            return jnp.concatenate([ref[c, rows, :] for c in range(ctn)], axis=1)

        def store(ref, rows, val):
            for c in range(ctn):
                ref[c, rows, :] = val[:, c * LANES:(c + 1) * LANES]

        def stage1(t2, carry):
            xin = load(z_ref, pl.ds(t2, h, stride=io_pitch))
            xb = pltpu.bitcast(xin, BF)
            o = jnp.dot(m1_ref[...], xb, preferred_element_type=F32)
            store(a_scr, pl.ds(t2, n1, stride=a_pitch), _pack(o[:n1], o[n1:]))
            return carry

        lax.fori_loop(0, n2, stage1, 0, unroll=FFT_UNROLL)

        def slab(k1, carry):
            r0 = pl.multiple_of(k1 * a_pitch, 8)
            s = pltpu.bitcast(load(a_scr, pl.ds(r0, n2)), BF)
            w4 = w4_ref[k1]
            cre2 = cre2_ref[...]
            cim2 = cim2_ref[...]
            top = cre2 * w4[0:1] + cim2 * w4[1:2]
            bot = cre2 * w4[2:3] + cim2 * w4[3:4]
            hblk = jnp.concatenate([top, bot], axis=0).astype(BF)
            x = jnp.dot(hblk, s, preferred_element_type=F32)
            kf = kf_ref[k1]
            kre, kim = _unpack_lo(kf), _unpack_hi(kf)
            xre, xim = x[:n2], x[n2:]
            y = jnp.concatenate([xre * kre - xim * kim, xre * kim + xim * kre], axis=0).astype(BF)
            b = jnp.dot(gc_ref[...], y, preferred_element_type=F32)
            store(a_scr, pl.ds(r0, n2), _pack(b[:n2], b[n2:]))
            return carry

        lax.fori_loop(0, n1, slab, 0, unroll=FFT_UNROLL)

        def stage3(t2, carry):
            bin_ = pltpu.bitcast(load(a_scr, pl.ds(t2, n1, stride=a_pitch)), BF)
            v4 = v4_ref[t2]
            d1re2 = d1re2_ref[...]
            d1im2 = d1im2_ref[...]
            top = d1re2 * v4[0:1] + d1im2 * v4[1:2]
            bot = d1re2 * v4[2:3] + d1im2 * v4[3:4]
            m3 = jnp.concatenate([top, bot], axis=0).astype(BF)
            o = jnp.dot(m3, bin_, preferred_element_type=F32)
            store(out_ref, pl.ds(t2, h, stride=io_pitch), _pack(o[:h], o[h:]))
            return carry

        lax.fori_loop(0, n2, stage3, 0, unroll=FFT_UNROLL)
        for r in range(n2, io_pitch):
            store(out_ref, pl.ds(r, h, stride=io_pitch), jnp.zeros((h, ctn * LANES), jnp.int32))
    else:
        z_ref, kf_ref, cre2_ref, cim2_ref, w4_ref, gc_ref, out_ref = refs
        ctn = z_ref.shape[0]
        s = pltpu.bitcast(jnp.concatenate([z_ref[c] for c in range(ctn)], axis=1), BF)
        w4 = w4_ref[0]
        top = cre2_ref[...] * w4[0:1] + cim2_ref[...] * w4[1:2]
        bot = cre2_ref[...] * w4[2:3] + cim2_ref[...] * w4[3:4]
        hblk = jnp.concatenate([top, bot], axis=0).astype(BF)
        x = jnp.dot(hblk, s, preferred_element_type=F32)
        kf = kf_ref[0]
        kre, kim = _unpack_lo(kf), _unpack_hi(kf)
        xre, xim = x[:n2], x[n2:]
        y = jnp.concatenate([xre * kre - xim * kim, xre * kim + xim * kre], axis=0).astype(BF)
        b = jnp.dot(gc_ref[...], y, preferred_element_type=F32)
        packed = _pack(b[:l], b[l:])
        for c in range(ctn):
            out_ref[c] = packed[:, c * LANES:(c + 1) * LANES]


def _fft_conv(zp, kf):
    hp, nct, rows_p, _ = zp.shape
    n1, n2 = kf.shape[0], kf.shape[1]
    l = n1 * n2 // 2
    assert rows_p == l // _seq_pitch(l)[0] * _seq_pitch(l)[1]
    ct = FFT_CT
    ctn = ct // LANES
    tb = _fft_tables(l)
    names = ['cre2', 'cim2', 'w4', 'gc'] + (['m1', 'd1re2', 'd1im2', 'v4'] if n1 > 1 else [])
    consts = [jnp.asarray(tb[k]).astype(BF) if k in ('gc', 'm1') else jnp.asarray(tb[k]) for k in names]
    scratch = [pltpu.VMEM((ctn, n1 * (n2 + 8), LANES), jnp.int32)] if n1 > 1 else []
    one = dict(pipeline_mode=pl.Buffered(1))
    return pl.pallas_call(
        functools.partial(_fft_body, l),
        out_shape=jax.ShapeDtypeStruct((hp, nct, rows_p, LANES), jnp.int32),
        grid=(nct // ctn, hp),
        in_specs=[pl.BlockSpec((None, ctn, rows_p, LANES), lambda j, p: (p, j, 0, 0), **one),
                  pl.BlockSpec((n1, n2, ct), lambda j, p: (0, 0, j), **one)] + [_const_spec(a.shape, 2) for a in consts],
        out_specs=pl.BlockSpec((None, ctn, rows_p, LANES), lambda j, p: (p, j, 0, 0), **one),
        scratch_shapes=scratch,
        compiler_params=_cparams(2),
        name="hyena_fft_conv",
    )(zp, kf, *consts)


POOL_HALO = 64


@functools.lru_cache(maxsize=None)
def _pool_bands(rows):
    off = np.arange(rows + 2 * POOL_HALO)[None, :] - POOL_HALO - np.arange(rows)[:, None]
    bands = []
    for w in POOL_WINDOWS:
        before = w // 2
        after = w - 1 - before
        bands.append(((off >= -before) & (off <= after)).astype(np.float32))
    return np.stack(bands)


def _pool_rows(ext, cur, t0, l, band_ref, pw_ref, ps_ref):
    rows = cur.shape[0]
    outs = []
    for g, w in enumerate(POOL_WINDOWS):
        before = w // 2
        after = w - 1 - before
        cols = slice(g * POOL_GROUP, (g + 1) * POOL_GROUP)
        sums = jnp.dot(band_ref[g], ext[:, cols], preferred_element_type=F32)
        t = t0 + lax.broadcasted_iota(jnp.int32, (rows, POOL_GROUP), 0)
        cnt = jnp.minimum(t + after + 1, l) - jnp.maximum(t - before, 0)
        m = sums / cnt.astype(F32) - cur[:, cols].astype(F32)
        outs.append(jnp.dot(m.astype(BF), pw_ref[g], preferred_element_type=F32) * ps_ref[:, cols])
    return jnp.concatenate(outs, axis=1).astype(BF)


def _mix_body(hp, blk, pitch, l, x_ref, gate_ref, y_ref, z_ref, x0_ref, hd_ref, at_ref, u_ref, up_ref, un_ref,
              band_ref, pw_ref, ps_ref, wb_ref, wo_ref, mod_ref, g_ref, b_ref, o_ref):
    hi = pl.program_id(0) >= hp
    i = pl.program_id(1)
    nct = y_ref.shape[0]
    d = D_MODEL
    tm = x_ref.shape[0]
    sub = tm // ROW_SPLIT
    prev = jnp.where(i > 0, up_ref[...], jnp.zeros(up_ref.shape, BF))
    nxt = jnp.where((i + 1) * tm < l, un_ref[...], jnp.zeros(un_ref.shape, BF))
    ext_all = jnp.concatenate([prev, u_ref[...], nxt], axis=0)

    def signal_rows(ref, start, size):
        if pitch == blk:
            return jnp.concatenate([ref[c, start:start + size, :] for c in range(nct)], axis=1)
        assert start % blk == 0 and size % blk == 0
        return jnp.concatenate(
            [jnp.concatenate([ref[c, j * pitch:j * pitch + blk, :] for c in range(nct)], axis=1)
             for j in range(start // blk, (start + size) // blk)], axis=0)

    for s in range(ROW_SPLIT):
        rows = slice(s * sub, (s + 1) * sub)
        yw = signal_rows(y_ref, s * sub, sub)
        zw = signal_rows(z_ref, s * sub, sub)
        y = jnp.where(hi, _unpack_hi(yw), _unpack_lo(yw))
        z = jnp.where(hi, _unpack_hi(zw), _unpack_lo(zw))
        hy = ((y + z * hd_ref[...]) * x0_ref[rows, :].astype(F32)).astype(BF)
        merged = jax.nn.sigmoid(gate_ref[rows, 0:d].astype(F32)) * jnp.dot(hy, wb_ref[0], preferred_element_type=F32)
        merged += jax.nn.sigmoid(gate_ref[rows, d:2 * d].astype(F32)) * jnp.dot(at_ref[rows, :], wb_ref[1],
                                                                                preferred_element_type=F32)
        pooled = _pool_rows(ext_all[s * sub:(s + 1) * sub + 2 * POOL_HALO], u_ref[rows, :], i * tm + s * sub, l,
                            band_ref, pw_ref, ps_ref)
        merged += jax.nn.sigmoid(gate_ref[rows, 2 * d:].astype(F32)) * jnp.dot(pooled, wb_ref[2],
                                                                               preferred_element_type=F32)
        out = jnp.dot(merged.astype(BF), wo_ref[...], preferred_element_type=F32)
        r = DN_ALPHA * x_ref[rows, :] + mod_ref[2:3, :] * out
        o_ref[rows, :] = _layer_norm(r, g_ref[...], b_ref[...])


def _mix(x, p, y_pair, z_pair, x0c, hy_d, attn, pool_w_bf, pool_scale, wb_bf, wo_bf, mod, ln_g, ln_b):
    b, l, d = x.shape
    hp = b // 2
    tm = min(l, 512)
    row = lambda bi, i: (bi, i, 0)
    pair = lambda bi, i: (bi % hp, 0, i, 0)
    nct = d // LANES
    blk, pitch = _seq_pitch(l)
    tm_p = tm // blk * pitch
    ng = len(POOL_WINDOWS)
    hb = tm // POOL_HALO
    n_hb = l // POOL_HALO
    pool_col = P_POOL // d
    bands = jnp.asarray(_pool_bands(tm // ROW_SPLIT), BF)
    return pl.pallas_call(
        functools.partial(_mix_body, hp, blk, pitch, l),
        out_shape=jax.ShapeDtypeStruct((b, l, d), F32),
        grid=(b, l // tm),
        in_specs=[pl.BlockSpec((None, tm, d), row),
                  pl.BlockSpec((None, tm, 3 * d), lambda bi, i: (bi, i, P_GATE // (3 * d))),
                  pl.BlockSpec((None, nct, tm_p, LANES), pair),
                  pl.BlockSpec((None, nct, tm_p, LANES), pair),
                  pl.BlockSpec((None, tm, d), row),
                  pl.BlockSpec((1, d), lambda bi, i: (0, 0)),
                  pl.BlockSpec((None, tm, d), row),
                  pl.BlockSpec((None, tm, d), lambda bi, i: (bi, i, pool_col)),
                  pl.BlockSpec((None, POOL_HALO, d), lambda bi, i: (bi, jnp.maximum(i * hb - 1, 0), pool_col)),
                  pl.BlockSpec((None, POOL_HALO, d), lambda bi, i: (bi, jnp.minimum((i + 1) * hb, n_hb - 1), pool_col)),
                  pl.BlockSpec(bands.shape, lambda bi, i: (0, 0, 0)),
                  pl.BlockSpec((ng, POOL_GROUP, POOL_GROUP), lambda bi, i: (0, 0, 0)),
                  pl.BlockSpec((1, d), lambda bi, i: (0, 0)),
                  pl.BlockSpec((3, d, d), lambda bi, i: (0, 0, 0), pipeline_mode=pl.Buffered(1)),
                  pl.BlockSpec((d, d), lambda bi, i: (0, 0), pipeline_mode=pl.Buffered(1)),
                  pl.BlockSpec((None, 6, d), lambda bi, i: (bi, 0, 0)),
                  pl.BlockSpec((1, d), lambda bi, i: (0, 0)),
                  pl.BlockSpec((1, d), lambda bi, i: (0, 0))],
        out_specs=pl.BlockSpec((None, tm, d), row),
        compiler_params=_cparams(2),
        name="branch_mix",
    )(x, p, y_pair, z_pair, x0c, hy_d.reshape(1, d), attn, p, p, p, bands, pool_w_bf, pool_scale.reshape(1, d),
      wb_bf, wo_bf, mod, ln_g.reshape(1, d), ln_b.reshape(1, d))


def _ffn_body(n_chunks, x_ref, mod_ref, w1_ref, w3_ref, w2_ref, g_ref, b_ref, o_ref):
    tiles = D_FF // MXU_DIM
    edges = [MXU_DIM * ((tiles * c) // n_chunks) for c in range(n_chunks + 1)]
    sub = x_ref.shape[0] // ROW_SPLIT
    for s in range(ROW_SPLIT):
        rows = slice(s * sub, (s + 1) * sub)
        x = x_ref[rows, :]
        h = (x * (1.0 + mod_ref[4:5, :]) + mod_ref[3:4, :]).astype(BF)
        acc = jnp.zeros(x.shape, F32)
        for c in range(n_chunks):
            sl = slice(edges[c], edges[c + 1])
            a = jnp.dot(h, w1_ref[:, sl], preferred_element_type=F32)
            bb = jnp.dot(h, w3_ref[:, sl], preferred_element_type=F32)
            gg = (a * jax.nn.sigmoid(a) * bb).astype(BF)
            acc += jnp.dot(gg, w2_ref[sl, :], preferred_element_type=F32)
        r = DN_ALPHA * x + mod_ref[5:6, :] * acc
        o_ref[rows, :] = _layer_norm(r, g_ref[...], b_ref[...])


def _ffn(x, mod, w1_bf, w3_bf, w2_bf, ln_g, ln_b):
    b, l, d = x.shape
    tm = min(l, 512)
    row = lambda bi, i: (bi, i, 0)
    return pl.pallas_call(
        functools.partial(_ffn_body, 2),
        out_shape=jax.ShapeDtypeStruct((b, l, d), F32),
        grid=(b, l // tm),
        in_specs=[pl.BlockSpec((None, tm, d), row),
                  pl.BlockSpec((None, 6, d), lambda bi, i: (bi, 0, 0)),
                  pl.BlockSpec((d, D_FF), lambda bi, i: (0, 0), pipeline_mode=pl.Buffered(1)),
                  pl.BlockSpec((d, D_FF), lambda bi, i: (0, 0), pipeline_mode=pl.Buffered(1)),
                  pl.BlockSpec((D_FF, d), lambda bi, i: (0, 0), pipeline_mode=pl.Buffered(1)),
                  pl.BlockSpec((1, d), lambda bi, i: (0, 0)),
                  pl.BlockSpec((1, d), lambda bi, i: (0, 0))],
        out_specs=pl.BlockSpec((None, tm, d), row),
        compiler_params=_cparams(2),
        name="swiglu_ffn",
    )(x, mod, w1_bf, w3_bf, w2_bf, ln_g.reshape(1, d), ln_b.reshape(1, d))


def _permute_w_in(w):
    return jnp.concatenate([w[:, C_GATE:], w[:, C_HY:C_POOL], w[:, C_Q:C_K], w[:, C_POOL:C_GATE],
                            w[:, C_K:C_V], w[:, C_V:C_HY]], axis=1).astype(BF)


def _v_cols(p):
    return p[..., P_V:P_V + KV_W]


def _k_cols(p):
    return p[..., P_K:P_K + KV_W]


def _keys_transposed(k):
    b, lk, _ = k.shape
    return k.reshape(b, lk, N_KV_HEADS, HEAD_DIM).transpose(0, 2, 3, 1)


def _stream_block(x, p, k_all, v_all, kf, mod, lw):
    attn = _attention(p, _keys_transposed(k_all), v_all)
    z_pair, x0c = _hyena_pre(p, lw['hy_conv_w'], lw['hy_conv_b'])
    y_pair = _fft_conv(z_pair, kf)
    x = _mix(x, p, y_pair, z_pair, x0c, lw['hy_d'], attn, lw['pool_w'], lw['pool_scale'], lw['w_branch'],
             lw['w_out'], mod, lw['ln1_g'], lw['ln1_b'])
    return _ffn(x, mod, lw['ffn_w1'], lw['ffn_w3'], lw['ffn_w2'], lw['ln2_g'], lw['ln2_b'])


def kernel(x, c, ctx, c_ctx, w_ada, b_ada, w_in, q_norm_g, k_norm_g, hy_conv_w, hy_conv_b, hf_w1, hf_b1, hf_freq,
           hf_w2, hf_b2, hf_w3, hy_d, pool_w, pool_scale, w_branch, w_out, ln1_g, ln1_b, ln2_g, ln2_b,
           ffn_w1, ffn_w3, ffn_w2):
    b, l, d = x.shape
    lc = ctx.shape[1]
    depth = w_ada.shape[0]
    assert b % 2 == 0 and l % GRID_W == 0

    rows = 16
    c_all = jnp.zeros((rows, d), F32).at[:b].set(c).at[b].set(c_ctx)
    cos_l, sin_l = _rope_tables(l, True)
    cos_c, sin_c = _rope_tables(lc, False)

    xl, xc = x, ctx
    for li in range(depth):
        last = li == depth - 1
        lw = dict(hy_conv_w=hy_conv_w[li], hy_conv_b=hy_conv_b[li], hy_d=hy_d[li],
                  pool_w=pool_w[li].astype(BF), pool_scale=pool_scale[li],
                  w_branch=w_branch[li].astype(BF), w_out=w_out[li].astype(BF),
                  ln1_g=ln1_g[li], ln1_b=ln1_b[li], ln2_g=ln2_g[li], ln2_b=ln2_b[li],
                  ffn_w1=ffn_w1[li].astype(BF), ffn_w3=ffn_w3[li].astype(BF), ffn_w2=ffn_w2[li].astype(BF))
        w_in_p = _permute_w_in(w_in[li])

        mod = _ada(c_all, w_ada[li], b_ada[li]).reshape(rows, 6, d)
        mod_l = mod[:b]
        mod_c = jnp.broadcast_to(mod[b:b + 1], (b, 6, d))

        filt = (hf_w1[li], hf_b1[li], hf_freq[li], hf_w2[li], hf_b2[li], hf_w3[li])
        kf_l = _kernel_spectrum(l, *_hyena_filter(l, *filt))

        p_l = _in_proj(xl, mod_l, w_in_p, cos_l, sin_l, q_norm_g[li], k_norm_g[li])
        p_c = _in_proj(xc, mod_c, w_in_p, cos_c, sin_c, q_norm_g[li], k_norm_g[li])
        k_c, v_c = _k_cols(p_c), _v_cols(p_c)
        k_all = jnp.concatenate([k_c, _k_cols(p_l)], axis=1)
        v_all = jnp.concatenate([v_c, _v_cols(p_l)], axis=1)

        xl = _stream_block(xl, p_l, k_all, v_all, kf_l, mod_l, lw)
        if not last:
            kf_c = _kernel_spectrum(lc, *_hyena_filter(lc, *filt))
            xc = _stream_block(xc, p_c, k_c, v_c, kf_c, mod_c, lw)
    return xl
```
